```python
import jax, jax.numpy as jnp
from jax import lax
import numpy as np

D_MODEL = 1024
BATCH = 16
SEQ = 4096
DEPTH = 1
DEC_BATCH = 8
DEC_SEQ = 16
PAST_LEN = 4096

CHUNK = 64
N_ATT_HEADS = 8
HEAD_DIM = 64
ATT_DIM = N_ATT_HEADS * HEAD_DIM
N_CONV_GROUPS = 8
CONV_DIM = N_CONV_GROUPS * HEAD_DIM
CONV_WIDTH = 3
D_FF = 4 * D_MODEL
Q_BLOCK = 128
NORM_EPS = 1e-6
PROJ_DIM = 3 * ATT_DIM + N_ATT_HEADS + 3 * CONV_DIM
ATT_SCALE = HEAD_DIM ** -0.5

kernel_name = 'hymba_fox_shortconv_streaming_step'


def rmsnorm(x, g):
    xf = x.astype(jnp.float32)
    y = xf * lax.rsqrt(jnp.mean(xf * xf, axis=-1, keepdims=True) + NORM_EPS)
    return (y * g.astype(jnp.float32)).astype(x.dtype)


def modulate(h, shift, scale):
    return h * (1 + scale[:, None, :]) + shift[:, None, :]


def to_heads(t):
    return t.reshape(t.shape[:-1] + (N_ATT_HEADS, HEAD_DIM))


def mix_inputs(h, w_in, b_f):
    p = h @ w_in
    cuts = [ATT_DIM, 2 * ATT_DIM, 3 * ATT_DIM, 3 * ATT_DIM + N_ATT_HEADS,
            3 * ATT_DIM + N_ATT_HEADS + CONV_DIM, 3 * ATT_DIM + N_ATT_HEADS + 2 * CONV_DIM]
    q, k, v, fl, bg, cg, u = jnp.split(p, cuts, axis=-1)
    logf = jax.nn.log_sigmoid(fl.astype(jnp.float32) + b_f.astype(jnp.float32))
    return to_heads(q), to_heads(k), to_heads(v), logf, bg, cg * u


def fox_prompt(q, k, v, logf):
    B, S = q.shape[0], q.shape[1]
    nb = S // Q_BLOCK
    F = jnp.cumsum(logf, axis=1)
    Fk = F.transpose(0, 2, 1)[:, :, None, :]
    qb = q.reshape(B, nb, Q_BLOCK, N_ATT_HEADS, HEAD_DIM).transpose(1, 0, 2, 3, 4)
    Fqb = F.reshape(B, nb, Q_BLOCK, N_ATT_HEADS).transpose(1, 0, 3, 2)
    kpos = jnp.arange(S)

    def block(args):
        qi, fqi, bi = args
        s = jnp.einsum('bqhd,bkhd->bhqk', qi, k, preferred_element_type=jnp.float32) * ATT_SCALE
        s = s + fqi[..., None] - Fk
        qpos = bi * Q_BLOCK + jnp.arange(Q_BLOCK)
        s = jnp.where(kpos[None, :] <= qpos[:, None], s, -jnp.inf)
        p = jax.nn.softmax(s, axis=-1)
        return jnp.einsum('bhqk,bkhd->bqhd', p.astype(v.dtype), v)

    o = lax.map(block, (qb, Fqb, jnp.arange(nb)))
    return o.transpose(1, 0, 2, 3, 4).reshape(B, S, ATT_DIM)


def fox_sample(q, k, v, logf, ck, cv, clogf):
    B, T = q.shape[0], q.shape[1]
    P = ck.shape[1]
    k_all = jnp.concatenate([ck.astype(k.dtype), k], axis=1)
    v_all = jnp.concatenate([cv.astype(v.dtype), v], axis=1)
    F = jnp.cumsum(jnp.concatenate([clogf.astype(jnp.float32), logf], axis=1), axis=1)
    Fq = F[:, P:].transpose(0, 2, 1)[..., None]
    Fk = F.transpose(0, 2, 1)[:, :, None, :]
    s = jnp.einsum('bqhd,bkhd->bhqk', q, k_all, preferred_element_type=jnp.float32) * ATT_SCALE
    s = s + Fq - Fk
    mask = jnp.arange(P + T)[None, :] <= (P + jnp.arange(T))[:, None]
    s = jnp.where(mask, s, -jnp.inf)
    p = jax.nn.softmax(s, axis=-1)
    o = jnp.einsum('bhqk,bkhd->bqhd', p.astype(v_all.dtype), v_all)
    return o.reshape(B, T, ATT_DIM)


def short_conv(u, prev, w_conv):
    ue = jnp.concatenate([prev.astype(u.dtype), u], axis=1)
    y = lax.conv_general_dilated(ue, w_conv[:, None, :].astype(u.dtype), window_strides=(1,),
                                 padding='VALID', dimension_numbers=('NWC', 'WIO', 'NWC'),
                                 feature_group_count=CONV_DIM)
    return y, ue[:, -(CONV_WIDTH - 1):]


def run_layer(x, c, w_ada, b_ada, g1, g2, w_in, b_f, w_conv, g_att, g_conv, w_out, w_up, w_down,
              attend, conv_prev):
    sh1, sc1, gt1, sh2, sc2, gt2 = jnp.split(jax.nn.silu(c) @ w_ada + b_ada, 6, axis=-1)
    h = modulate(rmsnorm(x, g1), sh1, sc1)
    q, k, v, logf, bg, u = mix_inputs(h, w_in, b_f)
    att = attend(q, k, v, logf)
    cv, conv_state = short_conv(u, conv_prev, w_conv)
    merged = jnp.concatenate([rmsnorm(att, g_att), rmsnorm(bg * cv, g_conv)], axis=-1)
    x = x + gt1[:, None, :] * (merged @ w_out)
    h = modulate(rmsnorm(x, g2), sh2, sc2)
    x = x + gt2[:, None, :] * (jnp.square(jax.nn.relu(h @ w_up)) @ w_down)
    return x, k, v, logf, conv_state


def setup_inputs(seed: int = 0) -> dict:
    key = jax.random.key(seed)
    ks = jax.random.split(key, 24)

    def nrm(k, shape, scale=1.0):
        return jax.random.normal(k, shape, jnp.float32) * scale

    return {
        'x_prompt': nrm(ks[0], (BATCH, SEQ, D_MODEL)),
        'x_sample': nrm(ks[1], (DEC_BATCH, DEC_SEQ, D_MODEL)),
        'cache_k': nrm(ks[2], (DEPTH, DEC_BATCH, PAST_LEN, N_ATT_HEADS, HEAD_DIM)),
        'cache_v': nrm(ks[3], (DEPTH, DEC_BATCH, PAST_LEN, N_ATT_HEADS, HEAD_DIM)),
        'cache_logf': jax.nn.log_sigmoid(3.0 + nrm(ks[4], (DEPTH, DEC_BATCH, PAST_LEN, N_ATT_HEADS))),
        'cache_conv': nrm(ks[5], (DEPTH, DEC_BATCH, CONV_WIDTH - 1, CONV_DIM), 0.5),
        'c_prompt': nrm(ks[6], (BATCH, D_MODEL)),
        'c_sample': nrm(ks[7], (DEC_BATCH, D_MODEL)),
        'w_ada': nrm(ks[8], (DEPTH, D_MODEL, 6 * D_MODEL), 0.2 * D_MODEL ** -0.5),
        'b_ada': nrm(ks[9], (DEPTH, 6 * D_MODEL), 0.01),
        'g_norm1': 1.0 + nrm(ks[10], (DEPTH, D_MODEL), 0.05),
        'g_norm2': 1.0 + nrm(ks[11], (DEPTH, D_MODEL), 0.05),
        'w_in': nrm(ks[12], (DEPTH, D_MODEL, PROJ_DIM), D_MODEL ** -0.5),
        'b_f': 3.0 + nrm(ks[13], (DEPTH, N_ATT_HEADS), 0.5),
        'w_conv': nrm(ks[14], (DEPTH, CONV_WIDTH, CONV_DIM), CONV_WIDTH ** -0.5),
        'g_attn_out': 1.0 + nrm(ks[15], (DEPTH, ATT_DIM), 0.05),
        'g_conv_out': 1.0 + nrm(ks[16], (DEPTH, CONV_DIM), 0.05),
        'w_out': nrm(ks[17], (DEPTH, D_MODEL, D_MODEL), D_MODEL ** -0.5),
        'w_up': nrm(ks[18], (DEPTH, D_MODEL, D_FF), D_MODEL ** -0.5),
        'w_down': nrm(ks[19], (DEPTH, D_FF, D_MODEL), D_FF ** -0.5),
        'w_ada_final': nrm(ks[20], (D_MODEL, 2 * D_MODEL), 0.2 * D_MODEL ** -0.5),
        'b_ada_final': nrm(ks[21], (2 * D_MODEL,), 0.01),
        'g_final': 1.0 + nrm(ks[22], (D_MODEL,), 0.05),
    }


def reference(x_prompt, x_sample, cache_k, cache_v, cache_logf, cache_conv, c_prompt, c_sample,
              w_ada, b_ada, g_norm1, g_norm2, w_in, b_f, w_conv, g_attn_out, g_conv_out, w_out,
              w_up, w_down, w_ada_final, b_ada_final, g_final):
    xp, xs = x_prompt, x_sample
    kp, vp, lp, cp = [], [], [], []
    ksl, vsl, lsl, csl = [], [], [], []
    for l in range(DEPTH):
        lw = (w_ada[l], b_ada[l], g_norm1[l], g_norm2[l], w_in[l], b_f[l], w_conv[l],
              g_attn_out[l], g_conv_out[l], w_out[l], w_up[l], w_down[l])
        prev0 = jnp.zeros((xp.shape[0], CONV_WIDTH - 1, CONV_DIM), xp.dtype)
        xp, k1, v1, f1, s1 = run_layer(xp, c_prompt, *lw, fox_prompt, prev0)
        ck, cv, cf = cache_k[l], cache_v[l], cache_logf[l]
        attend_s = lambda q, k, v, f, ck=ck, cv=cv, cf=cf: fox_sample(q, k, v, f, ck, cv, cf)
        xs, k2, v2, f2, s2 = run_layer(xs, c_sample, *lw, attend_s, cache_conv[l])
        kp.append(k1); vp.append(v1); lp.append(f1); cp.append(s1)
        ksl.append(k2); vsl.append(v2); lsl.append(f2); csl.append(s2)

    def final(x, c):
        sh, sc = jnp.split(jax.nn.silu(c) @ w_ada_final + b_ada_final, 2, axis=-1)
        return modulate(rmsnorm(x, g_final), sh, sc)

    y_prompt = final(xp, c_prompt)
    y_sample = final(xs, c_sample)
    return (y_prompt, y_sample,
            jnp.stack(kp), jnp.stack(vp), jnp.stack(lp), jnp.stack(cp),
            jnp.stack(ksl), jnp.stack(vsl), jnp.stack(lsl), jnp.stack(csl))
```

```python
import functools

import jax
import jax.numpy as jnp
from jax import lax
from jax.experimental import pallas as pl
from jax.experimental.pallas import tpu as pltpu

F32 = jnp.float32
BF16 = jnp.bfloat16

D_MODEL = 1024
N_HEADS = 8
HEAD_DIM = 64
ATT_DIM = N_HEADS * HEAD_DIM
CONV_DIM = 512
CONV_WIDTH = 3
D_FF = 4 * D_MODEL
NORM_EPS = 1e-6
ATT_SCALE = HEAD_DIM ** -0.5

LANES = 128
CARRY_ROWS = 8
N_SPLIT = 3
ONES_ROW = N_SPLIT * N_HEADS
VMEM_LIMIT_BYTES = 56 * 1024 * 1024

SEC_Q, SEC_K, SEC_V, SEC_BG, SEC_CG, SEC_U = range(6)
SEC_W = 512
F_COL = 6 * SEC_W
PROJ_COLS = F_COL + LANES


def _const_spec(shape):
    zeros = (0,) * len(shape)
    return pl.BlockSpec(shape, lambda *_: zeros, pipeline_mode=pl.Buffered(1))


def _rms(x):
    return x * lax.rsqrt(jnp.mean(x * x, axis=-1, keepdims=True) + NORM_EPS)


def _split3(x):
    hi = x.astype(BF16).astype(F32)
    r = x - hi
    mid = r.astype(BF16).astype(F32)
    lo = (r - mid).astype(BF16).astype(F32)
    return hi, mid, lo


def _cumsum_rows(l128, tri_ref):
    tri = tri_ref[...]
    parts = _split3(l128)
    sums = [jnp.dot(tri, p.astype(BF16), preferred_element_type=F32) for p in parts]
    return (sums[0] + sums[1]) + sums[2]


def _fcat(f128, lane):
    hi, mid, lo = _split3(f128)
    cat = hi + pltpu.roll(mid, N_HEADS, axis=1) + pltpu.roll(lo, 2 * N_HEADS, axis=1)
    cat = cat + jnp.where(lane == ONES_ROW, 1.0, 0.0)
    return cat.astype(BF16)


def _store_aug(dst_ref, main, extras, lane):
    low = lane < HEAD_DIM
    for pair in range(N_HEADS // 2):
        m = main[:, pair * LANES:(pair + 1) * LANES]
        e_even = extras[:, (2 * pair) * LANES:(2 * pair + 1) * LANES]
        e_odd = extras[:, (2 * pair + 1) * LANES:(2 * pair + 2) * LANES]
        dst_ref[0, 2 * pair] = jnp.where(low, m, e_even).astype(BF16)
        dst_ref[0, 2 * pair + 1] = jnp.where(low, e_odd, m).astype(BF16)


def _v_extras(lane):
    even = jnp.where(lane == HEAD_DIM, 1.0, 0.0)
    odd = jnp.where(lane == 0, 1.0, 0.0)
    return even, odd


def _store_v_aug(dst_ref, v, lane):
    low = lane < HEAD_DIM
    e_even, e_odd = _v_extras(lane)
    for pair in range(N_HEADS // 2):
        m = v[:, pair * LANES:(pair + 1) * LANES]
        dst_ref[0, 2 * pair] = jnp.where(low, m, e_even).astype(BF16)
        dst_ref[0, 2 * pair + 1] = jnp.where(low, e_odd, m).astype(BF16)


def _ada_kernel(c_ref, w_ref, b_ref, o_ref):
    c = c_ref[...]
    sc = (c * jax.nn.sigmoid(c)).astype(BF16)
    o_ref[...] = jnp.dot(sc, w_ref[...].astype(BF16), preferred_element_type=F32) + b_ref[...]


def _ada(c, w, b):
    rows, n = c.shape[0], w.shape[1]
    bn = 1024
    return pl.pallas_call(
        _ada_kernel,
        grid=(n // bn,),
        in_specs=[pl.BlockSpec((rows, D_MODEL), lambda j: (0, 0)),
                  pl.BlockSpec((D_MODEL, bn), lambda j: (0, j)),
                  pl.BlockSpec((1, bn), lambda j: (0, j))],
        out_specs=pl.BlockSpec((rows, bn), lambda j: (0, j)),
        out_shape=jax.ShapeDtypeStruct((rows, n), F32),
        name="ada",
    )(c, w, b.reshape(1, n))


def _proj_kernel(x_ref, sh_ref, sc_ref, g1_ref, w_ref, bf_ref, wconv_ref, gconv_ref, prev_ref,
                 f0_ref, tri_ref, pq_ref, pk_ref,
                 k_ref, v_ref, logf_ref, qa_ref, ka_ref, va_ref, cn_ref, cs_ref,
                 ue_ref, fc_ref, *, tile):
    s = pl.program_id(1)

    @pl.when(s == 0)
    def _():
        ue_ref[0:CARRY_ROWS, :] = jnp.zeros((CARRY_ROWS, CONV_DIM), F32)
        ue_ref[CARRY_ROWS - (CONV_WIDTH - 1):CARRY_ROWS, :] = prev_ref[0]
        fc_ref[...] = f0_ref[0]

    lane = lax.broadcasted_iota(jnp.int32, (tile, LANES), 1)
    h = _rms(x_ref[0]) * g1_ref[...]
    hb = (h * (1.0 + sc_ref[0]) + sh_ref[0]).astype(BF16)

    def proj(sec, width=SEC_W):
        return jnp.dot(hb, w_ref[:, sec * SEC_W:sec * SEC_W + width], preferred_element_type=F32)

    fl = proj(6, LANES) + bf_ref[...]
    logf = jnp.minimum(fl, 0.0) - jnp.log1p(jnp.exp(-jnp.abs(fl)))
    logf = jnp.where(lane < N_HEADS, logf, 0.0)
    logf_ref[0] = logf[:, :N_HEADS]
    f128 = _cumsum_rows(logf, tri_ref) + fc_ref[...]
    fc_ref[...] = f128[tile - 1:tile, :]
    fcat = _fcat(f128, lane)

    q = proj(SEC_Q) * ATT_SCALE
    _store_aug(qa_ref, q, jnp.dot(fcat, pq_ref[...], preferred_element_type=F32), lane)
    k = proj(SEC_K)
    k_ref[0] = k
    _store_aug(ka_ref, k, jnp.dot(fcat, pk_ref[...], preferred_element_type=F32), lane)
    v = proj(SEC_V)
    v_ref[0] = v
    _store_v_aug(va_ref, v, lane)

    up = proj(SEC_CG) * proj(SEC_U)
    ue_ref[CARRY_ROWS:CARRY_ROWS + tile, :] = up
    cv = (wconv_ref[0:1, :] * ue_ref[CARRY_ROWS - 2:CARRY_ROWS - 2 + tile, :]
          + wconv_ref[1:2, :] * ue_ref[CARRY_ROWS - 1:CARRY_ROWS - 1 + tile, :]
          + wconv_ref[2:3, :] * up)
    z = proj(SEC_BG) * cv
    cn_ref[0] = (_rms(z) * gconv_ref[...]).astype(BF16)
    ue_ref[0:CARRY_ROWS, :] = ue_ref[tile:tile + CARRY_ROWS, :]

    @pl.when(s == pl.num_programs(1) - 1)
    def _():
        cs_ref[0] = ue_ref[CARRY_ROWS + tile - (CONV_WIDTH - 1):CARRY_ROWS + tile, :]


def _proj(x, sh, sc, g1, w_pack, bf_pad, w_conv, g_conv, prev, f0, tri, pq, pk, tile):
    b, s, _ = x.shape
    assert s % tile == 0 and tile % CARRY_ROWS == 0
    ns = s // tile
    tok = lambda n: pl.BlockSpec((1, tile, n), lambda i, j: (i, j, 0))
    per_b = lambda r, n: pl.BlockSpec((1, r, n), lambda i, j: (i, 0, 0))
    aug = pl.BlockSpec((1, N_HEADS, tile, LANES), lambda i, j: (i, 0, j, 0))
    out_shape = (
        jax.ShapeDtypeStruct((b, s, ATT_DIM), F32),
        jax.ShapeDtypeStruct((b, s, ATT_DIM), F32),
        jax.ShapeDtypeStruct((b, s, N_HEADS), F32),
        jax.ShapeDtypeStruct((b, N_HEADS, s, LANES), BF16),
        jax.ShapeDtypeStruct((b, N_HEADS, s, LANES), BF16),
        jax.ShapeDtypeStruct((b, N_HEADS, s, LANES), BF16),
        jax.ShapeDtypeStruct((b, s, CONV_DIM), BF16),
        jax.ShapeDtypeStruct((b, CONV_WIDTH - 1, CONV_DIM), F32),
    )
    return pl.pallas_call(
        functools.partial(_proj_kernel, tile=tile),
        grid=(b, ns),
        in_specs=[tok(D_MODEL), per_b(1, D_MODEL), per_b(1, D_MODEL), _const_spec((1, D_MODEL)),
                  _const_spec((D_MODEL, PROJ_COLS)), _const_spec((1, LANES)),
                  _const_spec((CONV_WIDTH, CONV_DIM)), _const_spec((1, CONV_DIM)),
                  per_b(CONV_WIDTH - 1, CONV_DIM), per_b(1, LANES),
                  _const_spec((tile, tile)), _const_spec((LANES, N_HEADS * LANES)),
                  _const_spec((LANES, N_HEADS * LANES))],
        out_specs=(tok(ATT_DIM), tok(ATT_DIM), tok(N_HEADS), aug, aug, aug, tok(CONV_DIM),
                   per_b(CONV_WIDTH - 1, CONV_DIM)),
        out_shape=out_shape,
        scratch_shapes=[pltpu.VMEM((tile + CARRY_ROWS, CONV_DIM), F32), pltpu.VMEM((1, LANES), F32)],
        compiler_params=pltpu.CompilerParams(dimension_semantics=("arbitrary", "arbitrary"),
                                             vmem_limit_bytes=VMEM_LIMIT_BYTES),
        name="proj",
    )(x, sh, sc, g1, w_pack, bf_pad, w_conv, g_conv, prev, f0, tri, pq, pk)


def _cache_kernel(k_ref, v_ref, l_ref, tri_ref, pk_ref, ka_ref, va_ref, ft_ref, fc_ref, *, tile):
    s = pl.program_id(1)

    @pl.when(s == 0)
    def _():
        fc_ref[...] = jnp.zeros((1, LANES), F32)

    lane = lax.broadcasted_iota(jnp.int32, (tile, LANES), 1)
    f128 = _cumsum_rows(l_ref[0], tri_ref) + fc_ref[...]
    fc_ref[...] = f128[tile - 1:tile, :]
    fcat = _fcat(f128, lane)
    _store_aug(ka_ref, k_ref[0], jnp.dot(fcat, pk_ref[...], preferred_element_type=F32), lane)
    _store_v_aug(va_ref, v_ref[0], lane)

    @pl.when(s == pl.num_programs(1) - 1)
    def _():
        ft_ref[0] = fc_ref[...]


def _cache_aug(ck, cv, cl_pad, tri, pk, tile):
    b, p, _ = ck.shape
    tok = lambda n: pl.BlockSpec((1, tile, n), lambda i, j: (i, j, 0))
    aug = pl.BlockSpec((1, N_HEADS, tile, LANES), lambda i, j: (i, 0, j, 0))
    return pl.pallas_call(
        functools.partial(_cache_kernel, tile=tile),
        grid=(b, p // tile),
        in_specs=[tok(ATT_DIM), tok(ATT_DIM), tok(LANES), _const_spec((tile, tile)),
                  _const_spec((LANES, N_HEADS * LANES))],
        out_specs=(aug, aug, pl.BlockSpec((1, 1, LANES), lambda i, j: (i, 0, 0))),
        out_shape=(jax.ShapeDtypeStruct((b, N_HEADS, p, LANES), BF16),
                   jax.ShapeDtypeStruct((b, N_HEADS, p, LANES), BF16),
                   jax.ShapeDtypeStruct((b, 1, LANES), F32)),
        scratch_shapes=[pltpu.VMEM((1, LANES), F32)],
        compiler_params=pltpu.CompilerParams(dimension_semantics=("arbitrary", "arbitrary"),
                                             vmem_limit_bytes=VMEM_LIMIT_BYTES),
        name="cache_aug",
    )(ck, cv, cl_pad, tri, pk)


def _softmax_block(q, k, v, m, acc, mask):
    s = lax.dot_general(q, k, (((1,), (1,)), ((), ())), preferred_element_type=F32)
    if mask is not None:
        s = jnp.where(mask, s, -jnp.inf)
    m_new = jnp.maximum(m, jnp.max(s, axis=1, keepdims=True))
    p = jnp.exp(s - m_new)
    acc = acc * jnp.exp(m - m_new) + jnp.dot(p.astype(BF16), v, preferred_element_type=F32)
    return m_new, acc


def _normalise_pair(acc_even, acc_odd, lane):
    o_even = acc_even * (1.0 / acc_even[:, HEAD_DIM:HEAD_DIM + 1])
    o_odd = acc_odd * (1.0 / acc_odd[:, 0:1])
    return jnp.where(lane < HEAD_DIM, o_even, o_odd)


def _attn_kernel(q_ref, k_ref, v_ref, o_ref, *, tq):
    qi = pl.program_id(2)
    lane = lax.broadcasted_iota(jnp.int32, (tq, LANES), 1)
    causal = (lax.broadcasted_iota(jnp.int32, (tq, tq), 1)
              <= lax.broadcasted_iota(jnp.int32, (tq, tq), 0))
    accs = []
    for hh in range(2):
        q = q_ref[0, hh]

        def body(j, carry, hh=hh, q=q):
            off = pl.multiple_of(j * tq, tq)
            return _softmax_block(q, k_ref[0, hh, pl.ds(off, tq), :],
                                  v_ref[0, hh, pl.ds(off, tq), :], carry[0], carry[1], None)

        init = (jnp.full((tq, 1), -jnp.inf, F32), jnp.zeros((tq, LANES), F32))
        m, acc = lax.fori_loop(0, qi, body, init)
        off = pl.multiple_of(qi * tq, tq)
        _, acc = _softmax_block(q, k_ref[0, hh, pl.ds(off, tq), :],
                                v_ref[0, hh, pl.ds(off, tq), :], m, acc, causal)
        accs.append(acc)
    o_ref[0] = _normalise_pair(accs[0], accs[1], lane)


def _attn(qa, ka, va, tq):
    b, _, s, _ = qa.shape
    return pl.pallas_call(
        functools.partial(_attn_kernel, tq=tq),
        grid=(b, N_HEADS // 2, s // tq),
        in_specs=[pl.BlockSpec((1, 2, tq, LANES), lambda i, p, j: (i, p, j, 0)),
                  pl.BlockSpec((1, 2, s, LANES), lambda i, p, j: (i, p, 0, 0)),
                  pl.BlockSpec((1, 2, s, LANES), lambda i, p, j: (i, p, 0, 0))],
        out_specs=pl.BlockSpec((1, tq, LANES), lambda i, p, j: (i, j, p)),
        out_shape=jax.ShapeDtypeStruct((b, s, ATT_DIM), F32),
        compiler_params=pltpu.CompilerParams(
            dimension_semantics=("arbitrary", "arbitrary", "arbitrary"),
            vmem_limit_bytes=VMEM_LIMIT_BYTES),
        name="attn",
    )(qa, ka, va)


def _attn_sample_kernel(q_ref, kc_ref, vc_ref, kn_ref, vn_ref, o_ref, *, t):
    lane = lax.broadcasted_iota(jnp.int32, (t, LANES), 1)
    causal = (lax.broadcasted_iota(jnp.int32, (t, LANES), 1)
              <= lax.broadcasted_iota(jnp.int32, (t, LANES), 0))
    pad = jnp.zeros((LANES - t, LANES), BF16)
    accs = []
    for hh in range(2):
        q = q_ref[0, hh]
        init = (jnp.full((t, 1), -jnp.inf, F32), jnp.zeros((t, LANES), F32))
        m, acc = _softmax_block(q, kc_ref[0, hh], vc_ref[0, hh], init[0], init[1], None)
        kn = jnp.concatenate([kn_ref[0, hh], pad], axis=0)
        vn = jnp.concatenate([vn_ref[0, hh], pad], axis=0)
        _, acc = _softmax_block(q, kn, vn, m, acc, causal)
        accs.append(acc)
    o_ref[0] = _normalise_pair(accs[0], accs[1], lane)


def _attn_sample(qa, kc, vc, kn, vn):
    b, _, t, _ = qa.shape
    p = kc.shape[2]
    new = pl.BlockSpec((1, 2, t, LANES), lambda i, h: (i, h, 0, 0))
    old = pl.BlockSpec((1, 2, p, LANES), lambda i, h: (i, h, 0, 0))
    return pl.pallas_call(
        functools.partial(_attn_sample_kernel, t=t),
        grid=(b, N_HEADS // 2),
        in_specs=[new, old, old, new, new],
        out_specs=pl.BlockSpec((1, t, LANES), lambda i, h: (i, 0, h)),
        out_shape=jax.ShapeDtypeStruct((b, t, ATT_DIM), F32),
        compiler_params=pltpu.CompilerParams(dimension_semantics=("arbitrary", "arbitrary"),
                                             vmem_limit_bytes=VMEM_LIMIT_BYTES),
        name="attn_sample",
    )(qa, kc, vc, kn, vn)


def _post_kernel(x_ref, att_ref, cn_ref, gt1_ref, sh2_ref, sc2_ref, gt2_ref, shf_ref, scf_ref,
                 gatt_ref, g2_ref, gf_ref, wo_ref, wu_ref, wd_ref, y_ref, *, ff_chunk):
    xa = (_rms(att_ref[0]) * gatt_ref[...]).astype(BF16)
    mixed = (jnp.dot(xa, wo_ref[0:ATT_DIM, :], preferred_element_type=F32)
             + jnp.dot(cn_ref[0], wo_ref[ATT_DIM:, :], preferred_element_type=F32))
    x1 = x_ref[0] + gt1_ref[0] * mixed
    h2 = ((_rms(x1) * g2_ref[...]) * (1.0 + sc2_ref[0]) + sh2_ref[0]).astype(BF16)
    mlp = None
    for c in range(D_FF // ff_chunk):
        up = jnp.dot(h2, wu_ref[:, c * ff_chunk:(c + 1) * ff_chunk], preferred_element_type=F32)
        act = jnp.square(jnp.maximum(up, 0.0)).astype(BF16)
        part = jnp.dot(act, wd_ref[c * ff_chunk:(c + 1) * ff_chunk, :], preferred_element_type=F32)
        mlp = part if mlp is None else mlp + part
    x2 = x1 + gt2_ref[0] * mlp
    y_ref[0] = (_rms(x2) * gf_ref[...]) * (1.0 + scf_ref[0]) + shf_ref[0]


def _post(x, att, cn, mods, g_att, g2, g_final, w_out, w_up, w_down, tile):
    b, s, _ = x.shape
    tok = lambda n: pl.BlockSpec((1, tile, n), lambda i, j: (i, j, 0))
    per_b = pl.BlockSpec((1, 1, D_MODEL), lambda i, j: (i, 0, 0))
    return pl.pallas_call(
        functools.partial(_post_kernel, ff_chunk=1024),
        grid=(b, s // tile),
        in_specs=[tok(D_MODEL), tok(ATT_DIM), tok(CONV_DIM)] + [per_b] * 6
                 + [_const_spec((1, ATT_DIM)), _const_spec((1, D_MODEL)), _const_spec((1, D_MODEL)),
                    _const_spec((D_MODEL, D_MODEL)), _const_spec((D_MODEL, D_FF)),
                    _const_spec((D_FF, D_MODEL))],
        out_specs=tok(D_MODEL),
        out_shape=jax.ShapeDtypeStruct((b, s, D_MODEL), F32),
        compiler_params=pltpu.CompilerParams(dimension_semantics=("arbitrary", "arbitrary"),
                                             vmem_limit_bytes=VMEM_LIMIT_BYTES),
        name="post",
    )(x, att, cn, *mods, g_att, g2, g_final, w_out, w_up, w_down)


def _placement():
    pq = jnp.zeros((LANES, N_HEADS * LANES), F32)
    pk = jnp.zeros((LANES, N_HEADS * LANES), F32)
    for h in range(N_HEADS):
        base = h * LANES + (HEAD_DIM if h % 2 == 0 else 0)
        for part in range(N_SPLIT):
            pq = pq.at[part * N_HEADS + h, base + part].set(1.0)
            pq = pq.at[ONES_ROW, base + N_SPLIT + part].set(1.0)
            pk = pk.at[ONES_ROW, base + part].set(1.0)
            pk = pk.at[part * N_HEADS + h, base + N_SPLIT + part].set(-1.0)
    return pq.astype(BF16), pk.astype(BF16)


def _tri(n):
    return (jnp.arange(n)[None, :] <= jnp.arange(n)[:, None]).astype(BF16)


def _pack_w_in(w):
    a, c = ATT_DIM, CONV_DIM
    f0 = 3 * a
    b0 = f0 + N_HEADS
    pad = jnp.zeros((D_MODEL, LANES - N_HEADS), w.dtype)
    return jnp.concatenate([w[:, :f0], w[:, b0:b0 + 3 * c], w[:, f0:b0], pad], axis=1).astype(BF16)


def kernel(x_prompt, x_sample, cache_k, cache_v, cache_logf, cache_conv, c_prompt, c_sample, w_ada, b_ada, g_norm1, g_norm2, w_in, b_f, w_conv, g_attn_out, g_conv_out, w_out, w_up, w_down, w_ada_final, b_ada_final, g_final):
    assert w_ada.shape[0] == 1, "single layer"
    nb, s, _ = x_prompt.shape
    db, t, _ = x_sample.shape
    p = cache_k.shape[2]
    tile_prompt, tile_cache, tq = 512, 512, 512

    c_all = jnp.concatenate([c_prompt, c_sample], axis=0)
    mod = _ada(c_all, w_ada[0], b_ada[0])
    mod_f = _ada(c_all, w_ada_final, b_ada_final)
    sh1, sc1, gt1, sh2, sc2, gt2 = [m[:, None, :] for m in jnp.split(mod, 6, axis=-1)]
    shf, scf = [m[:, None, :] for m in jnp.split(mod_f, 2, axis=-1)]

    w_pack = _pack_w_in(w_in[0])
    bf_pad = jnp.pad(b_f[0], (0, LANES - N_HEADS)).reshape(1, LANES)
    pq, pk = _placement()
    g1 = g_norm1[0].reshape(1, D_MODEL)
    g2 = g_norm2[0].reshape(1, D_MODEL)
    gf = g_final.reshape(1, D_MODEL)
    g_att = g_attn_out[0].reshape(1, ATT_DIM)
    g_conv = g_conv_out[0].reshape(1, CONV_DIM)
    wo, wu, wd = w_out[0].astype(BF16), w_up[0].astype(BF16), w_down[0].astype(BF16)

    def layer(x, rows, prev, f0, tile, attend):
        sel = lambda m: m[rows]
        k, v, logf, qa, ka, va, cn, cs = _proj(x, sel(sh1), sel(sc1), g1, w_pack, bf_pad, w_conv[0],
                                               g_conv, prev, f0, _tri(tile), pq, pk, tile)
        att = attend(qa, ka, va)
        mods = [sel(m) for m in (gt1, sh2, sc2, gt2, shf, scf)]
        y = _post(x, att, cn, mods, g_att, g2, gf, wo, wu, wd, tile)
        bsz, sl = x.shape[0], x.shape[1]
        heads = lambda a: a.reshape(1, bsz, sl, N_HEADS, HEAD_DIM)
        return y, heads(k), heads(v), logf[None], cs[None]

    zeros_prev = jnp.zeros((nb, CONV_WIDTH - 1, CONV_DIM), F32)
    zeros_f = jnp.zeros((nb, 1, LANES), F32)
    yp, kp, vp, lp, cp = layer(x_prompt, slice(0, nb), zeros_prev, zeros_f, tile_prompt,
                               lambda qa, ka, va: _attn(qa, ka, va, tq))

    cl_pad = jnp.pad(cache_logf[0], ((0, 0), (0, 0), (0, LANES - N_HEADS)))
    kc, vc, f_tot = _cache_aug(cache_k[0].reshape(db, p, ATT_DIM), cache_v[0].reshape(db, p, ATT_DIM),
                               cl_pad, _tri(tile_cache), pk, tile_cache)
    ys, ks, vs, ls, cs = layer(x_sample, slice(nb, nb + db), cache_conv[0], f_tot, t,
                               lambda qa, ka, va: _attn_sample(qa, kc, vc, ka, va))
    return (yp, ys, kp, vp, lp, cp, ks, vs, ls, cs)
```

```python
import functools

import jax
import jax.numpy as jnp
from jax import lax
from jax.experimental import pallas as pl
from jax.experimental.pallas import tpu as pltpu

F32 = jnp.float32
BF16 = jnp.bfloat16

D_MODEL = 1024
N_HEADS = 8
HEAD_DIM = 64
ATT_DIM = N_HEADS * HEAD_DIM
CONV_DIM = 512
CONV_WIDTH = 3
D_FF = 4 * D_MODEL
NORM_EPS = 1e-6
ATT_SCALE = HEAD_DIM ** -0.5

LANES = 128
CARRY_ROWS = 8
N_SPLIT = 3
ONES_ROW = N_SPLIT * N_HEADS
VMEM_LIMIT_BYTES = 56 * 1024 * 1024
EXP_HEADROOM = 40.0

SEC_Q, SEC_K, SEC_V, SEC_BG, SEC_CG, SEC_U = range(6)
SEC_W = 512
F_COL = 6 * SEC_W
PROJ_COLS = F_COL + LANES


def _const_spec(shape):
    zeros = (0,) * len(shape)
    return pl.BlockSpec(shape, lambda *_: zeros, pipeline_mode=pl.Buffered(1))


def _rms(x):
    return x * lax.rsqrt(jnp.mean(x * x, axis=-1, keepdims=True) + NORM_EPS)


def _split3(x):
    hi = x.astype(BF16).astype(F32)
    r = x - hi
    mid = r.astype(BF16).astype(F32)
    lo = (r - mid).astype(BF16).astype(F32)
    return hi, mid, lo


def _cumsum_rows(l128, tri_ref):
    tri = tri_ref[...]
    parts = _split3(l128)
    sums = [jnp.dot(tri, p.astype(BF16), preferred_element_type=F32) for p in parts]
    return (sums[0] + sums[1]) + sums[2]


def _fcat(f128, lane):
    hi, mid, lo = _split3(f128)
    cat = hi + pltpu.roll(mid, N_HEADS, axis=1) + pltpu.roll(lo, 2 * N_HEADS, axis=1)
    cat = cat + jnp.where(lane == ONES_ROW, 1.0, 0.0)
    return cat.astype(BF16)


def _store_aug(dst_ref, main, extras, lane):
    low = lane < HEAD_DIM
    for pair in range(N_HEADS // 2):
        m = main[:, pair * LANES:(pair + 1) * LANES]
        e_even = extras[:, (2 * pair) * LANES:(2 * pair + 1) * LANES]
        e_odd = extras[:, (2 * pair + 1) * LANES:(2 * pair + 2) * LANES]
        dst_ref[0, 2 * pair] = jnp.where(low, m, e_even).astype(BF16)
        dst_ref[0, 2 * pair + 1] = jnp.where(low, e_odd, m).astype(BF16)


def _v_extras(lane):
    even = jnp.where(lane == HEAD_DIM, 1.0, 0.0)
    odd = jnp.where(lane == 0, 1.0, 0.0)
    return even, odd


def _store_v_aug(dst_ref, v, lane):
    low = lane < HEAD_DIM
    e_even, e_odd = _v_extras(lane)
    for pair in range(N_HEADS // 2):
        m = v[:, pair * LANES:(pair + 1) * LANES]
        dst_ref[0, 2 * pair] = jnp.where(low, m, e_even).astype(BF16)
        dst_ref[0, 2 * pair + 1] = jnp.where(low, e_odd, m).astype(BF16)


def _ada_kernel(c_ref, w_ref, b_ref, o_ref):
    c = c_ref[...]
    sc = (c * jax.nn.sigmoid(c)).astype(BF16)
    o_ref[...] = jnp.dot(sc, w_ref[...].astype(BF16), preferred_element_type=F32) + b_ref[...]


def _ada(c, w, b):
    rows, n = c.shape[0], w.shape[1]
    bn = 1024
    return pl.pallas_call(
        _ada_kernel,
        grid=(n // bn,),
        in_specs=[pl.BlockSpec((rows, D_MODEL), lambda j: (0, 0)),
                  pl.BlockSpec((D_MODEL, bn), lambda j: (0, j)),
                  pl.BlockSpec((1, bn), lambda j: (0, j))],
        out_specs=pl.BlockSpec((rows, bn), lambda j: (0, j)),
        out_shape=jax.ShapeDtypeStruct((rows, n), F32),
        name="ada",
    )(c, w, b.reshape(1, n))


def _proj_kernel(x_ref, sh_ref, sc_ref, g1_ref, w_ref, bf_ref, wconv_ref, gconv_ref, prev_ref,
                 f0_ref, tri_ref, pq_ref, pk_ref,
                 k_ref, v_ref, logf_ref, qa_ref, ka_ref, va_ref, cn_ref, cs_ref,
                 ue_ref, fc_ref, *, tile):
    s = pl.program_id(1)

    @pl.when(s == 0)
    def _():
        ue_ref[0:CARRY_ROWS, :] = jnp.zeros((CARRY_ROWS, CONV_DIM), F32)
        ue_ref[CARRY_ROWS - (CONV_WIDTH - 1):CARRY_ROWS, :] = prev_ref[0]
        fc_ref[...] = f0_ref[0]

    lane = lax.broadcasted_iota(jnp.int32, (tile, LANES), 1)
    h = _rms(x_ref[0]) * g1_ref[...]
    hb = (h * (1.0 + sc_ref[0]) + sh_ref[0]).astype(BF16)

    def proj(sec, width=SEC_W):
        return jnp.dot(hb, w_ref[:, sec * SEC_W:sec * SEC_W + width], preferred_element_type=F32)

    fl = proj(6, LANES) + bf_ref[...]
    logf = jnp.minimum(fl, 0.0) - jnp.log1p(jnp.exp(-jnp.abs(fl)))
    logf = jnp.where(lane < N_HEADS, logf, 0.0)
    logf_ref[0] = logf[:, :N_HEADS]
    f128 = _cumsum_rows(logf, tri_ref) + fc_ref[...]
    fc_ref[...] = f128[tile - 1:tile, :]
    fcat = _fcat(f128, lane)

    q = proj(SEC_Q) * ATT_SCALE
    _store_aug(qa_ref, q, jnp.dot(fcat, pq_ref[...], preferred_element_type=F32), lane)
    k = proj(SEC_K)
    k_ref[0] = k
    _store_aug(ka_ref, k, jnp.dot(fcat, pk_ref[...], preferred_element_type=F32), lane)
    v = proj(SEC_V)
    v_ref[0] = v
    _store_v_aug(va_ref, v, lane)

    up = proj(SEC_CG) * proj(SEC_U)
    ue_ref[CARRY_ROWS:CARRY_ROWS + tile, :] = up
    cv = (wconv_ref[0:1, :] * ue_ref[CARRY_ROWS - 2:CARRY_ROWS - 2 + tile, :]
          + wconv_ref[1:2, :] * ue_ref[CARRY_ROWS - 1:CARRY_ROWS - 1 + tile, :]
          + wconv_ref[2:3, :] * up)
    z = proj(SEC_BG) * cv
    cn_ref[0] = (_rms(z) * gconv_ref[...]).astype(BF16)
    ue_ref[0:CARRY_ROWS, :] = ue_ref[tile:tile + CARRY_ROWS, :]

    @pl.when(s == pl.num_programs(1) - 1)
    def _():
        cs_ref[0] = ue_ref[CARRY_ROWS + tile - (CONV_WIDTH - 1):CARRY_ROWS + tile, :]


def _proj(x, sh, sc, g1, w_pack, bf_pad, w_conv, g_conv, prev, f0, tri, pq, pk, tile):
    b, s, _ = x.shape
    assert s % tile == 0 and tile % CARRY_ROWS == 0
    ns = s // tile
    tok = lambda n: pl.BlockSpec((1, tile, n), lambda i, j: (i, j, 0))
    per_b = lambda r, n: pl.BlockSpec((1, r, n), lambda i, j: (i, 0, 0))
    aug = pl.BlockSpec((1, N_HEADS, tile, LANES), lambda i, j: (i, 0, j, 0))
    out_shape = (
        jax.ShapeDtypeStruct((b, s, ATT_DIM), F32),
        jax.ShapeDtypeStruct((b, s, ATT_DIM), F32),
        jax.ShapeDtypeStruct((b, s, N_HEADS), F32),
        jax.ShapeDtypeStruct((b, N_HEADS, s, LANES), BF16),
        jax.ShapeDtypeStruct((b, N_HEADS, s, LANES), BF16),
        jax.ShapeDtypeStruct((b, N_HEADS, s, LANES), BF16),
        jax.ShapeDtypeStruct((b, s, CONV_DIM), BF16),
        jax.ShapeDtypeStruct((b, CONV_WIDTH - 1, CONV_DIM), F32),
    )
    return pl.pallas_call(
        functools.partial(_proj_kernel, tile=tile),
        grid=(b, ns),
        in_specs=[tok(D_MODEL), per_b(1, D_MODEL), per_b(1, D_MODEL), _const_spec((1, D_MODEL)),
                  _const_spec((D_MODEL, PROJ_COLS)), _const_spec((1, LANES)),
                  _const_spec((CONV_WIDTH, CONV_DIM)), _const_spec((1, CONV_DIM)),
                  per_b(CONV_WIDTH - 1, CONV_DIM), per_b(1, LANES),
                  _const_spec((tile, tile)), _const_spec((LANES, N_HEADS * LANES)),
                  _const_spec((LANES, N_HEADS * LANES))],
        out_specs=(tok(ATT_DIM), tok(ATT_DIM), tok(N_HEADS), aug, aug, aug, tok(CONV_DIM),
                   per_b(CONV_WIDTH - 1, CONV_DIM)),
        out_shape=out_shape,
        scratch_shapes=[pltpu.VMEM((tile + CARRY_ROWS, CONV_DIM), F32), pltpu.VMEM((1, LANES), F32)],
        compiler_params=pltpu.CompilerParams(dimension_semantics=("arbitrary", "arbitrary"),
                                             vmem_limit_bytes=VMEM_LIMIT_BYTES),
        name="proj",
    )(x, sh, sc, g1, w_pack, bf_pad, w_conv, g_conv, prev, f0, tri, pq, pk)


def _cache_kernel(k_ref, v_ref, l_ref, tri_ref, pk_ref, ka_ref, va_ref, ft_ref, fc_ref, *, tile):
    s = pl.program_id(1)

    @pl.when(s == 0)
    def _():
        fc_ref[...] = jnp.zeros((1, LANES), F32)

    lane = lax.broadcasted_iota(jnp.int32, (tile, LANES), 1)
    f128 = _cumsum_rows(l_ref[0], tri_ref) + fc_ref[...]
    fc_ref[...] = f128[tile - 1:tile, :]
    fcat = _fcat(f128, lane)
    _store_aug(ka_ref, k_ref[0], jnp.dot(fcat, pk_ref[...], preferred_element_type=F32), lane)
    _store_v_aug(va_ref, v_ref[0], lane)

    @pl.when(s == pl.num_programs(1) - 1)
    def _():
        ft_ref[0] = fc_ref[...]


def _cache_aug(ck, cv, cl_pad, tri, pk, tile):
    b, p, _ = ck.shape
    tok = lambda n: pl.BlockSpec((1, tile, n), lambda i, j: (i, j, 0))
    aug = pl.BlockSpec((1, N_HEADS, tile, LANES), lambda i, j: (i, 0, j, 0))
    return pl.pallas_call(
        functools.partial(_cache_kernel, tile=tile),
        grid=(b, p // tile),
        in_specs=[tok(ATT_DIM), tok(ATT_DIM), tok(LANES), _const_spec((tile, tile)),
                  _const_spec((LANES, N_HEADS * LANES))],
        out_specs=(aug, aug, pl.BlockSpec((1, 1, LANES), lambda i, j: (i, 0, 0))),
        out_shape=(jax.ShapeDtypeStruct((b, N_HEADS, p, LANES), BF16),
                   jax.ShapeDtypeStruct((b, N_HEADS, p, LANES), BF16),
                   jax.ShapeDtypeStruct((b, 1, LANES), F32)),
        scratch_shapes=[pltpu.VMEM((1, LANES), F32)],
        compiler_params=pltpu.CompilerParams(dimension_semantics=("arbitrary", "arbitrary"),
                                             vmem_limit_bytes=VMEM_LIMIT_BYTES),
        name="cache_aug",
    )(ck, cv, cl_pad, tri, pk)


def _softmax_block(q, k, v, m, acc, mask):
    s = lax.dot_general(q, k, (((1,), (1,)), ((), ())), preferred_element_type=F32)
    if mask is not None:
        s = jnp.where(mask, s, -jnp.inf)
    m_new = jnp.maximum(m, jnp.max(s, axis=1, keepdims=True))
    p = jnp.exp(s - m_new)
    acc = acc * jnp.exp(m - m_new) + jnp.dot(p.astype(BF16), v, preferred_element_type=F32)
    return m_new, acc


def _normalise_pair(acc_even, acc_odd, lane):
    o_even = acc_even * (1.0 / acc_even[:, HEAD_DIM:HEAD_DIM + 1])
    o_odd = acc_odd * (1.0 / acc_odd[:, 0:1])
    return jnp.where(lane < HEAD_DIM, o_even, o_odd)


def _head_online(q, kv_block, qi, causal, tq):
    def body(j, carry):
        k, v = kv_block(j)
        return _softmax_block(q, k, v, carry[0], carry[1], None)

    init = (jnp.full((tq, 1), -jnp.inf, F32), jnp.zeros((tq, LANES), F32))
    m, acc = lax.fori_loop(0, qi, body, init)
    k, v = kv_block(qi)
    return _softmax_block(q, k, v, m, acc, causal)[1]


def _scores(q, k):
    return lax.dot_general(q, k, (((1,), (1,)), ((), ())), preferred_element_type=F32)


def _pair_fixed_max(qs, kv_block, qi, causal, tq):
    ms, accs = [], []
    for hh, q in enumerate(qs):
        k, v = kv_block(hh, qi)
        s = jnp.where(causal, _scores(q, k), -jnp.inf)
        m = jnp.max(s, axis=1, keepdims=True)
        ms.append(m)
        accs.append(jnp.dot(jnp.exp(s - m).astype(BF16), v, preferred_element_type=F32))

    def body(j, carry):
        out = []
        for hh, q in enumerate(qs):
            acc, tmax = carry[hh]
            k, v = kv_block(hh, j)
            t = _scores(q, k) - ms[hh]
            for c in range(tq // LANES):
                tmax = jnp.maximum(tmax, t[:, c * LANES:(c + 1) * LANES])
            out.append((acc + jnp.dot(jnp.exp(t).astype(BF16), v, preferred_element_type=F32), tmax))
        return tuple(out)

    init = tuple((acc, jnp.full((tq, LANES), -jnp.inf, F32)) for acc in accs)
    return lax.fori_loop(0, qi, body, init)


def _attn_kernel(q_ref, k_ref, v_ref, o_ref, *, tq):
    qi = pl.program_id(2)
    lane = lax.broadcasted_iota(jnp.int32, (tq, LANES), 1)
    causal = (lax.broadcasted_iota(jnp.int32, (tq, tq), 1)
              <= lax.broadcasted_iota(jnp.int32, (tq, tq), 0))

    def kv_block(hh, j):
        off = pl.multiple_of(j * tq, tq)
        return k_ref[0, hh, pl.ds(off, tq), :], v_ref[0, hh, pl.ds(off, tq), :]

    heads = q_ref.shape[1]
    qs = [q_ref[0, hh] for hh in range(heads)]
    accs = []
    for hh, (acc, tmax) in enumerate(_pair_fixed_max(qs, kv_block, qi, causal, tq)):
        redo = functools.partial(_head_online, qs[hh], functools.partial(kv_block, hh), qi, causal, tq)
        accs.append(lax.cond(jnp.max(tmax) > EXP_HEADROOM, redo, lambda acc=acc: acc))
    for pair in range(heads // 2):
        o_ref[0, :, pair * LANES:(pair + 1) * LANES] = _normalise_pair(
            accs[2 * pair], accs[2 * pair + 1], lane)


def _attn(qa, ka, va, tq, heads):
    b, _, s, _ = qa.shape
    assert heads % 2 == 0 and N_HEADS % heads == 0
    return pl.pallas_call(
        functools.partial(_attn_kernel, tq=tq),
        grid=(b, N_HEADS // heads, s // tq),
        in_specs=[pl.BlockSpec((1, heads, tq, LANES), lambda i, p, j: (i, p, j, 0)),
                  pl.BlockSpec((1, heads, s, LANES), lambda i, p, j: (i, p, 0, 0)),
                  pl.BlockSpec((1, heads, s, LANES), lambda i, p, j: (i, p, 0, 0))],
        out_specs=pl.BlockSpec((1, tq, heads * HEAD_DIM), lambda i, p, j: (i, j, p)),
        out_shape=jax.ShapeDtypeStruct((b, s, ATT_DIM), F32),
        compiler_params=pltpu.CompilerParams(
            dimension_semantics=("arbitrary", "arbitrary", "arbitrary"),
            vmem_limit_bytes=VMEM_LIMIT_BYTES),
        name="attn",
    )(qa, ka, va)


def _attn_sample_kernel(q_ref, kc_ref, vc_ref, kn_ref, vn_ref, o_ref, *, t):
    lane = lax.broadcasted_iota(jnp.int32, (t, LANES), 1)
    causal = (lax.broadcasted_iota(jnp.int32, (t, LANES), 1)
              <= lax.broadcasted_iota(jnp.int32, (t, LANES), 0))
    pad = jnp.zeros((LANES - t, LANES), BF16)
    accs = []
    for hh in range(2):
        q = q_ref[0, hh]
        init = (jnp.full((t, 1), -jnp.inf, F32), jnp.zeros((t, LANES), F32))
        m, acc = _softmax_block(q, kc_ref[0, hh], vc_ref[0, hh], init[0], init[1], None)
        kn = jnp.concatenate([kn_ref[0, hh], pad], axis=0)
        vn = jnp.concatenate([vn_ref[0, hh], pad], axis=0)
        _, acc = _softmax_block(q, kn, vn, m, acc, causal)
        accs.append(acc)
    o_ref[0] = _normalise_pair(accs[0], accs[1], lane)


def _attn_sample(qa, kc, vc, kn, vn):
    b, _, t, _ = qa.shape
    p = kc.shape[2]
    new = pl.BlockSpec((1, 2, t, LANES), lambda i, h: (i, h, 0, 0))
    old = pl.BlockSpec((1, 2, p, LANES), lambda i, h: (i, h, 0, 0))
    return pl.pallas_call(
        functools.partial(_attn_sample_kernel, t=t),
        grid=(b, N_HEADS // 2),
        in_specs=[new, old, old, new, new],
        out_specs=pl.BlockSpec((1, t, LANES), lambda i, h: (i, 0, h)),
        out_shape=jax.ShapeDtypeStruct((b, t, ATT_DIM), F32),
        compiler_params=pltpu.CompilerParams(dimension_semantics=("arbitrary", "arbitrary"),
                                             vmem_limit_bytes=VMEM_LIMIT_BYTES),
        name="attn_sample",
    )(qa, kc, vc, kn, vn)


def _post_kernel(x_ref, att_ref, cn_ref, gt1_ref, sh2_ref, sc2_ref, gt2_ref, shf_ref, scf_ref,
                 gatt_ref, g2_ref, gf_ref, wo_ref, wu_ref, wd_ref, y_ref, *, ff_chunk):
    xa = (_rms(att_ref[0]) * gatt_ref[...]).astype(BF16)
    mixed = (jnp.dot(xa, wo_ref[0:ATT_DIM, :], preferred_element_type=F32)
             + jnp.dot(cn_ref[0], wo_ref[ATT_DIM:, :], preferred_element_type=F32))
    x1 = x_ref[0] + gt1_ref[0] * mixed
    h2 = ((_rms(x1) * g2_ref[...]) * (1.0 + sc2_ref[0]) + sh2_ref[0]).astype(BF16)
    mlp = None
    for c in range(D_FF // ff_chunk):
        up = jnp.dot(h2, wu_ref[:, c * ff_chunk:(c + 1) * ff_chunk], preferred_element_type=F32)
        act = jnp.square(jnp.maximum(up, 0.0)).astype(BF16)
        part = jnp.dot(act, wd_ref[c * ff_chunk:(c + 1) * ff_chunk, :], preferred_element_type=F32)
        mlp = part if mlp is None else mlp + part
    x2 = x1 + gt2_ref[0] * mlp
    y_ref[0] = (_rms(x2) * gf_ref[...]) * (1.0 + scf_ref[0]) + shf_ref[0]


def _post(x, att, cn, mods, g_att, g2, g_final, w_out, w_up, w_down, tile):
    b, s, _ = x.shape
    tok = lambda n: pl.BlockSpec((1, tile, n), lambda i, j: (i, j, 0))
    per_b = pl.BlockSpec((1, 1, D_MODEL), lambda i, j: (i, 0, 0))
    return pl.pallas_call(
        functools.partial(_post_kernel, ff_chunk=1024),
        grid=(b, s // tile),
        in_specs=[tok(D_MODEL), tok(ATT_DIM), tok(CONV_DIM)] + [per_b] * 6
                 + [_const_spec((1, ATT_DIM)), _const_spec((1, D_MODEL)), _const_spec((1, D_MODEL)),
                    _const_spec((D_MODEL, D_MODEL)), _const_spec((D_MODEL, D_FF)),
                    _const_spec((D_FF, D_MODEL))],
        out_specs=tok(D_MODEL),
        out_shape=jax.ShapeDtypeStruct((b, s, D_MODEL), F32),
        compiler_params=pltpu.CompilerParams(dimension_semantics=("arbitrary", "arbitrary"),
                                             vmem_limit_bytes=VMEM_LIMIT_BYTES),
        name="post",
    )(x, att, cn, *mods, g_att, g2, g_final, w_out, w_up, w_down)


def _placement():
    pq = jnp.zeros((LANES, N_HEADS * LANES), F32)
    pk = jnp.zeros((LANES, N_HEADS * LANES), F32)
    for h in range(N_HEADS):
        base = h * LANES + (HEAD_DIM if h % 2 == 0 else 0)
        for part in range(N_SPLIT):
            pq = pq.at[part * N_HEADS + h, base + part].set(1.0)
            pq = pq.at[ONES_ROW, base + N_SPLIT + part].set(1.0)
            pk = pk.at[ONES_ROW, base + part].set(1.0)
            pk = pk.at[part * N_HEADS + h, base + N_SPLIT + part].set(-1.0)
    return pq.astype(BF16), pk.astype(BF16)


def _tri(n):
    return (jnp.arange(n)[None, :] <= jnp.arange(n)[:, None]).astype(BF16)


def _pack_w_in(w):
    a, c = ATT_DIM, CONV_DIM
    f0 = 3 * a
    b0 = f0 + N_HEADS
    pad = jnp.zeros((D_MODEL, LANES - N_HEADS), w.dtype)
    return jnp.concatenate([w[:, :f0], w[:, b0:b0 + 3 * c], w[:, f0:b0], pad], axis=1).astype(BF16)


def kernel(x_prompt, x_sample, cache_k, cache_v, cache_logf, cache_conv, c_prompt, c_sample, w_ada, b_ada, g_norm1, g_norm2, w_in, b_f, w_conv, g_attn_out, g_conv_out, w_out, w_up, w_down, w_ada_final, b_ada_final, g_final):
    assert w_ada.shape[0] == 1, "single layer"
    nb, s, _ = x_prompt.shape
    db, t, _ = x_sample.shape
    p = cache_k.shape[2]
    tile_prompt, tile_cache, tq, heads_per_step = 512, 512, 512, 4

    c_all = jnp.concatenate([c_prompt, c_sample], axis=0)
    mod = _ada(c_all, w_ada[0], b_ada[0])
    mod_f = _ada(c_all, w_ada_final, b_ada_final)
    sh1, sc1, gt1, sh2, sc2, gt2 = [m[:, None, :] for m in jnp.split(mod, 6, axis=-1)]
    shf, scf = [m[:, None, :] for m in jnp.split(mod_f, 2, axis=-1)]

    w_pack = _pack_w_in(w_in[0])
    bf_pad = jnp.pad(b_f[0], (0, LANES - N_HEADS)).reshape(1, LANES)
    pq, pk = _placement()
    g1 = g_norm1[0].reshape(1, D_MODEL)
    g2 = g_norm2[0].reshape(1, D_MODEL)
    gf = g_final.reshape(1, D_MODEL)
    g_att = g_attn_out[0].reshape(1, ATT_DIM)
    g_conv = g_conv_out[0].reshape(1, CONV_DIM)
    wo, wu, wd = w_out[0].astype(BF16), w_up[0].astype(BF16), w_down[0].astype(BF16)

    def layer(x, rows, prev, f0, tile, attend):
        sel = lambda m: m[rows]
        k, v, logf, qa, ka, va, cn, cs = _proj(x, sel(sh1), sel(sc1), g1, w_pack, bf_pad, w_conv[0],
                                               g_conv, prev, f0, _tri(tile), pq, pk, tile)
        att = attend(qa, ka, va)
        mods = [sel(m) for m in (gt1, sh2, sc2, gt2, shf, scf)]
        y = _post(x, att, cn, mods, g_att, g2, gf, wo, wu, wd, tile)
        bsz, sl = x.shape[0], x.shape[1]
        heads = lambda a: a.reshape(1, bsz, sl, N_HEADS, HEAD_DIM)
        return y, heads(k), heads(v), logf[None], cs[None]

    zeros_prev = jnp.zeros((nb, CONV_WIDTH - 1, CONV_DIM), F32)
    zeros_f = jnp.zeros((nb, 1, LANES), F32)
    yp, kp, vp, lp, cp = layer(x_prompt, slice(0, nb), zeros_prev, zeros_f, tile_prompt,
                               lambda qa, ka, va: _attn(qa, ka, va, tq, heads_per_step))

    cl_pad = jnp.pad(cache_logf[0], ((0, 0), (0, 0), (0, LANES - N_HEADS)))
    kc, vc, f_tot = _cache_aug(cache_k[0].reshape(db, p, ATT_DIM), cache_v[0].reshape(db, p, ATT_DIM),
                               cl_pad, _tri(tile_cache), pk, tile_cache)
    ys, ks, vs, ls, cs = layer(x_sample, slice(nb, nb + db), cache_conv[0], f_tot, t,
                               lambda qa, ka, va: _attn_sample(qa, kc, vc, ka, va))
    return (yp, ys, kp, vp, lp, cp, ks, vs, ls, cs)
```

```python
import functools

import jax
import jax.numpy as jnp
from jax import lax
from jax.experimental import pallas as pl
from jax.experimental.pallas import tpu as pltpu

F32 = jnp.float32
BF16 = jnp.bfloat16

D_MODEL = 1024
N_HEADS = 8
HEAD_DIM = 64
ATT_DIM = N_HEADS * HEAD_DIM
CONV_DIM = 512
CONV_WIDTH = 3
D_FF = 4 * D_MODEL
NORM_EPS = 1e-6
ATT_SCALE = HEAD_DIM ** -0.5

LANES = 128
CARRY_ROWS = 8
N_SPLIT = 3
ONES_ROW = N_SPLIT * N_HEADS
VMEM_LIMIT_BYTES = 56 * 1024 * 1024
BF16_ROWS = 16
SHIFT_ROW = 8
LOG2E = 1.4426950408889634

SEC_Q, SEC_K, SEC_V, SEC_BG, SEC_CG, SEC_U = range(6)
SEC_W = 512
F_COL = 6 * SEC_W
PROJ_COLS = F_COL + LANES


def _const_spec(shape):
    zeros = (0,) * len(shape)
    return pl.BlockSpec(shape, lambda *_: zeros, pipeline_mode=pl.Buffered(1))


def _rms(x):
    return x * lax.rsqrt(jnp.mean(x * x, axis=-1, keepdims=True) + NORM_EPS)


def _split3(x):
    hi = x.astype(BF16).astype(F32)
    r = x - hi
    mid = r.astype(BF16).astype(F32)
    lo = (r - mid).astype(BF16).astype(F32)
    return hi, mid, lo


def _cumsum_rows(l128, tri_ref):
    tri = tri_ref[...]
    parts = _split3(l128)
    sums = [jnp.dot(tri, p.astype(BF16), preferred_element_type=F32) for p in parts]
    return (sums[0] + sums[1]) + sums[2]


def _fcat(f128, lane):
    hi, mid, lo = _split3(f128 * LOG2E)
    cat = hi + pltpu.roll(mid, N_HEADS, axis=1) + pltpu.roll(lo, 2 * N_HEADS, axis=1)
    cat = cat + jnp.where(lane == ONES_ROW, 1.0, 0.0)
    return cat.astype(BF16)


def _store_aug(dst_ref, main, extras, lane):
    low = lane < HEAD_DIM
    for pair in range(N_HEADS // 2):
        m = main[:, pair * LANES:(pair + 1) * LANES]
        e_even = extras[:, (2 * pair) * LANES:(2 * pair + 1) * LANES]
        e_odd = extras[:, (2 * pair + 1) * LANES:(2 * pair + 2) * LANES]
        dst_ref[0, 2 * pair] = jnp.where(low, m, e_even).astype(BF16)
        dst_ref[0, 2 * pair + 1] = jnp.where(low, e_odd, m).astype(BF16)


def _v_extras(lane):
    even = jnp.where(lane == HEAD_DIM, 1.0, 0.0)
    odd = jnp.where(lane == 0, 1.0, 0.0)
    return even, odd


def _store_v_aug(dst_ref, v, lane):
    low = lane < HEAD_DIM
    e_even, e_odd = _v_extras(lane)
    for pair in range(N_HEADS // 2):
        m = v[:, pair * LANES:(pair + 1) * LANES]
        dst_ref[0, 2 * pair] = jnp.where(low, m, e_even).astype(BF16)
        dst_ref[0, 2 * pair + 1] = jnp.where(low, e_odd, m).astype(BF16)


def _ada_kernel(c_ref, w_ref, b_ref, o_ref):
    c = c_ref[...]
    sc = (c * jax.nn.sigmoid(c)).astype(BF16)
    o_ref[...] = jnp.dot(sc, w_ref[...].astype(BF16), preferred_element_type=F32) + b_ref[...]


def _ada(c, w, b):
    rows, n = c.shape[0], w.shape[1]
    bn = 1024
    return pl.pallas_call(
        _ada_kernel,
        grid=(n // bn,),
        in_specs=[pl.BlockSpec((rows, D_MODEL), lambda j: (0, 0)),
                  pl.BlockSpec((D_MODEL, bn), lambda j: (0, j)),
                  pl.BlockSpec((1, bn), lambda j: (0, j))],
        out_specs=pl.BlockSpec((rows, bn), lambda j: (0, j)),
        out_shape=jax.ShapeDtypeStruct((rows, n), F32),
        name="ada",
    )(c, w, b.reshape(1, n))


def _store_qv_transposed(qa_ref, va_ref, hb, fcat, wt_ref, pqt_ref, tile):
    nt = (((1,), (1,)), ((), ()))
    qv_t = lax.dot_general(wt_ref[...], hb, nt, preferred_element_type=F32)
    eq_t = lax.dot_general(pqt_ref[...], fcat, nt, preferred_element_type=F32)
    ones_row = jnp.where(lax.broadcasted_iota(jnp.int32, (HEAD_DIM, tile), 0) == 0, 1.0, 0.0)
    for h in range(N_HEADS):
        rows = slice(h * HEAD_DIM, (h + 1) * HEAD_DIM)
        parts = [qv_t[rows], eq_t[rows]]
        qa_ref[0, h] = jnp.concatenate(parts if h % 2 == 0 else parts[::-1], axis=0).astype(BF16)
        v_rows = slice(ATT_DIM + h * HEAD_DIM, ATT_DIM + (h + 1) * HEAD_DIM)
        va_ref[0, h] = jnp.concatenate([qv_t[v_rows], ones_row], axis=0).astype(BF16)


def _proj_kernel(x_ref, sh_ref, sc_ref, g1_ref, w_ref, wt_ref, bf_ref, wconv_ref, gconv_ref, prev_ref,
                 f0_ref, tri_ref, pq_ref, pqt_ref, pk_ref,
                 k_ref, v_ref, logf_ref, qa_ref, ka_ref, va_ref, cn_ref, cs_ref,
                 ue_ref, fc_ref, *, tile, transposed_qv):
    s = pl.program_id(1)

    @pl.when(s == 0)
    def _():
        ue_ref[0:CARRY_ROWS, :] = jnp.zeros((CARRY_ROWS, CONV_DIM), F32)
        ue_ref[CARRY_ROWS - (CONV_WIDTH - 1):CARRY_ROWS, :] = prev_ref[0]
        fc_ref[...] = f0_ref[0]

    lane = lax.broadcasted_iota(jnp.int32, (tile, LANES), 1)
    h = _rms(x_ref[0]) * g1_ref[...]
    hb = (h * (1.0 + sc_ref[0]) + sh_ref[0]).astype(BF16)

    def proj(sec, width=SEC_W):
        return jnp.dot(hb, w_ref[:, sec * SEC_W:sec * SEC_W + width], preferred_element_type=F32)

    fl = proj(6, LANES) + bf_ref[...]
    logf = jnp.minimum(fl, 0.0) - jnp.log1p(jnp.exp(-jnp.abs(fl)))
    logf = jnp.where(lane < N_HEADS, logf, 0.0)
    logf_ref[0] = logf[:, :N_HEADS]
    f128 = _cumsum_rows(logf, tri_ref) + fc_ref[...]
    fc_ref[...] = f128[tile - 1:tile, :]
    fcat = _fcat(f128, lane)

    k = proj(SEC_K)
    k_ref[0] = k
    _store_aug(ka_ref, k, jnp.dot(fcat, pk_ref[...], preferred_element_type=F32), lane)
    v = proj(SEC_V)
    v_ref[0] = v
    if transposed_qv:
        _store_qv_transposed(qa_ref, va_ref, hb, fcat, wt_ref, pqt_ref, tile)
    else:
        _store_aug(qa_ref, proj(SEC_Q), jnp.dot(fcat, pq_ref[...], preferred_element_type=F32), lane)
        _store_v_aug(va_ref, v, lane)

    up = proj(SEC_CG) * proj(SEC_U)
    ue_ref[CARRY_ROWS:CARRY_ROWS + tile, :] = up
    cv = (wconv_ref[0:1, :] * ue_ref[CARRY_ROWS - 2:CARRY_ROWS - 2 + tile, :]
          + wconv_ref[1:2, :] * ue_ref[CARRY_ROWS - 1:CARRY_ROWS - 1 + tile, :]
          + wconv_ref[2:3, :] * up)
    z = proj(SEC_BG) * cv
    cn_ref[0] = (_rms(z) * gconv_ref[...]).astype(BF16)
    ue_ref[0:CARRY_ROWS, :] = ue_ref[tile:tile + CARRY_ROWS, :]

    @pl.when(s == pl.num_programs(1) - 1)
    def _():
        cs_ref[0] = ue_ref[CARRY_ROWS + tile - (CONV_WIDTH - 1):CARRY_ROWS + tile, :]


def _proj(x, sh, sc, g1, w_pack, w_t, bf_pad, w_conv, g_conv, prev, f0, tri, pq, pq_t, pk, tile,
          transposed_qv):
    b, s, _ = x.shape
    assert s % tile == 0 and tile % CARRY_ROWS == 0
    ns = s // tile
    tok = lambda n: pl.BlockSpec((1, tile, n), lambda i, j: (i, j, 0))
    per_b = lambda r, n: pl.BlockSpec((1, r, n), lambda i, j: (i, 0, 0))
    aug = pl.BlockSpec((1, N_HEADS, tile, LANES), lambda i, j: (i, 0, j, 0))
    aug_sds = jax.ShapeDtypeStruct((b, N_HEADS, s, LANES), BF16)
    if transposed_qv:
        qv = pl.BlockSpec((1, N_HEADS, LANES, tile), lambda i, j: (i, 0, 0, j))
        qv_sds = jax.ShapeDtypeStruct((b, N_HEADS, LANES, s), BF16)
    else:
        qv, qv_sds = aug, aug_sds
    out_shape = (
        jax.ShapeDtypeStruct((b, s, ATT_DIM), F32),
        jax.ShapeDtypeStruct((b, s, ATT_DIM), F32),
        jax.ShapeDtypeStruct((b, s, N_HEADS), F32),
        qv_sds,
        aug_sds,
        qv_sds,
        jax.ShapeDtypeStruct((b, s, CONV_DIM), BF16),
        jax.ShapeDtypeStruct((b, CONV_WIDTH - 1, CONV_DIM), F32),
    )
    return pl.pallas_call(
        functools.partial(_proj_kernel, tile=tile, transposed_qv=transposed_qv),
        grid=(b, ns),
        in_specs=[tok(D_MODEL), per_b(1, D_MODEL), per_b(1, D_MODEL), _const_spec((1, D_MODEL)),
                  _const_spec((D_MODEL, PROJ_COLS)), _const_spec((2 * ATT_DIM, D_MODEL)),
                  _const_spec((1, LANES)),
                  _const_spec((CONV_WIDTH, CONV_DIM)), _const_spec((1, CONV_DIM)),
                  per_b(CONV_WIDTH - 1, CONV_DIM), per_b(1, LANES),
                  _const_spec((tile, tile)), _const_spec((LANES, N_HEADS * LANES)),
                  _const_spec((ATT_DIM, LANES)), _const_spec((LANES, N_HEADS * LANES))],
        out_specs=(tok(ATT_DIM), tok(ATT_DIM), tok(N_HEADS), qv, aug, qv, tok(CONV_DIM),
                   per_b(CONV_WIDTH - 1, CONV_DIM)),
        out_shape=out_shape,
        scratch_shapes=[pltpu.VMEM((tile + CARRY_ROWS, CONV_DIM), F32), pltpu.VMEM((1, LANES), F32)],
        compiler_params=pltpu.CompilerParams(dimension_semantics=("arbitrary", "arbitrary"),
                                             vmem_limit_bytes=VMEM_LIMIT_BYTES),
        name="proj",
    )(x, sh, sc, g1, w_pack, w_t, bf_pad, w_conv, g_conv, prev, f0, tri, pq, pq_t, pk)


def _cache_kernel(k_ref, v_ref, l_ref, tri_ref, pk_ref, ka_ref, va_ref, ft_ref, fc_ref, *, tile):
    s = pl.program_id(1)

    @pl.when(s == 0)
    def _():
        fc_ref[...] = jnp.zeros((1, LANES), F32)

    lane = lax.broadcasted_iota(jnp.int32, (tile, LANES), 1)
    f128 = _cumsum_rows(l_ref[0], tri_ref) + fc_ref[...]
    fc_ref[...] = f128[tile - 1:tile, :]
    fcat = _fcat(f128, lane)
    _store_aug(ka_ref, k_ref[0], jnp.dot(fcat, pk_ref[...], preferred_element_type=F32), lane)
    _store_v_aug(va_ref, v_ref[0], lane)

    @pl.when(s == pl.num_programs(1) - 1)
    def _():
        ft_ref[0] = fc_ref[...]


def _cache_aug(ck, cv, cl_pad, tri, pk, tile):
    b, p, _ = ck.shape
    tok = lambda n: pl.BlockSpec((1, tile, n), lambda i, j: (i, j, 0))
    aug = pl.BlockSpec((1, N_HEADS, tile, LANES), lambda i, j: (i, 0, j, 0))
    return pl.pallas_call(
        functools.partial(_cache_kernel, tile=tile),
        grid=(b, p // tile),
        in_specs=[tok(ATT_DIM), tok(ATT_DIM), tok(LANES), _const_spec((tile, tile)),
                  _const_spec((LANES, N_HEADS * LANES))],
        out_specs=(aug, aug, pl.BlockSpec((1, 1, LANES), lambda i, j: (i, 0, 0))),
        out_shape=(jax.ShapeDtypeStruct((b, N_HEADS, p, LANES), BF16),
                   jax.ShapeDtypeStruct((b, N_HEADS, p, LANES), BF16),
                   jax.ShapeDtypeStruct((b, 1, LANES), F32)),
        scratch_shapes=[pltpu.VMEM((1, LANES), F32)],
        compiler_params=pltpu.CompilerParams(dimension_semantics=("arbitrary", "arbitrary"),
                                             vmem_limit_bytes=VMEM_LIMIT_BYTES),
        name="cache_aug",
    )(ck, cv, cl_pad, tri, pk)


def _softmax_block(q, k, v, m, acc, mask):
    s = lax.dot_general(q, k, (((1,), (1,)), ((), ())), preferred_element_type=F32)
    if mask is not None:
        s = jnp.where(mask, s, -jnp.inf)
    m_new = jnp.maximum(m, jnp.max(s, axis=1, keepdims=True))
    p = jnp.exp2(s - m_new)
    acc = acc * jnp.exp2(m - m_new) + jnp.dot(p.astype(BF16), v, preferred_element_type=F32)
    return m_new, acc


def _normalise_pair(acc_even, acc_odd, lane):
    o_even = acc_even * (1.0 / acc_even[:, HEAD_DIM:HEAD_DIM + 1])
    o_odd = acc_odd * (1.0 / acc_odd[:, 0:1])
    return jnp.where(lane < HEAD_DIM, o_even, o_odd)


def _softmax_block_t(qt, k, vt, m, acc, mask):
    s = jnp.dot(k, qt, preferred_element_type=F32)
    if mask is not None:
        s = jnp.where(mask, s, -jnp.inf)
    m_new = jnp.maximum(m, jnp.max(s, axis=0, keepdims=True))
    p = jnp.exp2(s - m_new)
    acc = acc * jnp.exp2(m - m_new) + jnp.dot(vt, p.astype(BF16), preferred_element_type=F32)
    return m_new, acc


def _head_online(qt, kv_block, qi, causal, tq):
    def body(j, carry):
        k, vt = kv_block(j)
        return _softmax_block_t(qt, k, vt, carry[0], carry[1], None)

    init = (jnp.full((1, tq), -jnp.inf, F32), jnp.zeros((LANES, tq), F32))
    m, acc = lax.fori_loop(0, qi, body, init)
    k, vt = kv_block(qi)
    return _softmax_block_t(qt, k, vt, m, acc, causal)[1]


def _with_shift_rows(qt, m, parity):
    g0 = HEAD_DIM if parity == 0 else 0
    hi, mid, lo = _split3(-m)
    row = lax.broadcasted_iota(jnp.int32, (BF16_ROWS, m.shape[1]), 0)
    add = jnp.where(row == SHIFT_ROW, hi,
                    jnp.where(row == SHIFT_ROW + 1, mid, jnp.where(row == SHIFT_ROW + 2, lo, 0.0)))
    grp = (qt[g0:g0 + BF16_ROWS].astype(F32) + add).astype(BF16)
    pieces = ([qt[:g0]] if g0 else []) + [grp, qt[g0 + BF16_ROWS:]]
    return jnp.concatenate(pieces, axis=0)


def _heads_fixed_max(qts, kv_block, qi, causal, acc_ref, qs_ref):
    n = len(qts)
    dot = functools.partial(jnp.dot, preferred_element_type=F32)
    diag = lambda hh: jnp.where(causal, dot(kv_block(hh, qi)[0], qts[hh]), -jnp.inf)
    s_next = diag(0)
    for hh in range(n):
        s, s_next = s_next, (diag(hh + 1) if hh + 1 < n else None)
        m = jnp.max(s, axis=0, keepdims=True)
        qs_ref[hh] = _with_shift_rows(qts[hh], m, hh % 2)
        acc_ref[hh] = dot(kv_block(hh, qi)[1], jnp.exp2(s - m).astype(BF16))

    def body(j, carry):
        scores = lambda hh: dot(kv_block(hh, j)[0], qs_ref[hh])
        s_next = scores(0)
        for hh in range(n):
            s, s_next = s_next, (scores(hh + 1) if hh + 1 < n else None)
            acc_ref[hh] += dot(kv_block(hh, j)[1], jnp.exp2(s).astype(BF16))
        return carry

    lax.fori_loop(0, qi, body, 0)
    return [acc_ref[hh] for hh in range(n)]


def _attn_kernel(qt_ref, k_ref, vt_ref, o_ref, acc_ref, qs_ref, *, tq):
    qi = pl.program_id(2)
    causal = (lax.broadcasted_iota(jnp.int32, (tq, tq), 0)
              <= lax.broadcasted_iota(jnp.int32, (tq, tq), 1))

    def kv_block(hh, j):
        off = pl.multiple_of(j * tq, tq)
        return k_ref[0, hh, pl.ds(off, tq), :], vt_ref[0, hh, :, pl.ds(off, tq)]

    heads = qt_ref.shape[1]
    qts = [qt_ref[0, hh] for hh in range(heads)]
    outs = []
    for hh, acc in enumerate(_heads_fixed_max(qts, kv_block, qi, causal, acc_ref, qs_ref)):
        finite = jnp.min(jnp.where(jnp.isfinite(acc), 1.0, 0.0)) > 0.5
        redo = functools.partial(_head_online, qts[hh], functools.partial(kv_block, hh), qi, causal, tq)
        acc = lax.cond(finite, lambda acc=acc: acc, redo)
        outs.append(acc[0:HEAD_DIM] * (1.0 / acc[HEAD_DIM:HEAD_DIM + 1]))
    for pair in range(heads // 2):
        both = jnp.concatenate([outs[2 * pair], outs[2 * pair + 1]], axis=0)
        o_ref[0, :, pair * LANES:(pair + 1) * LANES] = both.T


def _attn(qt, ka, vt, tq, heads):
    b, _, s, _ = ka.shape
    assert heads % 2 == 0 and N_HEADS % heads == 0
    return pl.pallas_call(
        functools.partial(_attn_kernel, tq=tq),
        grid=(b, N_HEADS // heads, s // tq),
        in_specs=[pl.BlockSpec((1, heads, LANES, tq), lambda i, p, j: (i, p, 0, j)),
                  pl.BlockSpec((1, heads, s, LANES), lambda i, p, j: (i, p, 0, 0)),
                  pl.BlockSpec((1, heads, LANES, s), lambda i, p, j: (i, p, 0, 0))],
        out_specs=pl.BlockSpec((1, tq, heads * HEAD_DIM), lambda i, p, j: (i, j, p)),
        out_shape=jax.ShapeDtypeStruct((b, s, ATT_DIM), F32),
        scratch_shapes=[pltpu.VMEM((heads, LANES, tq), F32), pltpu.VMEM((heads, LANES, tq), BF16)],
        compiler_params=pltpu.CompilerParams(
            dimension_semantics=("arbitrary", "arbitrary", "arbitrary"),
            vmem_limit_bytes=VMEM_LIMIT_BYTES),
        name="attn",
    )(qt, ka, vt)


def _attn_sample_kernel(q_ref, kc_ref, vc_ref, kn_ref, vn_ref, o_ref, *, t):
    lane = lax.broadcasted_iota(jnp.int32, (t, LANES), 1)
    causal = (lax.broadcasted_iota(jnp.int32, (t, LANES), 1)
              <= lax.broadcasted_iota(jnp.int32, (t, LANES), 0))
    pad = jnp.zeros((LANES - t, LANES), BF16)
    accs = []
    for hh in range(2):
        q = q_ref[0, hh]
        init = (jnp.full((t, 1), -jnp.inf, F32), jnp.zeros((t, LANES), F32))
        m, acc = _softmax_block(q, kc_ref[0, hh], vc_ref[0, hh], init[0], init[1], None)
        kn = jnp.concatenate([kn_ref[0, hh], pad], axis=0)
        vn = jnp.concatenate([vn_ref[0, hh], pad], axis=0)
        _, acc = _softmax_block(q, kn, vn, m, acc, causal)
        accs.append(acc)
    o_ref[0] = _normalise_pair(accs[0], accs[1], lane)


def _attn_sample(qa, kc, vc, kn, vn):
    b, _, t, _ = qa.shape
    p = kc.shape[2]
    new = pl.BlockSpec((1, 2, t, LANES), lambda i, h: (i, h, 0, 0))
    old = pl.BlockSpec((1, 2, p, LANES), lambda i, h: (i, h, 0, 0))
    return pl.pallas_call(
        functools.partial(_attn_sample_kernel, t=t),
        grid=(b, N_HEADS // 2),
        in_specs=[new, old, old, new, new],
        out_specs=pl.BlockSpec((1, t, LANES), lambda i, h: (i, 0, h)),
        out_shape=jax.ShapeDtypeStruct((b, t, ATT_DIM), F32),
        compiler_params=pltpu.CompilerParams(dimension_semantics=("arbitrary", "arbitrary"),
                                             vmem_limit_bytes=VMEM_LIMIT_BYTES),
        name="attn_sample",
    )(qa, kc, vc, kn, vn)


def _post_kernel(x_ref, att_ref, cn_ref, gt1_ref, sh2_ref, sc2_ref, gt2_ref, shf_ref, scf_ref,
                 gatt_ref, g2_ref, gf_ref, wo_ref, wu_ref, wd_ref, y_ref, *, ff_chunk):
    xa = (_rms(att_ref[0]) * gatt_ref[...]).astype(BF16)
    mixed = (jnp.dot(xa, wo_ref[0:ATT_DIM, :], preferred_element_type=F32)
             + jnp.dot(cn_ref[0], wo_ref[ATT_DIM:, :], preferred_element_type=F32))
    x1 = x_ref[0] + gt1_ref[0] * mixed
    h2 = ((_rms(x1) * g2_ref[...]) * (1.0 + sc2_ref[0]) + sh2_ref[0]).astype(BF16)
    mlp = None
    for c in range(D_FF // ff_chunk):
        up = jnp.dot(h2, wu_ref[:, c * ff_chunk:(c + 1) * ff_chunk], preferred_element_type=F32)
        act = jnp.square(jnp.maximum(up, 0.0)).astype(BF16)
        part = jnp.dot(act, wd_ref[c * ff_chunk:(c + 1) * ff_chunk, :], preferred_element_type=F32)
        mlp = part if mlp is None else mlp + part
    x2 = x1 + gt2_ref[0] * mlp
    y_ref[0] = (_rms(x2) * gf_ref[...]) * (1.0 + scf_ref[0]) + shf_ref[0]


def _post(x, att, cn, mods, g_att, g2, g_final, w_out, w_up, w_down, tile):
    b, s, _ = x.shape
    tok = lambda n: pl.BlockSpec((1, tile, n), lambda i, j: (i, j, 0))
    per_b = pl.BlockSpec((1, 1, D_MODEL), lambda i, j: (i, 0, 0))
    return pl.pallas_call(
        functools.partial(_post_kernel, ff_chunk=1024),
        grid=(b, s // tile),
        in_specs=[tok(D_MODEL), tok(ATT_DIM), tok(CONV_DIM)] + [per_b] * 6
                 + [_const_spec((1, ATT_DIM)), _const_spec((1, D_MODEL)), _const_spec((1, D_MODEL)),
                    _const_spec((D_MODEL, D_MODEL)), _const_spec((D_MODEL, D_FF)),
                    _const_spec((D_FF, D_MODEL))],
        out_specs=tok(D_MODEL),
        out_shape=jax.ShapeDtypeStruct((b, s, D_MODEL), F32),
        compiler_params=pltpu.CompilerParams(dimension_semantics=("arbitrary", "arbitrary"),
                                             vmem_limit_bytes=VMEM_LIMIT_BYTES),
        name="post",
    )(x, att, cn, *mods, g_att, g2, g_final, w_out, w_up, w_down)


def _placement():
    pq = jnp.zeros((LANES, N_HEADS * LANES), F32)
    pk = jnp.zeros((LANES, N_HEADS * LANES), F32)
    pq_t = jnp.zeros((ATT_DIM, LANES), F32)
    for h in range(N_HEADS):
        base = h * LANES + (HEAD_DIM if h % 2 == 0 else 0)
        for part in range(N_SPLIT):
            pq = pq.at[part * N_HEADS + h, base + part].set(1.0)
            pq = pq.at[ONES_ROW, base + N_SPLIT + part].set(1.0)
            pq_t = pq_t.at[h * HEAD_DIM + part, part * N_HEADS + h].set(1.0)
            pq_t = pq_t.at[h * HEAD_DIM + N_SPLIT + part, ONES_ROW].set(1.0)
            pk = pk.at[ONES_ROW, base + part].set(1.0)
            pk = pk.at[part * N_HEADS + h, base + N_SPLIT + part].set(-1.0)
            pk = pk.at[ONES_ROW, base + SHIFT_ROW + part].set(1.0)
    return pq.astype(BF16), pq_t.astype(BF16), pk.astype(BF16)


def _tri(n):
    return (jnp.arange(n)[None, :] <= jnp.arange(n)[:, None]).astype(BF16)


def _pack_w_in(w):
    a, c = ATT_DIM, CONV_DIM
    f0 = 3 * a
    b0 = f0 + N_HEADS
    pad = jnp.zeros((D_MODEL, LANES - N_HEADS), w.dtype)
    wq = w[:, :a] * (ATT_SCALE * LOG2E)
    packed = jnp.concatenate([wq, w[:, a:f0], w[:, b0:b0 + 3 * c], w[:, f0:b0], pad], axis=1)
    transposed = jnp.concatenate([wq, w[:, 2 * a:f0]], axis=1).T
    return packed.astype(BF16), transposed.astype(BF16)


def kernel(x_prompt, x_sample, cache_k, cache_v, cache_logf, cache_conv, c_prompt, c_sample, w_ada, b_ada, g_norm1, g_norm2, w_in, b_f, w_conv, g_attn_out, g_conv_out, w_out, w_up, w_down, w_ada_final, b_ada_final, g_final):
    assert w_ada.shape[0] == 1, "single layer"
    nb, s, _ = x_prompt.shape
    db, t, _ = x_sample.shape
    p = cache_k.shape[2]
    tile_prompt, tile_cache, tq, heads_per_step = 512, 512, 512, 8

    c_all = jnp.concatenate([c_prompt, c_sample], axis=0)
    mod = _ada(c_all, w_ada[0], b_ada[0])
    mod_f = _ada(c_all, w_ada_final, b_ada_final)
    sh1, sc1, gt1, sh2, sc2, gt2 = [m[:, None, :] for m in jnp.split(mod, 6, axis=-1)]
    shf, scf = [m[:, None, :] for m in jnp.split(mod_f, 2, axis=-1)]

    w_pack, w_t = _pack_w_in(w_in[0])
    bf_pad = jnp.pad(b_f[0], (0, LANES - N_HEADS)).reshape(1, LANES)
    pq, pq_t, pk = _placement()
    g1 = g_norm1[0].reshape(1, D_MODEL)
    g2 = g_norm2[0].reshape(1, D_MODEL)
    gf = g_final.reshape(1, D_MODEL)
    g_att = g_attn_out[0].reshape(1, ATT_DIM)
    g_conv = g_conv_out[0].reshape(1, CONV_DIM)
    wo, wu, wd = w_out[0].astype(BF16), w_up[0].astype(BF16), w_down[0].astype(BF16)

    def layer(x, rows, prev, f0, tile, transposed_qv, attend):
        sel = lambda m: m[rows]
        k, v, logf, qa, ka, va, cn, cs = _proj(x, sel(sh1), sel(sc1), g1, w_pack, w_t, bf_pad, w_conv[0],
                                               g_conv, prev, f0, _tri(tile), pq, pq_t, pk, tile,
                                               transposed_qv)
        att = attend(qa, ka, va)
        mods = [sel(m) for m in (gt1, sh2, sc2, gt2, shf, scf)]
        y = _post(x, att, cn, mods, g_att, g2, gf, wo, wu, wd, tile)
        bsz, sl = x.shape[0], x.shape[1]
        heads = lambda a: a.reshape(1, bsz, sl, N_HEADS, HEAD_DIM)
        return y, heads(k), heads(v), logf[None], cs[None]

    zeros_prev = jnp.zeros((nb, CONV_WIDTH - 1, CONV_DIM), F32)
    zeros_f = jnp.zeros((nb, 1, LANES), F32)
    yp, kp, vp, lp, cp = layer(x_prompt, slice(0, nb), zeros_prev, zeros_f, tile_prompt, True,
                               lambda qt, ka, vt: _attn(qt, ka, vt, tq, heads_per_step))

    cl_pad = jnp.pad(cache_logf[0], ((0, 0), (0, 0), (0, LANES - N_HEADS)))
    kc, vc, f_tot = _cache_aug(cache_k[0].reshape(db, p, ATT_DIM), cache_v[0].reshape(db, p, ATT_DIM),
                               cl_pad, _tri(tile_cache), pk, tile_cache)
    ys, ks, vs, ls, cs = layer(x_sample, slice(nb, nb + db), cache_conv[0], f_tot, t, False,
                               lambda qa, ka, va: _attn_sample(qa, kc, vc, ka, va))
    return (yp, ys, kp, vp, lp, cp, ks, vs, ls, cs)
```

```python
import functools

import jax
import jax.numpy as jnp
from jax import lax
from jax.experimental import pallas as pl
from jax.experimental.pallas import tpu as pltpu

F32 = jnp.float32
BF16 = jnp.bfloat16

D_MODEL = 1024
N_HEADS = 8
HEAD_DIM = 64
ATT_DIM = N_HEADS * HEAD_DIM
CONV_DIM = 512
CONV_WIDTH = 3
D_FF = 4 * D_MODEL
NORM_EPS = 1e-6
ATT_SCALE = HEAD_DIM ** -0.5

LANES = 128
CARRY_ROWS = 8
N_SPLIT = 3
ONES_ROW = N_SPLIT * N_HEADS
VMEM_LIMIT_BYTES = 56 * 1024 * 1024
BF16_ROWS = 16
SHIFT_ROW = 8
E_F0 = 16
ROW_CHUNKS = 2
LOG2E = 1.4426950408889634

SEC_Q, SEC_K, SEC_V, SEC_BG, SEC_CG, SEC_U = range(6)
SEC_W = 512
F_COL = 6 * SEC_W
PROJ_COLS = F_COL + LANES


def _const_spec(shape):
    zeros = (0,) * len(shape)
    return pl.BlockSpec(shape, lambda *_: zeros, pipeline_mode=pl.Buffered(1))


def _rms(x):
    return x * lax.rsqrt(jnp.mean(x * x, axis=-1, keepdims=True) + NORM_EPS)


def _split3(x):
    hi = x.astype(BF16).astype(F32)
    r = x - hi
    mid = r.astype(BF16).astype(F32)
    lo = (r - mid).astype(BF16).astype(F32)
    return hi, mid, lo


def _cumsum_rows(l128, tri_ref, lane):
    hi, mid, lo = _split3(l128)
    cat = hi + pltpu.roll(mid, N_HEADS, axis=1) + pltpu.roll(lo, 2 * N_HEADS, axis=1)
    sums = jnp.dot(tri_ref[...], cat.astype(BF16), preferred_element_type=F32)
    total = ((sums + pltpu.roll(sums, LANES - N_HEADS, axis=1))
             + pltpu.roll(sums, LANES - 2 * N_HEADS, axis=1))
    return jnp.where(lane < N_HEADS, total, 0.0)


def _f_parts(f128):
    hi, mid, lo = _split3(f128 * LOG2E)
    return hi + pltpu.roll(mid, N_HEADS, axis=1) + pltpu.roll(lo, 2 * N_HEADS, axis=1)


def _fcat(parts, lane):
    return (parts + jnp.where(lane == ONES_ROW, 1.0, 0.0)).astype(BF16)


def _k_extras(parts, lane):
    ones = jnp.where((lane < N_SPLIT) | ((lane >= SHIFT_ROW) & (lane < SHIFT_ROW + N_SPLIT)), 1.0, 0.0)
    odd = ones - pltpu.roll(parts, E_F0, axis=1)
    return pltpu.roll(odd, HEAD_DIM, axis=1), odd


def _store_aug(dst_ref, main, extras_even, extras_odd, lane):
    low = lane < HEAD_DIM
    for pair in range(N_HEADS // 2):
        m = main[:, pair * LANES:(pair + 1) * LANES]
        dst_ref[0, 2 * pair] = jnp.where(low, m, extras_even(pair)).astype(BF16)
        dst_ref[0, 2 * pair + 1] = jnp.where(low, extras_odd(pair), m).astype(BF16)


def _store_v_aug(dst_ref, v, lane):
    even = jnp.where(lane == HEAD_DIM, 1.0, 0.0)
    odd = jnp.where(lane == 0, 1.0, 0.0)
    _store_aug(dst_ref, v, lambda pair: even, lambda pair: odd, lane)


def _ada_kernel(c_ref, w_ref, b_ref, o_ref):
    c = c_ref[...]
    sc = (c * jax.nn.sigmoid(c)).astype(BF16)
    o_ref[...] = jnp.dot(sc, w_ref[...].astype(BF16), preferred_element_type=F32) + b_ref[...]


def _ada(c, w, b):
    rows, n = c.shape[0], w.shape[1]
    bn = 1024
    return pl.pallas_call(
        _ada_kernel,
        grid=(n // bn,),
        in_specs=[pl.BlockSpec((rows, D_MODEL), lambda j: (0, 0)),
                  pl.BlockSpec((D_MODEL, bn), lambda j: (0, j)),
                  pl.BlockSpec((1, bn), lambda j: (0, j))],
        out_specs=pl.BlockSpec((rows, bn), lambda j: (0, j)),
        out_shape=jax.ShapeDtypeStruct((rows, n), F32),
        name="ada",
    )(c, w, b.reshape(1, n))


def _store_qv_transposed(qa_ref, va_ref, qv_t, eq_t, tile):
    ones_row = jnp.where(lax.broadcasted_iota(jnp.int32, (HEAD_DIM, tile), 0) == 0, 1.0, 0.0)
    for h in range(N_HEADS):
        rows = slice(h * HEAD_DIM, (h + 1) * HEAD_DIM)
        parts = [qv_t[rows], eq_t[rows]]
        qa_ref[0, h] = jnp.concatenate(parts if h % 2 == 0 else parts[::-1], axis=0).astype(BF16)
        v_rows = slice(ATT_DIM + h * HEAD_DIM, ATT_DIM + (h + 1) * HEAD_DIM)
        va_ref[0, h] = jnp.concatenate([qv_t[v_rows], ones_row], axis=0).astype(BF16)


def _proj_kernel(x_ref, sh_ref, sc_ref, g1_ref, w_ref, wt_ref, bf_ref, wconv_ref, gconv_ref, prev_ref,
                 f0_ref, tri_ref, pq_ref, pqt_ref,
                 k_ref, v_ref, logf_ref, qa_ref, ka_ref, va_ref, cn_ref, cs_ref,
                 ue_ref, fc_ref, *, tile, transposed_qv):
    s = pl.program_id(1)

    @pl.when(s == 0)
    def _():
        ue_ref[0:CARRY_ROWS, :] = jnp.zeros((CARRY_ROWS, CONV_DIM), F32)
        ue_ref[CARRY_ROWS - (CONV_WIDTH - 1):CARRY_ROWS, :] = prev_ref[0]
        fc_ref[...] = f0_ref[0]

    lane = lax.broadcasted_iota(jnp.int32, (tile, LANES), 1)
    nt = (((1,), (1,)), ((), ()))
    gain = g1_ref[...] * (1.0 + sc_ref[0])
    shift = sh_ref[0]

    chunk = tile // ROW_CHUNKS if tile % (ROW_CHUNKS * BF16_ROWS) == 0 else tile
    starts = list(range(0, tile, chunk))

    def conv_matmuls(r0):
        hb_c = (_rms(x_ref[0, r0:r0 + chunk, :]) * gain + shift).astype(BF16)
        sec = lambda i: jnp.dot(hb_c, w_ref[:, i * SEC_W:(i + 1) * SEC_W], preferred_element_type=F32)
        return hb_c, sec(SEC_CG), sec(SEC_U), sec(SEC_BG)

    def conv_elementwise(r0, cg, u, bg):
        up = cg * u
        base = CARRY_ROWS + r0
        ue_ref[base:base + chunk, :] = up
        cv = (wconv_ref[0:1, :] * ue_ref[base - 2:base - 2 + chunk, :]
              + wconv_ref[1:2, :] * ue_ref[base - 1:base - 1 + chunk, :]
              + wconv_ref[2:3, :] * up)
        cn_ref[0, r0:r0 + chunk, :] = (_rms(bg * cv) * gconv_ref[...]).astype(BF16)

    hbs, pending = [], None
    for r0 in starts:
        hb_c, cg, u, bg = conv_matmuls(r0)
        hbs.append(hb_c)
        if pending is not None:
            conv_elementwise(*pending)
        pending = (r0, cg, u, bg)
    hb = jnp.concatenate(hbs, axis=0) if len(hbs) > 1 else hbs[0]

    def proj(sec, width=SEC_W):
        return jnp.dot(hb, w_ref[:, sec * SEC_W:sec * SEC_W + width], preferred_element_type=F32)

    if transposed_qv:
        fl = lax.dot_general(wt_ref[2 * ATT_DIM:, :], hb, nt, preferred_element_type=F32).T
    else:
        fl = proj(6, LANES)
    k = proj(SEC_K)
    k_ref[0] = k
    conv_elementwise(*pending)
    ue_ref[0:CARRY_ROWS, :] = ue_ref[tile:tile + CARRY_ROWS, :]

    fl = fl + bf_ref[...]
    logf = jnp.minimum(fl, 0.0) - jnp.log1p(jnp.exp(-jnp.abs(fl)))
    logf = jnp.where(lane < N_HEADS, logf, 0.0)
    logf_ref[0] = logf[:, :N_HEADS]
    f128 = _cumsum_rows(logf, tri_ref, lane) + fc_ref[...]
    v = proj(SEC_V)
    v_ref[0] = v
    fc_ref[...] = f128[tile - 1:tile, :]
    parts = _f_parts(f128)
    fcat = _fcat(parts, lane)

    ke_even, ke_odd = _k_extras(parts, lane)
    _store_aug(ka_ref, k, lambda pair: ke_even, lambda pair: ke_odd, lane)
    if transposed_qv:
        eq_t = lax.dot_general(pqt_ref[...], fcat, nt, preferred_element_type=F32)
        qv_t = lax.dot_general(wt_ref[:2 * ATT_DIM, :], hb, nt, preferred_element_type=F32)
        _store_qv_transposed(qa_ref, va_ref, qv_t, eq_t, tile)
    else:
        eq = jnp.dot(fcat, pq_ref[...], preferred_element_type=F32)
        _store_aug(qa_ref, proj(SEC_Q), lambda pair: eq[:, 2 * pair * LANES:(2 * pair + 1) * LANES],
                   lambda pair: eq[:, (2 * pair + 1) * LANES:(2 * pair + 2) * LANES], lane)
        _store_v_aug(va_ref, v, lane)

    @pl.when(s == pl.num_programs(1) - 1)
    def _():
        cs_ref[0] = ue_ref[CARRY_ROWS + tile - (CONV_WIDTH - 1):CARRY_ROWS + tile, :]


def _proj(x, sh, sc, g1, w_pack, w_t, bf_pad, w_conv, g_conv, prev, f0, tri, pq, pq_t, tile,
          transposed_qv):
    b, s, _ = x.shape
    assert s % tile == 0 and tile % CARRY_ROWS == 0
    ns = s // tile
    tok = lambda n: pl.BlockSpec((1, tile, n), lambda i, j: (i, j, 0))
    per_b = lambda r, n: pl.BlockSpec((1, r, n), lambda i, j: (i, 0, 0))
    aug = pl.BlockSpec((1, N_HEADS, tile, LANES), lambda i, j: (i, 0, j, 0))
    aug_sds = jax.ShapeDtypeStruct((b, N_HEADS, s, LANES), BF16)
    if transposed_qv:
        qv = pl.BlockSpec((1, N_HEADS, LANES, tile), lambda i, j: (i, 0, 0, j))
        qv_sds = jax.ShapeDtypeStruct((b, N_HEADS, LANES, s), BF16)
    else:
        qv, qv_sds = aug, aug_sds
    out_shape = (
        jax.ShapeDtypeStruct((b, s, ATT_DIM), F32),
        jax.ShapeDtypeStruct((b, s, ATT_DIM), F32),
        jax.ShapeDtypeStruct((b, s, N_HEADS), F32),
        qv_sds,
        aug_sds,
        qv_sds,
        jax.ShapeDtypeStruct((b, s, CONV_DIM), BF16),
        jax.ShapeDtypeStruct((b, CONV_WIDTH - 1, CONV_DIM), F32),
    )
    return pl.pallas_call(
        functools.partial(_proj_kernel, tile=tile, transposed_qv=transposed_qv),
        grid=(b, ns),
        in_specs=[tok(D_MODEL), per_b(1, D_MODEL), per_b(1, D_MODEL), _const_spec((1, D_MODEL)),
                  _const_spec((D_MODEL, PROJ_COLS)), _const_spec((2 * ATT_DIM + LANES, D_MODEL)),
                  _const_spec((1, LANES)),
                  _const_spec((CONV_WIDTH, CONV_DIM)), _const_spec((1, CONV_DIM)),
                  per_b(CONV_WIDTH - 1, CONV_DIM), per_b(1, LANES),
                  _const_spec((tile, tile)), _const_spec((LANES, N_HEADS * LANES)),
                  _const_spec((ATT_DIM, LANES))],
        out_specs=(tok(ATT_DIM), tok(ATT_DIM), tok(N_HEADS), qv, aug, qv, tok(CONV_DIM),
                   per_b(CONV_WIDTH - 1, CONV_DIM)),
        out_shape=out_shape,
        scratch_shapes=[pltpu.VMEM((tile + CARRY_ROWS, CONV_DIM), F32), pltpu.VMEM((1, LANES), F32)],
        compiler_params=pltpu.CompilerParams(dimension_semantics=("arbitrary", "arbitrary"),
                                             vmem_limit_bytes=VMEM_LIMIT_BYTES),
        name="proj",
    )(x, sh, sc, g1, w_pack, w_t, bf_pad, w_conv, g_conv, prev, f0, tri, pq, pq_t)


def _cache_kernel(k_ref, v_ref, l_ref, tri_ref, ka_ref, va_ref, ft_ref, fc_ref, *, tile):
    s = pl.program_id(1)

    @pl.when(s == 0)
    def _():
        fc_ref[...] = jnp.zeros((1, LANES), F32)

    lane = lax.broadcasted_iota(jnp.int32, (tile, LANES), 1)
    f128 = _cumsum_rows(l_ref[0], tri_ref, lane) + fc_ref[...]
    fc_ref[...] = f128[tile - 1:tile, :]
    ke_even, ke_odd = _k_extras(_f_parts(f128), lane)
    _store_aug(ka_ref, k_ref[0], lambda pair: ke_even, lambda pair: ke_odd, lane)
    _store_v_aug(va_ref, v_ref[0], lane)

    @pl.when(s == pl.num_programs(1) - 1)
    def _():
        ft_ref[0] = fc_ref[...]


def _cache_aug(ck, cv, cl_pad, tri, tile):
    b, p, _ = ck.shape
    tok = lambda n: pl.BlockSpec((1, tile, n), lambda i, j: (i, j, 0))
    aug = pl.BlockSpec((1, N_HEADS, tile, LANES), lambda i, j: (i, 0, j, 0))
    return pl.pallas_call(
        functools.partial(_cache_kernel, tile=tile),
        grid=(b, p // tile),
        in_specs=[tok(ATT_DIM), tok(ATT_DIM), tok(LANES), _const_spec((tile, tile))],
        out_specs=(aug, aug, pl.BlockSpec((1, 1, LANES), lambda i, j: (i, 0, 0))),
        out_shape=(jax.ShapeDtypeStruct((b, N_HEADS, p, LANES), BF16),
                   jax.ShapeDtypeStruct((b, N_HEADS, p, LANES), BF16),
                   jax.ShapeDtypeStruct((b, 1, LANES), F32)),
        scratch_shapes=[pltpu.VMEM((1, LANES), F32)],
        compiler_params=pltpu.CompilerParams(dimension_semantics=("arbitrary", "arbitrary"),
                                             vmem_limit_bytes=VMEM_LIMIT_BYTES),
        name="cache_aug",
    )(ck, cv, cl_pad, tri)


def _softmax_block(q, k, v, m, acc, mask):
    s = lax.dot_general(q, k, (((1,), (1,)), ((), ())), preferred_element_type=F32)
    if mask is not None:
        s = jnp.where(mask, s, -jnp.inf)
    m_new = jnp.maximum(m, jnp.max(s, axis=1, keepdims=True))
    p = jnp.exp2(s - m_new)
    acc = acc * jnp.exp2(m - m_new) + jnp.dot(p.astype(BF16), v, preferred_element_type=F32)
    return m_new, acc


def _normalise_pair(acc_even, acc_odd, lane):
    o_even = acc_even * (1.0 / acc_even[:, HEAD_DIM:HEAD_DIM + 1])
    o_odd = acc_odd * (1.0 / acc_odd[:, 0:1])
    return jnp.where(lane < HEAD_DIM, o_even, o_odd)


def _softmax_block_t(qt, k, vt, m, acc, mask):
    s = jnp.dot(k, qt, preferred_element_type=F32)
    if mask is not None:
        s = jnp.where(mask, s, -jnp.inf)
    m_new = jnp.maximum(m, jnp.max(s, axis=0, keepdims=True))
    p = jnp.exp2(s - m_new)
    acc = acc * jnp.exp2(m - m_new) + jnp.dot(vt, p.astype(BF16), preferred_element_type=F32)
    return m_new, acc


def _head_online(qt, kv_block, qi, causal, tq):
    def body(j, carry):
        k, vt = kv_block(j)
        return _softmax_block_t(qt, k, vt, carry[0], carry[1], None)

    init = (jnp.full((1, tq), -jnp.inf, F32), jnp.zeros((LANES, tq), F32))
    m, acc = lax.fori_loop(0, qi, body, init)
    k, vt = kv_block(qi)
    return _softmax_block_t(qt, k, vt, m, acc, causal)[1]


def _with_shift_rows(qt, m, parity):
    g0 = HEAD_DIM if parity == 0 else 0
    hi, mid, lo = _split3(-m)
    row = lax.broadcasted_iota(jnp.int32, (BF16_ROWS, m.shape[1]), 0)
    add = jnp.where(row == SHIFT_ROW, hi,
                    jnp.where(row == SHIFT_ROW + 1, mid, jnp.where(row == SHIFT_ROW + 2, lo, 0.0)))
    grp = (qt[g0:g0 + BF16_ROWS].astype(F32) + add).astype(BF16)
    pieces = ([qt[:g0]] if g0 else []) + [grp, qt[g0 + BF16_ROWS:]]
    return jnp.concatenate(pieces, axis=0)


def _heads_fixed_max(qts, kv_block, qi, causal, acc_ref, qs_ref):
    n = len(qts)
    dot = functools.partial(jnp.dot, preferred_element_type=F32)
    diag = lambda hh: jnp.where(causal, dot(kv_block(hh, qi)[0], qts[hh]), -jnp.inf)
    s_next = diag(0)
    for hh in range(n):
        s, s_next = s_next, (diag(hh + 1) if hh + 1 < n else None)
        m = jnp.max(s, axis=0, keepdims=True)
        qs_ref[hh] = _with_shift_rows(qts[hh], m, hh % 2)
        acc_ref[hh] = dot(kv_block(hh, qi)[1], jnp.exp2(s - m).astype(BF16))

    def body(j, carry):
        scores = lambda hh: dot(kv_block(hh, j)[0], qs_ref[hh])
        s_next = scores(0)
        for hh in range(n):
            s, s_next = s_next, (scores(hh + 1) if hh + 1 < n else None)
            acc_ref[hh] += dot(kv_block(hh, j)[1], jnp.exp2(s).astype(BF16))
        return carry

    lax.fori_loop(0, qi, body, 0)
    return [acc_ref[hh] for hh in range(n)]


def _attn_kernel(qt_ref, k_ref, vt_ref, o_ref, acc_ref, qs_ref, *, tq):
    qi = pl.program_id(2)
    causal = (lax.broadcasted_iota(jnp.int32, (tq, tq), 0)
              <= lax.broadcasted_iota(jnp.int32, (tq, tq), 1))

    def kv_block(hh, j):
        off = pl.multiple_of(j * tq, tq)
        return k_ref[0, hh, pl.ds(off, tq), :], vt_ref[0, hh, :, pl.ds(off, tq)]

    heads = qt_ref.shape[1]
    qts = [qt_ref[0, hh] for hh in range(heads)]
    outs = []
    for hh, acc in enumerate(_heads_fixed_max(qts, kv_block, qi, causal, acc_ref, qs_ref)):
        finite = jnp.min(jnp.where(jnp.isfinite(acc), 1.0, 0.0)) > 0.5
        redo = functools.partial(_head_online, qts[hh], functools.partial(kv_block, hh), qi, causal, tq)
        acc = lax.cond(finite, lambda acc=acc: acc, redo)
        outs.append(acc[0:HEAD_DIM] * (1.0 / acc[HEAD_DIM:HEAD_DIM + 1]))
    for pair in range(heads // 2):
        both = jnp.concatenate([outs[2 * pair], outs[2 * pair + 1]], axis=0)
        o_ref[0, :, pair * LANES:(pair + 1) * LANES] = both.T


def _attn(qt, ka, vt, tq, heads):
    b, _, s, _ = ka.shape
    assert heads % 2 == 0 and N_HEADS % heads == 0
    return pl.pallas_call(
        functools.partial(_attn_kernel, tq=tq),
        grid=(b, N_HEADS // heads, s // tq),
        in_specs=[pl.BlockSpec((1, heads, LANES, tq), lambda i, p, j: (i, p, 0, j)),
                  pl.BlockSpec((1, heads, s, LANES), lambda i, p, j: (i, p, 0, 0)),
                  pl.BlockSpec((1, heads, LANES, s), lambda i, p, j: (i, p, 0, 0))],
        out_specs=pl.BlockSpec((1, tq, heads * HEAD_DIM), lambda i, p, j: (i, j, p)),
        out_shape=jax.ShapeDtypeStruct((b, s, ATT_DIM), F32),
        scratch_shapes=[pltpu.VMEM((heads, LANES, tq), F32), pltpu.VMEM((heads, LANES, tq), BF16)],
        compiler_params=pltpu.CompilerParams(
            dimension_semantics=("arbitrary", "arbitrary", "arbitrary"),
            vmem_limit_bytes=VMEM_LIMIT_BYTES),
        name="attn",
    )(qt, ka, vt)


def _attn_sample_kernel(q_ref, kc_ref, vc_ref, kn_ref, vn_ref, o_ref, *, t):
    lane = lax.broadcasted_iota(jnp.int32, (t, LANES), 1)
    causal = (lax.broadcasted_iota(jnp.int32, (t, LANES), 1)
              <= lax.broadcasted_iota(jnp.int32, (t, LANES), 0))
    pad = jnp.zeros((LANES - t, LANES), BF16)
    accs = []
    for hh in range(2):
        q = q_ref[0, hh]
        init = (jnp.full((t, 1), -jnp.inf, F32), jnp.zeros((t, LANES), F32))
        m, acc = _softmax_block(q, kc_ref[0, hh], vc_ref[0, hh], init[0], init[1], None)
        kn = jnp.concatenate([kn_ref[0, hh], pad], axis=0)
        vn = jnp.concatenate([vn_ref[0, hh], pad], axis=0)
        _, acc = _softmax_block(q, kn, vn, m, acc, causal)
        accs.append(acc)
    o_ref[0] = _normalise_pair(accs[0], accs[1], lane)


def _attn_sample(qa, kc, vc, kn, vn):
    b, _, t, _ = qa.shape
    p = kc.shape[2]
    new = pl.BlockSpec((1, 2, t, LANES), lambda i, h: (i, h, 0, 0))
    old = pl.BlockSpec((1, 2, p, LANES), lambda i, h: (i, h, 0, 0))
    return pl.pallas_call(
        functools.partial(_attn_sample_kernel, t=t),
        grid=(b, N_HEADS // 2),
        in_specs=[new, old, old, new, new],
        out_specs=pl.BlockSpec((1, t, LANES), lambda i, h: (i, 0, h)),
        out_shape=jax.ShapeDtypeStruct((b, t, ATT_DIM), F32),
        compiler_params=pltpu.CompilerParams(dimension_semantics=("arbitrary", "arbitrary"),
                                             vmem_limit_bytes=VMEM_LIMIT_BYTES),
        name="attn_sample",
    )(qa, kc, vc, kn, vn)


def _post_kernel(x_ref, att_ref, cn_ref, gt1_ref, sh2_ref, sc2_ref, gt2_ref, shf_ref, scf_ref,
                 gatt_ref, g2_ref, gf_ref, wo_ref, wu_ref, wd_ref, y_ref, *, ff_chunk):
    xa = (_rms(att_ref[0]) * gatt_ref[...]).astype(BF16)
    mixed = (jnp.dot(xa, wo_ref[0:ATT_DIM, :], preferred_element_type=F32)
             + jnp.dot(cn_ref[0], wo_ref[ATT_DIM:, :], preferred_element_type=F32))
    x1 = x_ref[0] + gt1_ref[0] * mixed
    h2 = ((_rms(x1) * g2_ref[...]) * (1.0 + sc2_ref[0]) + sh2_ref[0]).astype(BF16)
    mlp = None
    for c in range(D_FF // ff_chunk):
        up = jnp.dot(h2, wu_ref[:, c * ff_chunk:(c + 1) * ff_chunk], preferred_element_type=F32)
        act = jnp.square(jnp.maximum(up, 0.0)).astype(BF16)
        part = jnp.dot(act, wd_ref[c * ff_chunk:(c + 1) * ff_chunk, :], preferred_element_type=F32)
        mlp = part if mlp is None else mlp + part
    x2 = x1 + gt2_ref[0] * mlp
    y_ref[0] = (_rms(x2) * gf_ref[...]) * (1.0 + scf_ref[0]) + shf_ref[0]


def _post(x, att, cn, mods, g_att, g2, g_final, w_out, w_up, w_down, tile):
    b, s, _ = x.shape
    tok = lambda n: pl.BlockSpec((1, tile, n), lambda i, j: (i, j, 0))
    per_b = pl.BlockSpec((1, 1, D_MODEL), lambda i, j: (i, 0, 0))
    return pl.pallas_call(
        functools.partial(_post_kernel, ff_chunk=1024),
        grid=(b, s // tile),
        in_specs=[tok(D_MODEL), tok(ATT_DIM), tok(CONV_DIM)] + [per_b] * 6
                 + [_const_spec((1, ATT_DIM)), _const_spec((1, D_MODEL)), _const_spec((1, D_MODEL)),
                    _const_spec((D_MODEL, D_MODEL)), _const_spec((D_MODEL, D_FF)),
                    _const_spec((D_FF, D_MODEL))],
        out_specs=tok(D_MODEL),
        out_shape=jax.ShapeDtypeStruct((b, s, D_MODEL), F32),
        compiler_params=pltpu.CompilerParams(dimension_semantics=("arbitrary", "arbitrary"),
                                             vmem_limit_bytes=VMEM_LIMIT_BYTES),
        name="post",
    )(x, att, cn, *mods, g_att, g2, g_final, w_out, w_up, w_down)


def _placement():
    pq = jnp.zeros((LANES, N_HEADS * LANES), F32)
    pq_t = jnp.zeros((ATT_DIM, LANES), F32)
    for h in range(N_HEADS):
        base = h * LANES + (HEAD_DIM if h % 2 == 0 else 0)
        for part in range(N_SPLIT):
            src, own = part * N_HEADS + h, E_F0 + part * N_HEADS + h
            pq = pq.at[src, base + part].set(1.0)
            pq = pq.at[ONES_ROW, base + own].set(1.0)
            pq_t = pq_t.at[h * HEAD_DIM + part, src].set(1.0)
            pq_t = pq_t.at[h * HEAD_DIM + own, ONES_ROW].set(1.0)
    return pq.astype(BF16), pq_t.astype(BF16)


def _tri(n):
    return (jnp.arange(n)[None, :] <= jnp.arange(n)[:, None]).astype(BF16)


def _pack_w_in(w):
    a, c = ATT_DIM, CONV_DIM
    f0 = 3 * a
    b0 = f0 + N_HEADS
    pad = jnp.zeros((D_MODEL, LANES - N_HEADS), w.dtype)
    wq = w[:, :a] * (ATT_SCALE * LOG2E)
    packed = jnp.concatenate([wq, w[:, a:f0], w[:, b0:b0 + 3 * c], w[:, f0:b0], pad], axis=1)
    transposed = jnp.concatenate([wq, w[:, 2 * a:f0], w[:, f0:b0], pad], axis=1).T
    return packed.astype(BF16), transposed.astype(BF16)


def kernel(x_prompt, x_sample, cache_k, cache_v, cache_logf, cache_conv, c_prompt, c_sample, w_ada, b_ada, g_norm1, g_norm2, w_in, b_f, w_conv, g_attn_out, g_conv_out, w_out, w_up, w_down, w_ada_final, b_ada_final, g_final):
    assert w_ada.shape[0] == 1, "single layer"
    nb, s, _ = x_prompt.shape
    db, t, _ = x_sample.shape
    p = cache_k.shape[2]
    tile_prompt, tile_cache, tq, heads_per_step = 512, 512, 512, 8

    c_all = jnp.concatenate([c_prompt, c_sample], axis=0)
    mod = _ada(c_all, w_ada[0], b_ada[0])
    mod_f = _ada(c_all, w_ada_final, b_ada_final)
    sh1, sc1, gt1, sh2, sc2, gt2 = [m[:, None, :] for m in jnp.split(mod, 6, axis=-1)]
    shf, scf = [m[:, None, :] for m in jnp.split(mod_f, 2, axis=-1)]

    w_pack, w_t = _pack_w_in(w_in[0])
    bf_pad = jnp.pad(b_f[0], (0, LANES - N_HEADS)).reshape(1, LANES)
    pq, pq_t = _placement()
    g1 = g_norm1[0].reshape(1, D_MODEL)
    g2 = g_norm2[0].reshape(1, D_MODEL)
    gf = g_final.reshape(1, D_MODEL)
    g_att = g_attn_out[0].reshape(1, ATT_DIM)
    g_conv = g_conv_out[0].reshape(1, CONV_DIM)
    wo, wu, wd = w_out[0].astype(BF16), w_up[0].astype(BF16), w_down[0].astype(BF16)

    def layer(x, rows, prev, f0, tile, transposed_qv, attend):
        sel = lambda m: m[rows]
        k, v, logf, qa, ka, va, cn, cs = _proj(x, sel(sh1), sel(sc1), g1, w_pack, w_t, bf_pad, w_conv[0],
                                               g_conv, prev, f0, _tri(tile), pq, pq_t, tile,
                                               transposed_qv)
        att = attend(qa, ka, va)
        mods = [sel(m) for m in (gt1, sh2, sc2, gt2, shf, scf)]
        y = _post(x, att, cn, mods, g_att, g2, gf, wo, wu, wd, tile)
        bsz, sl = x.shape[0], x.shape[1]
        heads = lambda a: a.reshape(1, bsz, sl, N_HEADS, HEAD_DIM)
        return y, heads(k), heads(v), logf[None], cs[None]

    zeros_prev = jnp.zeros((nb, CONV_WIDTH - 1, CONV_DIM), F32)
    zeros_f = jnp.zeros((nb, 1, LANES), F32)
    yp, kp, vp, lp, cp = layer(x_prompt, slice(0, nb), zeros_prev, zeros_f, tile_prompt, True,
                               lambda qt, ka, vt: _attn(qt, ka, vt, tq, heads_per_step))

    cl_pad = jnp.pad(cache_logf[0], ((0, 0), (0, 0), (0, LANES - N_HEADS)))
    kc, vc, f_tot = _cache_aug(cache_k[0].reshape(db, p, ATT_DIM), cache_v[0].reshape(db, p, ATT_DIM),
                               cl_pad, _tri(tile_cache), tile_cache)
    ys, ks, vs, ls, cs = layer(x_sample, slice(nb, nb + db), cache_conv[0], f_tot, t, False,
                               lambda qa, ka, va: _attn_sample(qa, kc, vc, ka, va))
    return (yp, ys, kp, vp, lp, cp, ks, vs, ls, cs)
```

```python
import functools

import jax
import jax.numpy as jnp
from jax import lax
from jax.experimental import pallas as pl
from jax.experimental.pallas import tpu as pltpu

F32 = jnp.float32
BF16 = jnp.bfloat16

D_MODEL = 1024
N_HEADS = 8
HEAD_DIM = 64
ATT_DIM = N_HEADS * HEAD_DIM
CONV_DIM = 512
CONV_WIDTH = 3
D_FF = 4 * D_MODEL
NORM_EPS = 1e-6
ATT_SCALE = HEAD_DIM ** -0.5

LANES = 128
CARRY_ROWS = 8
N_SPLIT = 3
ONES_ROW = N_SPLIT * N_HEADS
VMEM_LIMIT_BYTES = 56 * 1024 * 1024
BF16_ROWS = 16
SHIFT_ROW = 8
E_F0 = 16
ROW_CHUNKS = 2
LOG2E = 1.4426950408889634

SEC_Q, SEC_K, SEC_V, SEC_BG, SEC_CG, SEC_U = range(6)
SEC_W = 512
F_COL = 6 * SEC_W
PROJ_COLS = F_COL + LANES


def _const_spec(shape):
    zeros = (0,) * len(shape)
    return pl.BlockSpec(shape, lambda *_: zeros, pipeline_mode=pl.Buffered(1))


def _rms(x):
    return x * lax.rsqrt(jnp.mean(x * x, axis=-1, keepdims=True) + NORM_EPS)


def _split3(x):
    hi = x.astype(BF16).astype(F32)
    r = x - hi
    mid = r.astype(BF16).astype(F32)
    lo = (r - mid).astype(BF16).astype(F32)
    return hi, mid, lo


def _cumsum_rows(l128, tri_ref, lane):
    hi, mid, lo = _split3(l128)
    cat = hi + pltpu.roll(mid, N_HEADS, axis=1) + pltpu.roll(lo, 2 * N_HEADS, axis=1)
    sums = jnp.dot(tri_ref[...], cat.astype(BF16), preferred_element_type=F32)
    total = ((sums + pltpu.roll(sums, LANES - N_HEADS, axis=1))
             + pltpu.roll(sums, LANES - 2 * N_HEADS, axis=1))
    return jnp.where(lane < N_HEADS, total, 0.0)


def _f_parts(f128):
    hi, mid, lo = _split3(f128 * LOG2E)
    return hi + pltpu.roll(mid, N_HEADS, axis=1) + pltpu.roll(lo, 2 * N_HEADS, axis=1)


def _fcat(parts, lane):
    return (parts + jnp.where(lane == ONES_ROW, 1.0, 0.0)).astype(BF16)


def _k_extras(parts, lane):
    ones = jnp.where((lane < N_SPLIT) | ((lane >= SHIFT_ROW) & (lane < SHIFT_ROW + N_SPLIT)), 1.0, 0.0)
    odd = ones - pltpu.roll(parts, E_F0, axis=1)
    return pltpu.roll(odd, HEAD_DIM, axis=1), odd


def _store_aug(dst_ref, main, extras_even, extras_odd, lane):
    low = lane < HEAD_DIM
    for pair in range(N_HEADS // 2):
        m = main[:, pair * LANES:(pair + 1) * LANES]
        dst_ref[0, 2 * pair] = jnp.where(low, m, extras_even(pair)).astype(BF16)
        dst_ref[0, 2 * pair + 1] = jnp.where(low, extras_odd(pair), m).astype(BF16)


def _store_v_aug(dst_ref, v, lane):
    even = jnp.where(lane == HEAD_DIM, 1.0, 0.0)
    odd = jnp.where(lane == 0, 1.0, 0.0)
    _store_aug(dst_ref, v, lambda pair: even, lambda pair: odd, lane)


def _ada_kernel(c_ref, w_ref, b_ref, o_ref):
    c = c_ref[...]
    sc = (c * jax.nn.sigmoid(c)).astype(BF16)
    o_ref[...] = jnp.dot(sc, w_ref[...].astype(BF16), preferred_element_type=F32) + b_ref[...]


def _ada(c, w, b):
    rows, n = c.shape[0], w.shape[1]
    bn = 1024
    return pl.pallas_call(
        _ada_kernel,
        grid=(n // bn,),
        in_specs=[pl.BlockSpec((rows, D_MODEL), lambda j: (0, 0)),
                  pl.BlockSpec((D_MODEL, bn), lambda j: (0, j)),
                  pl.BlockSpec((1, bn), lambda j: (0, j))],
        out_specs=pl.BlockSpec((rows, bn), lambda j: (0, j)),
        out_shape=jax.ShapeDtypeStruct((rows, n), F32),
        name="ada",
    )(c, w, b.reshape(1, n))


def _store_qv_transposed(qa_ref, va_ref, qv_t, eq_t, tile):
    ones_row = jnp.where(lax.broadcasted_iota(jnp.int32, (HEAD_DIM, tile), 0) == 0, 1.0, 0.0)
    for h in range(N_HEADS):
        rows = slice(h * HEAD_DIM, (h + 1) * HEAD_DIM)
        parts = [qv_t[rows], eq_t[rows]]
        qa_ref[0, h] = jnp.concatenate(parts if h % 2 == 0 else parts[::-1], axis=0).astype(BF16)
        v_rows = slice(ATT_DIM + h * HEAD_DIM, ATT_DIM + (h + 1) * HEAD_DIM)
        va_ref[0, h] = jnp.concatenate([qv_t[v_rows], ones_row], axis=0).astype(BF16)


def _proj_kernel(x_ref, sh_ref, sc_ref, g1_ref, w_ref, wt_ref, bf_ref, wconv_ref, gconv_ref, prev_ref,
                 f0_ref, tri_ref, pq_ref, pqt_ref,
                 k_ref, v_ref, logf_ref, qa_ref, ka_ref, va_ref, cn_ref, cs_ref,
                 ue_ref, fc_ref, *, tile, transposed_qv):
    s = pl.program_id(1)

    @pl.when(s == 0)
    def _():
        ue_ref[0:CARRY_ROWS, :] = jnp.zeros((CARRY_ROWS, CONV_DIM), F32)
        ue_ref[CARRY_ROWS - (CONV_WIDTH - 1):CARRY_ROWS, :] = prev_ref[0]
        fc_ref[...] = f0_ref[0]

    lane = lax.broadcasted_iota(jnp.int32, (tile, LANES), 1)
    nt = (((1,), (1,)), ((), ()))
    gain = g1_ref[...] * (1.0 + sc_ref[0])
    shift = sh_ref[0]

    chunk = tile // ROW_CHUNKS if tile % (ROW_CHUNKS * BF16_ROWS) == 0 else tile
    starts = list(range(0, tile, chunk))

    def conv_matmuls(r0):
        hb_c = (_rms(x_ref[0, r0:r0 + chunk, :]) * gain + shift).astype(BF16)
        sec = lambda i: jnp.dot(hb_c, w_ref[:, i * SEC_W:(i + 1) * SEC_W], preferred_element_type=F32)
        return hb_c, sec(SEC_CG), sec(SEC_U), sec(SEC_BG)

    def conv_elementwise(r0, cg, u, bg):
        up = cg * u
        base = CARRY_ROWS + r0
        ue_ref[base:base + chunk, :] = up
        cv = (wconv_ref[0:1, :] * ue_ref[base - 2:base - 2 + chunk, :]
              + wconv_ref[1:2, :] * ue_ref[base - 1:base - 1 + chunk, :]
              + wconv_ref[2:3, :] * up)
        cn_ref[0, r0:r0 + chunk, :] = (_rms(bg * cv) * gconv_ref[...]).astype(BF16)

    hbs, pending = [], None
    for r0 in starts:
        hb_c, cg, u, bg = conv_matmuls(r0)
        hbs.append(hb_c)
        if pending is not None:
            conv_elementwise(*pending)
        pending = (r0, cg, u, bg)
    hb = jnp.concatenate(hbs, axis=0) if len(hbs) > 1 else hbs[0]

    def proj(sec, width=SEC_W):
        return jnp.dot(hb, w_ref[:, sec * SEC_W:sec * SEC_W + width], preferred_element_type=F32)

    if transposed_qv:
        fl = lax.dot_general(wt_ref[2 * ATT_DIM:, :], hb, nt, preferred_element_type=F32).T
    else:
        fl = proj(6, LANES)
    k = proj(SEC_K)
    k_ref[0] = k
    conv_elementwise(*pending)
    ue_ref[0:CARRY_ROWS, :] = ue_ref[tile:tile + CARRY_ROWS, :]

    fl = fl + bf_ref[...]
    logf = jnp.minimum(fl, 0.0) - jnp.log1p(jnp.exp(-jnp.abs(fl)))
    logf = jnp.where(lane < N_HEADS, logf, 0.0)
    logf_ref[0] = logf[:, :N_HEADS]
    f128 = _cumsum_rows(logf, tri_ref, lane) + fc_ref[...]
    v = proj(SEC_V)
    v_ref[0] = v
    fc_ref[...] = f128[tile - 1:tile, :]
    parts = _f_parts(f128)
    fcat = _fcat(parts, lane)

    ke_even, ke_odd = _k_extras(parts, lane)
    _store_aug(ka_ref, k, lambda pair: ke_even, lambda pair: ke_odd, lane)
    if transposed_qv:
        eq_t = lax.dot_general(pqt_ref[...], fcat, nt, preferred_element_type=F32)
        qv_t = lax.dot_general(wt_ref[:2 * ATT_DIM, :], hb, nt, preferred_element_type=F32)
        _store_qv_transposed(qa_ref, va_ref, qv_t, eq_t, tile)
    else:
        eq = jnp.dot(fcat, pq_ref[...], preferred_element_type=F32)
        _store_aug(qa_ref, proj(SEC_Q), lambda pair: eq[:, 2 * pair * LANES:(2 * pair + 1) * LANES],
                   lambda pair: eq[:, (2 * pair + 1) * LANES:(2 * pair + 2) * LANES], lane)
        _store_v_aug(va_ref, v, lane)

    @pl.when(s == pl.num_programs(1) - 1)
    def _():
        cs_ref[0] = ue_ref[CARRY_ROWS + tile - (CONV_WIDTH - 1):CARRY_ROWS + tile, :]


def _proj(x, sh, sc, g1, w_pack, w_t, bf_pad, w_conv, g_conv, prev, f0, tri, pq, pq_t, tile,
          transposed_qv):
    b, s, _ = x.shape
    assert s % tile == 0 and tile % CARRY_ROWS == 0
    ns = s // tile
    tok = lambda n: pl.BlockSpec((1, tile, n), lambda i, j: (i, j, 0))
    per_b = lambda r, n: pl.BlockSpec((1, r, n), lambda i, j: (i, 0, 0))
    aug = pl.BlockSpec((1, N_HEADS, tile, LANES), lambda i, j: (i, 0, j, 0))
    aug_sds = jax.ShapeDtypeStruct((b, N_HEADS, s, LANES), BF16)
    if transposed_qv:
        qv = pl.BlockSpec((1, N_HEADS, LANES, tile), lambda i, j: (i, 0, 0, j))
        qv_sds = jax.ShapeDtypeStruct((b, N_HEADS, LANES, s), BF16)
    else:
        qv, qv_sds = aug, aug_sds
    out_shape = (
        jax.ShapeDtypeStruct((b, s, ATT_DIM), F32),
        jax.ShapeDtypeStruct((b, s, ATT_DIM), F32),
        jax.ShapeDtypeStruct((b, s, N_HEADS), F32),
        qv_sds,
        aug_sds,
        qv_sds,
        jax.ShapeDtypeStruct((b, s, CONV_DIM), BF16),
        jax.ShapeDtypeStruct((b, CONV_WIDTH - 1, CONV_DIM), F32),
    )
    return pl.pallas_call(
        functools.partial(_proj_kernel, tile=tile, transposed_qv=transposed_qv),
        grid=(b, ns),
        in_specs=[tok(D_MODEL), per_b(1, D_MODEL), per_b(1, D_MODEL), _const_spec((1, D_MODEL)),
                  _const_spec((D_MODEL, PROJ_COLS)), _const_spec((2 * ATT_DIM + LANES, D_MODEL)),
                  _const_spec((1, LANES)),
                  _const_spec((CONV_WIDTH, CONV_DIM)), _const_spec((1, CONV_DIM)),
                  per_b(CONV_WIDTH - 1, CONV_DIM), per_b(1, LANES),
                  _const_spec((tile, tile)), _const_spec((LANES, N_HEADS * LANES)),
                  _const_spec((ATT_DIM, LANES))],
        out_specs=(tok(ATT_DIM), tok(ATT_DIM), tok(N_HEADS), qv, aug, qv, tok(CONV_DIM),
                   per_b(CONV_WIDTH - 1, CONV_DIM)),
        out_shape=out_shape,
        scratch_shapes=[pltpu.VMEM((tile + CARRY_ROWS, CONV_DIM), F32), pltpu.VMEM((1, LANES), F32)],
        compiler_params=pltpu.CompilerParams(dimension_semantics=("arbitrary", "arbitrary"),
                                             vmem_limit_bytes=VMEM_LIMIT_BYTES),
        name="proj",
    )(x, sh, sc, g1, w_pack, w_t, bf_pad, w_conv, g_conv, prev, f0, tri, pq, pq_t)


def _cache_kernel(k_ref, v_ref, l_ref, tri_ref, ka_ref, va_ref, ft_ref, fc_ref, *, tile):
    s = pl.program_id(1)

    @pl.when(s == 0)
    def _():
        fc_ref[...] = jnp.zeros((1, LANES), F32)

    lane = lax.broadcasted_iota(jnp.int32, (tile, LANES), 1)
    f128 = _cumsum_rows(l_ref[0], tri_ref, lane) + fc_ref[...]
    fc_ref[...] = f128[tile - 1:tile, :]
    ke_even, ke_odd = _k_extras(_f_parts(f128), lane)
    _store_aug(ka_ref, k_ref[0], lambda pair: ke_even, lambda pair: ke_odd, lane)
    _store_v_aug(va_ref, v_ref[0], lane)

    @pl.when(s == pl.num_programs(1) - 1)
    def _():
        ft_ref[0] = fc_ref[...]


def _cache_aug(ck, cv, cl_pad, tri, tile):
    b, p, _ = ck.shape
    tok = lambda n: pl.BlockSpec((1, tile, n), lambda i, j: (i, j, 0))
    aug = pl.BlockSpec((1, N_HEADS, tile, LANES), lambda i, j: (i, 0, j, 0))
    return pl.pallas_call(
        functools.partial(_cache_kernel, tile=tile),
        grid=(b, p // tile),
        in_specs=[tok(ATT_DIM), tok(ATT_DIM), tok(LANES), _const_spec((tile, tile))],
        out_specs=(aug, aug, pl.BlockSpec((1, 1, LANES), lambda i, j: (i, 0, 0))),
        out_shape=(jax.ShapeDtypeStruct((b, N_HEADS, p, LANES), BF16),
                   jax.ShapeDtypeStruct((b, N_HEADS, p, LANES), BF16),
                   jax.ShapeDtypeStruct((b, 1, LANES), F32)),
        scratch_shapes=[pltpu.VMEM((1, LANES), F32)],
        compiler_params=pltpu.CompilerParams(dimension_semantics=("arbitrary", "arbitrary"),
                                             vmem_limit_bytes=VMEM_LIMIT_BYTES),
        name="cache_aug",
    )(ck, cv, cl_pad, tri)


def _softmax_block(q, k, v, m, acc, mask):
    s = lax.dot_general(q, k, (((1,), (1,)), ((), ())), preferred_element_type=F32)
    if mask is not None:
        s = jnp.where(mask, s, -jnp.inf)
    m_new = jnp.maximum(m, jnp.max(s, axis=1, keepdims=True))
    p = jnp.exp2(s - m_new)
    acc = acc * jnp.exp2(m - m_new) + jnp.dot(p.astype(BF16), v, preferred_element_type=F32)
    return m_new, acc


def _normalise_pair(acc_even, acc_odd, lane):
    o_even = acc_even * (1.0 / acc_even[:, HEAD_DIM:HEAD_DIM + 1])
    o_odd = acc_odd * (1.0 / acc_odd[:, 0:1])
    return jnp.where(lane < HEAD_DIM, o_even, o_odd)


def _softmax_block_t(qt, k, vt, m, acc, mask):
    s = jnp.dot(k, qt, preferred_element_type=F32)
    if mask is not None:
        s = jnp.where(mask, s, -jnp.inf)
    m_new = jnp.maximum(m, jnp.max(s, axis=0, keepdims=True))
    p = jnp.exp2(s - m_new)
    acc = acc * jnp.exp2(m - m_new) + jnp.dot(vt, p.astype(BF16), preferred_element_type=F32)
    return m_new, acc


def _head_online(qt, kv_block, qi, causal, tq):
    def body(j, carry):
        k, vt = kv_block(j)
        return _softmax_block_t(qt, k, vt, carry[0], carry[1], None)

    init = (jnp.full((1, tq), -jnp.inf, F32), jnp.zeros((LANES, tq), F32))
    m, acc = lax.fori_loop(0, qi, body, init)
    k, vt = kv_block(qi)
    return _softmax_block_t(qt, k, vt, m, acc, causal)[1]


def _with_shift_rows(qt, m, parity):
    g0 = HEAD_DIM if parity == 0 else 0
    hi, mid, lo = _split3(-m)
    row = lax.broadcasted_iota(jnp.int32, (BF16_ROWS, m.shape[1]), 0)
    add = jnp.where(row == SHIFT_ROW, hi,
                    jnp.where(row == SHIFT_ROW + 1, mid, jnp.where(row == SHIFT_ROW + 2, lo, 0.0)))
    grp = (qt[g0:g0 + BF16_ROWS].astype(F32) + add).astype(BF16)
    pieces = ([qt[:g0]] if g0 else []) + [grp, qt[g0 + BF16_ROWS:]]
    return jnp.concatenate(pieces, axis=0)


def _heads_fixed_max(qts, kv_block, qi, causal, acc_ref, qs_ref):
    n = len(qts)
    dot = functools.partial(jnp.dot, preferred_element_type=F32)
    diag = lambda hh: jnp.where(causal, dot(kv_block(hh, qi)[0], qts[hh]), -jnp.inf)
    s_next = diag(0)
    for hh in range(n):
        s, s_next = s_next, (diag(hh + 1) if hh + 1 < n else None)
        m = jnp.max(s, axis=0, keepdims=True)
        qs_ref[hh] = _with_shift_rows(qts[hh], m, hh % 2)
        acc_ref[hh] = dot(kv_block(hh, qi)[1], jnp.exp2(s - m).astype(BF16))

    def body(j, carry):
        scores = lambda hh: dot(kv_block(hh, j)[0], qs_ref[hh])
        s_next = scores(0)
        for hh in range(n):
            s, s_next = s_next, (scores(hh + 1) if hh + 1 < n else None)
            acc_ref[hh] += dot(kv_block(hh, j)[1], jnp.exp2(s).astype(BF16))
        return carry

    lax.fori_loop(0, qi, body, 0)
    return [acc_ref[hh] for hh in range(n)]


def _attn_kernel(qt_ref, k_ref, vt_ref, o_ref, acc_ref, qs_ref, *, tq):
    qi = pl.program_id(2)
    causal = (lax.broadcasted_iota(jnp.int32, (tq, tq), 0)
              <= lax.broadcasted_iota(jnp.int32, (tq, tq), 1))

    def kv_block(hh, j):
        off = pl.multiple_of(j * tq, tq)
        return k_ref[0, hh, pl.ds(off, tq), :], vt_ref[0, hh, :, pl.ds(off, tq)]

    heads = qt_ref.shape[1]
    qts = [qt_ref[0, hh] for hh in range(heads)]

    def finish(accs):
        outs = [acc[0:HEAD_DIM] * (1.0 / acc[HEAD_DIM:HEAD_DIM + 1]) for acc in accs]
        finite = None
        for pair in range(heads // 2):
            both = jnp.concatenate([outs[2 * pair], outs[2 * pair + 1]], axis=0)
            o_ref[0, :, pair * LANES:(pair + 1) * LANES] = both.T
            ok = jnp.where(jnp.isfinite(both), 1.0, 0.0)
            finite = ok if finite is None else jnp.minimum(finite, ok)
        return jnp.min(finite) > 0.5

    all_finite = finish(_heads_fixed_max(qts, kv_block, qi, causal, acc_ref, qs_ref))

    @pl.when(jnp.logical_not(all_finite))
    def _():
        finish([_head_online(qts[hh], functools.partial(kv_block, hh), qi, causal, tq)
                for hh in range(heads)])


def _attn(qt, ka, vt, tq, heads):
    b, _, s, _ = ka.shape
    assert heads % 2 == 0 and N_HEADS % heads == 0
    return pl.pallas_call(
        functools.partial(_attn_kernel, tq=tq),
        grid=(b, N_HEADS // heads, s // tq),
        in_specs=[pl.BlockSpec((1, heads, LANES, tq), lambda i, p, j: (i, p, 0, j)),
                  pl.BlockSpec((1, heads, s, LANES), lambda i, p, j: (i, p, 0, 0)),
                  pl.BlockSpec((1, heads, LANES, s), lambda i, p, j: (i, p, 0, 0))],
        out_specs=pl.BlockSpec((1, tq, heads * HEAD_DIM), lambda i, p, j: (i, j, p)),
        out_shape=jax.ShapeDtypeStruct((b, s, ATT_DIM), F32),
        scratch_shapes=[pltpu.VMEM((heads, LANES, tq), F32), pltpu.VMEM((heads, LANES, tq), BF16)],
        compiler_params=pltpu.CompilerParams(
            dimension_semantics=("arbitrary", "arbitrary", "arbitrary"),
            vmem_limit_bytes=VMEM_LIMIT_BYTES),
        name="attn",
    )(qt, ka, vt)


def _attn_sample_kernel(q_ref, kc_ref, vc_ref, kn_ref, vn_ref, o_ref, *, t):
    lane = lax.broadcasted_iota(jnp.int32, (t, LANES), 1)
    causal = (lax.broadcasted_iota(jnp.int32, (t, LANES), 1)
              <= lax.broadcasted_iota(jnp.int32, (t, LANES), 0))
    pad = jnp.zeros((LANES - t, LANES), BF16)
    accs = []
    for hh in range(2):
        q = q_ref[0, hh]
        init = (jnp.full((t, 1), -jnp.inf, F32), jnp.zeros((t, LANES), F32))
        m, acc = _softmax_block(q, kc_ref[0, hh], vc_ref[0, hh], init[0], init[1], None)
        kn = jnp.concatenate([kn_ref[0, hh], pad], axis=0)
        vn = jnp.concatenate([vn_ref[0, hh], pad], axis=0)
        _, acc = _softmax_block(q, kn, vn, m, acc, causal)
        accs.append(acc)
    o_ref[0] = _normalise_pair(accs[0], accs[1], lane)


def _attn_sample(qa, kc, vc, kn, vn):
    b, _, t, _ = qa.shape
    p = kc.shape[2]
    new = pl.BlockSpec((1, 2, t, LANES), lambda i, h: (i, h, 0, 0))
    old = pl.BlockSpec((1, 2, p, LANES), lambda i, h: (i, h, 0, 0))
    return pl.pallas_call(
        functools.partial(_attn_sample_kernel, t=t),
        grid=(b, N_HEADS // 2),
        in_specs=[new, old, old, new, new],
        out_specs=pl.BlockSpec((1, t, LANES), lambda i, h: (i, 0, h)),
        out_shape=jax.ShapeDtypeStruct((b, t, ATT_DIM), F32),
        compiler_params=pltpu.CompilerParams(dimension_semantics=("arbitrary", "arbitrary"),
                                             vmem_limit_bytes=VMEM_LIMIT_BYTES),
        name="attn_sample",
    )(qa, kc, vc, kn, vn)


def _post_kernel(x_ref, att_ref, cn_ref, gt1_ref, sh2_ref, sc2_ref, gt2_ref, shf_ref, scf_ref,
                 gatt_ref, g2_ref, gf_ref, wo_ref, wu_ref, wd_ref, y_ref, *, ff_chunk):
    xa = (_rms(att_ref[0]) * gatt_ref[...]).astype(BF16)
    mixed = (jnp.dot(xa, wo_ref[0:ATT_DIM, :], preferred_element_type=F32)
             + jnp.dot(cn_ref[0], wo_ref[ATT_DIM:, :], preferred_element_type=F32))
    x1 = x_ref[0] + gt1_ref[0] * mixed
    h2 = ((_rms(x1) * g2_ref[...]) * (1.0 + sc2_ref[0]) + sh2_ref[0]).astype(BF16)
    mlp = None
    for c in range(D_FF // ff_chunk):
        up = jnp.dot(h2, wu_ref[:, c * ff_chunk:(c + 1) * ff_chunk], preferred_element_type=F32)
        act = jnp.square(jnp.maximum(up, 0.0)).astype(BF16)
        part = jnp.dot(act, wd_ref[c * ff_chunk:(c + 1) * ff_chunk, :], preferred_element_type=F32)
        mlp = part if mlp is None else mlp + part
    x2 = x1 + gt2_ref[0] * mlp
    y_ref[0] = (_rms(x2) * gf_ref[...]) * (1.0 + scf_ref[0]) + shf_ref[0]


def _post(x, att, cn, mods, g_att, g2, g_final, w_out, w_up, w_down, tile):
    b, s, _ = x.shape
    tok = lambda n: pl.BlockSpec((1, tile, n), lambda i, j: (i, j, 0))
    per_b = pl.BlockSpec((1, 1, D_MODEL), lambda i, j: (i, 0, 0))
    return pl.pallas_call(
        functools.partial(_post_kernel, ff_chunk=1024),
        grid=(b, s // tile),
        in_specs=[tok(D_MODEL), tok(ATT_DIM), tok(CONV_DIM)] + [per_b] * 6
                 + [_const_spec((1, ATT_DIM)), _const_spec((1, D_MODEL)), _const_spec((1, D_MODEL)),
                    _const_spec((D_MODEL, D_MODEL)), _const_spec((D_MODEL, D_FF)),
                    _const_spec((D_FF, D_MODEL))],
        out_specs=tok(D_MODEL),
        out_shape=jax.ShapeDtypeStruct((b, s, D_MODEL), F32),
        compiler_params=pltpu.CompilerParams(dimension_semantics=("arbitrary", "arbitrary"),
                                             vmem_limit_bytes=VMEM_LIMIT_BYTES),
        name="post",
    )(x, att, cn, *mods, g_att, g2, g_final, w_out, w_up, w_down)


def _placement():
    pq = jnp.zeros((LANES, N_HEADS * LANES), F32)
    pq_t = jnp.zeros((ATT_DIM, LANES), F32)
    for h in range(N_HEADS):
        base = h * LANES + (HEAD_DIM if h % 2 == 0 else 0)
        for part in range(N_SPLIT):
            src, own = part * N_HEADS + h, E_F0 + part * N_HEADS + h
            pq = pq.at[src, base + part].set(1.0)
            pq = pq.at[ONES_ROW, base + own].set(1.0)
            pq_t = pq_t.at[h * HEAD_DIM + part, src].set(1.0)
            pq_t = pq_t.at[h * HEAD_DIM + own, ONES_ROW].set(1.0)
    return pq.astype(BF16), pq_t.astype(BF16)


def _tri(n):
    return (jnp.arange(n)[None, :] <= jnp.arange(n)[:, None]).astype(BF16)


def _pack_w_in(w):
    a, c = ATT_DIM, CONV_DIM
    f0 = 3 * a
    b0 = f0 + N_HEADS
    pad = jnp.zeros((D_MODEL, LANES - N_HEADS), w.dtype)
    wq = w[:, :a] * (ATT_SCALE * LOG2E)
    packed = jnp.concatenate([wq, w[:, a:f0], w[:, b0:b0 + 3 * c], w[:, f0:b0], pad], axis=1)
    transposed = jnp.concatenate([wq, w[:, 2 * a:f0], w[:, f0:b0], pad], axis=1).T
    return packed.astype(BF16), transposed.astype(BF16)


def kernel(x_prompt, x_sample, cache_k, cache_v, cache_logf, cache_conv, c_prompt, c_sample, w_ada, b_ada, g_norm1, g_norm2, w_in, b_f, w_conv, g_attn_out, g_conv_out, w_out, w_up, w_down, w_ada_final, b_ada_final, g_final):
    assert w_ada.shape[0] == 1, "single layer"
    nb, s, _ = x_prompt.shape
    db, t, _ = x_sample.shape
    p = cache_k.shape[2]
    tile_prompt, tile_cache, tq, heads_per_step = 512, 512, 512, 8

    c_all = jnp.concatenate([c_prompt, c_sample], axis=0)
    mod = _ada(c_all, w_ada[0], b_ada[0])
    mod_f = _ada(c_all, w_ada_final, b_ada_final)
    sh1, sc1, gt1, sh2, sc2, gt2 = [m[:, None, :] for m in jnp.split(mod, 6, axis=-1)]
    shf, scf = [m[:, None, :] for m in jnp.split(mod_f, 2, axis=-1)]

    w_pack, w_t = _pack_w_in(w_in[0])
    bf_pad = jnp.pad(b_f[0], (0, LANES - N_HEADS)).reshape(1, LANES)
    pq, pq_t = _placement()
    g1 = g_norm1[0].reshape(1, D_MODEL)
    g2 = g_norm2[0].reshape(1, D_MODEL)
    gf = g_final.reshape(1, D_MODEL)
    g_att = g_attn_out[0].reshape(1, ATT_DIM)
    g_conv = g_conv_out[0].reshape(1, CONV_DIM)
    wo, wu, wd = w_out[0].astype(BF16), w_up[0].astype(BF16), w_down[0].astype(BF16)

    def layer(x, rows, prev, f0, tile, transposed_qv, attend):
        sel = lambda m: m[rows]
        k, v, logf, qa, ka, va, cn, cs = _proj(x, sel(sh1), sel(sc1), g1, w_pack, w_t, bf_pad, w_conv[0],
                                               g_conv, prev, f0, _tri(tile), pq, pq_t, tile,
                                               transposed_qv)
        att = attend(qa, ka, va)
        mods = [sel(m) for m in (gt1, sh2, sc2, gt2, shf, scf)]
        y = _post(x, att, cn, mods, g_att, g2, gf, wo, wu, wd, tile)
        bsz, sl = x.shape[0], x.shape[1]
        heads = lambda a: a.reshape(1, bsz, sl, N_HEADS, HEAD_DIM)
        return y, heads(k), heads(v), logf[None], cs[None]

    zeros_prev = jnp.zeros((nb, CONV_WIDTH - 1, CONV_DIM), F32)
    zeros_f = jnp.zeros((nb, 1, LANES), F32)
    yp, kp, vp, lp, cp = layer(x_prompt, slice(0, nb), zeros_prev, zeros_f, tile_prompt, True,
                               lambda qt, ka, vt: _attn(qt, ka, vt, tq, heads_per_step))

    cl_pad = jnp.pad(cache_logf[0], ((0, 0), (0, 0), (0, LANES - N_HEADS)))
    kc, vc, f_tot = _cache_aug(cache_k[0].reshape(db, p, ATT_DIM), cache_v[0].reshape(db, p, ATT_DIM),
                               cl_pad, _tri(tile_cache), tile_cache)
    ys, ks, vs, ls, cs = layer(x_sample, slice(nb, nb + db), cache_conv[0], f_tot, t, False,
                               lambda qa, ka, va: _attn_sample(qa, kc, vc, ka, va))
    return (yp, ys, kp, vp, lp, cp, ks, vs, ls, cs)
```

```python
import functools

import jax
import jax.numpy as jnp
from jax import lax
from jax.experimental import pallas as pl
from jax.experimental.pallas import tpu as pltpu

F32 = jnp.float32
BF16 = jnp.bfloat16

D_MODEL = 1024
N_HEADS = 8
HEAD_DIM = 64
ATT_DIM = N_HEADS * HEAD_DIM
CONV_DIM = 512
CONV_WIDTH = 3
D_FF = 4 * D_MODEL
NORM_EPS = 1e-6
ATT_SCALE = HEAD_DIM ** -0.5

LANES = 128
CARRY_ROWS = 8
N_SPLIT = 3
ONES_ROW = N_SPLIT * N_HEADS
VMEM_LIMIT_BYTES = 56 * 1024 * 1024
BF16_ROWS = 16
SHIFT_ROW = 8
E_F0 = 16
ROW_CHUNKS = 2
LOG2E = 1.4426950408889634

SEC_Q, SEC_K, SEC_V, SEC_BG, SEC_CG, SEC_U = range(6)
SEC_W = 512
F_COL = 6 * SEC_W
PROJ_COLS = F_COL + LANES


def _const_spec(shape):
    zeros = (0,) * len(shape)
    return pl.BlockSpec(shape, lambda *_: zeros, pipeline_mode=pl.Buffered(1))


def _rms(x):
    return x * lax.rsqrt(jnp.mean(x * x, axis=-1, keepdims=True) + NORM_EPS)


def _split3(x):
    hi = x.astype(BF16).astype(F32)
    r = x - hi
    mid = r.astype(BF16).astype(F32)
    lo = (r - mid).astype(BF16).astype(F32)
    return hi, mid, lo


def _cumsum_rows(l128, tri_ref, lane):
    hi, mid, lo = _split3(l128)
    cat = hi + pltpu.roll(mid, N_HEADS, axis=1) + pltpu.roll(lo, 2 * N_HEADS, axis=1)
    sums = jnp.dot(tri_ref[...], cat.astype(BF16), preferred_element_type=F32)
    total = ((sums + pltpu.roll(sums, LANES - N_HEADS, axis=1))
             + pltpu.roll(sums, LANES - 2 * N_HEADS, axis=1))
    return jnp.where(lane < N_HEADS, total, 0.0)


def _f_parts(f128):
    hi, mid, lo = _split3(f128 * LOG2E)
    return hi + pltpu.roll(mid, N_HEADS, axis=1) + pltpu.roll(lo, 2 * N_HEADS, axis=1)


def _fcat(parts, lane):
    return (parts + jnp.where(lane == ONES_ROW, 1.0, 0.0)).astype(BF16)


def _k_extras(parts, lane):
    ones = jnp.where((lane < N_SPLIT) | ((lane >= SHIFT_ROW) & (lane < SHIFT_ROW + N_SPLIT)), 1.0, 0.0)
    odd = ones - pltpu.roll(parts, E_F0, axis=1)
    return pltpu.roll(odd, HEAD_DIM, axis=1), odd


def _store_aug(dst_ref, main, extras_even, extras_odd, lane):
    low = lane < HEAD_DIM
    for pair in range(N_HEADS // 2):
        m = main[:, pair * LANES:(pair + 1) * LANES]
        dst_ref[0, 2 * pair] = jnp.where(low, m, extras_even(pair)).astype(BF16)
        dst_ref[0, 2 * pair + 1] = jnp.where(low, extras_odd(pair), m).astype(BF16)


def _store_v_aug(dst_ref, v, lane):
    even = jnp.where(lane == HEAD_DIM, 1.0, 0.0)
    odd = jnp.where(lane == 0, 1.0, 0.0)
    _store_aug(dst_ref, v, lambda pair: even, lambda pair: odd, lane)


def _ada_kernel(c_ref, w_ref, b_ref, o_ref):
    c = c_ref[...]
    sc = (c * jax.nn.sigmoid(c)).astype(BF16)
    o_ref[...] = jnp.dot(sc, w_ref[...].astype(BF16), preferred_element_type=F32) + b_ref[...]


def _ada(c, w, b):
    rows, n = c.shape[0], w.shape[1]
    bn = 1024
    return pl.pallas_call(
        _ada_kernel,
        grid=(n // bn,),
        in_specs=[pl.BlockSpec((rows, D_MODEL), lambda j: (0, 0)),
                  pl.BlockSpec((D_MODEL, bn), lambda j: (0, j)),
                  pl.BlockSpec((1, bn), lambda j: (0, j))],
        out_specs=pl.BlockSpec((rows, bn), lambda j: (0, j)),
        out_shape=jax.ShapeDtypeStruct((rows, n), F32),
        name="ada",
    )(c, w, b.reshape(1, n))


def _store_qv_transposed(qa_ref, va_ref, q_t, v_t, eq_t, tile):
    ones_row = jnp.where(lax.broadcasted_iota(jnp.int32, (HEAD_DIM, tile), 0) == 0, 1.0, 0.0)
    for h in range(N_HEADS):
        rows = slice(h * HEAD_DIM, (h + 1) * HEAD_DIM)
        parts = [q_t[rows], eq_t[rows]]
        qa_ref[0, h] = jnp.concatenate(parts if h % 2 == 0 else parts[::-1], axis=0).astype(BF16)
        va_ref[0, h] = jnp.concatenate([v_t[rows], ones_row], axis=0).astype(BF16)


def _proj_kernel(x_ref, sh_ref, sc_ref, g1_ref, w_ref, wt_ref, bf_ref, wconv_ref, gconv_ref, prev_ref,
                 f0_ref, tri_ref, pq_ref, pqt_ref,
                 k_ref, v_ref, logf_ref, qa_ref, ka_ref, va_ref, cn_ref, cs_ref,
                 ue_ref, fc_ref, *, tile, transposed_qv):
    s = pl.program_id(1)

    @pl.when(s == 0)
    def _():
        ue_ref[0:CARRY_ROWS, :] = jnp.zeros((CARRY_ROWS, CONV_DIM), F32)
        ue_ref[CARRY_ROWS - (CONV_WIDTH - 1):CARRY_ROWS, :] = prev_ref[0]
        fc_ref[...] = f0_ref[0]

    lane = lax.broadcasted_iota(jnp.int32, (tile, LANES), 1)
    nt = (((1,), (1,)), ((), ()))
    gain = g1_ref[...] * (1.0 + sc_ref[0])
    shift = sh_ref[0]

    chunk = tile // ROW_CHUNKS if tile % (ROW_CHUNKS * BF16_ROWS) == 0 else tile
    starts = list(range(0, tile, chunk))

    def conv_matmuls(r0):
        hb_c = (_rms(x_ref[0, r0:r0 + chunk, :]) * gain + shift).astype(BF16)
        sec = lambda i: jnp.dot(hb_c, w_ref[:, i * SEC_W:(i + 1) * SEC_W], preferred_element_type=F32)
        return hb_c, sec(SEC_CG), sec(SEC_U), sec(SEC_BG)

    def conv_elementwise(r0, cg, u, bg):
        up = cg * u
        base = CARRY_ROWS + r0
        ue_ref[base:base + chunk, :] = up
        cv = (wconv_ref[0:1, :] * ue_ref[base - 2:base - 2 + chunk, :]
              + wconv_ref[1:2, :] * ue_ref[base - 1:base - 1 + chunk, :]
              + wconv_ref[2:3, :] * up)
        cn_ref[0, r0:r0 + chunk, :] = (_rms(bg * cv) * gconv_ref[...]).astype(BF16)

    hbs, pending = [], None
    for r0 in starts:
        hb_c, cg, u, bg = conv_matmuls(r0)
        hbs.append(hb_c)
        if pending is not None:
            conv_elementwise(*pending)
        pending = (r0, cg, u, bg)
    hb = jnp.concatenate(hbs, axis=0) if len(hbs) > 1 else hbs[0]

    def proj(sec, width=SEC_W):
        return jnp.dot(hb, w_ref[:, sec * SEC_W:sec * SEC_W + width], preferred_element_type=F32)

    if transposed_qv:
        fl = lax.dot_general(wt_ref[ATT_DIM:, :], hb, nt, preferred_element_type=F32).T
    else:
        fl = proj(6, LANES)
    k = proj(SEC_K)
    k_ref[0] = k
    conv_elementwise(*pending)
    ue_ref[0:CARRY_ROWS, :] = ue_ref[tile:tile + CARRY_ROWS, :]

    fl = fl + bf_ref[...]
    logf = jnp.minimum(fl, 0.0) - jnp.log1p(jnp.exp(-jnp.abs(fl)))
    logf = jnp.where(lane < N_HEADS, logf, 0.0)
    logf_ref[0] = logf[:, :N_HEADS]
    f128 = _cumsum_rows(logf, tri_ref, lane) + fc_ref[...]
    v = proj(SEC_V)
    v_ref[0] = v
    fc_ref[...] = f128[tile - 1:tile, :]
    parts = _f_parts(f128)
    fcat = _fcat(parts, lane)

    ke_even, ke_odd = _k_extras(parts, lane)
    _store_aug(ka_ref, k, lambda pair: ke_even, lambda pair: ke_odd, lane)
    if transposed_qv:
        eq_t = lax.dot_general(pqt_ref[...], fcat, nt, preferred_element_type=F32)
        q_t = lax.dot_general(wt_ref[:ATT_DIM, :], hb, nt, preferred_element_type=F32)
        _store_qv_transposed(qa_ref, va_ref, q_t, v.T, eq_t, tile)
    else:
        eq = jnp.dot(fcat, pq_ref[...], preferred_element_type=F32)
        _store_aug(qa_ref, proj(SEC_Q), lambda pair: eq[:, 2 * pair * LANES:(2 * pair + 1) * LANES],
                   lambda pair: eq[:, (2 * pair + 1) * LANES:(2 * pair + 2) * LANES], lane)
        _store_v_aug(va_ref, v, lane)

    @pl.when(s == pl.num_programs(1) - 1)
    def _():
        cs_ref[0] = ue_ref[CARRY_ROWS + tile - (CONV_WIDTH - 1):CARRY_ROWS + tile, :]


def _proj(x, sh, sc, g1, w_pack, w_t, bf_pad, w_conv, g_conv, prev, f0, tri, pq, pq_t, tile,
          transposed_qv):
    b, s, _ = x.shape
    assert s % tile == 0 and tile % CARRY_ROWS == 0
    ns = s // tile
    tok = lambda n: pl.BlockSpec((1, tile, n), lambda i, j: (i, j, 0))
    per_b = lambda r, n: pl.BlockSpec((1, r, n), lambda i, j: (i, 0, 0))
    aug = pl.BlockSpec((1, N_HEADS, tile, LANES), lambda i, j: (i, 0, j, 0))
    aug_sds = jax.ShapeDtypeStruct((b, N_HEADS, s, LANES), BF16)
    if transposed_qv:
        qv = pl.BlockSpec((1, N_HEADS, LANES, tile), lambda i, j: (i, 0, 0, j))
        qv_sds = jax.ShapeDtypeStruct((b, N_HEADS, LANES, s), BF16)
    else:
        qv, qv_sds = aug, aug_sds
    out_shape = (
        jax.ShapeDtypeStruct((b, s, ATT_DIM), F32),
        jax.ShapeDtypeStruct((b, s, ATT_DIM), F32),
        jax.ShapeDtypeStruct((b, s, N_HEADS), F32),
        qv_sds,
        aug_sds,
        qv_sds,
        jax.ShapeDtypeStruct((b, s, CONV_DIM), BF16),
        jax.ShapeDtypeStruct((b, CONV_WIDTH - 1, CONV_DIM), F32),
    )
    return pl.pallas_call(
        functools.partial(_proj_kernel, tile=tile, transposed_qv=transposed_qv),
        grid=(b, ns),
        in_specs=[tok(D_MODEL), per_b(1, D_MODEL), per_b(1, D_MODEL), _const_spec((1, D_MODEL)),
                  _const_spec((D_MODEL, PROJ_COLS)), _const_spec((ATT_DIM + LANES, D_MODEL)),
                  _const_spec((1, LANES)),
                  _const_spec((CONV_WIDTH, CONV_DIM)), _const_spec((1, CONV_DIM)),
                  per_b(CONV_WIDTH - 1, CONV_DIM), per_b(1, LANES),
                  _const_spec((tile, tile)), _const_spec((LANES, N_HEADS * LANES)),
                  _const_spec((ATT_DIM, LANES))],
        out_specs=(tok(ATT_DIM), tok(ATT_DIM), tok(N_HEADS), qv, aug, qv, tok(CONV_DIM),
                   per_b(CONV_WIDTH - 1, CONV_DIM)),
        out_shape=out_shape,
        scratch_shapes=[pltpu.VMEM((tile + CARRY_ROWS, CONV_DIM), F32), pltpu.VMEM((1, LANES), F32)],
        compiler_params=pltpu.CompilerParams(dimension_semantics=("arbitrary", "arbitrary"),
                                             vmem_limit_bytes=VMEM_LIMIT_BYTES),
        name="proj",
    )(x, sh, sc, g1, w_pack, w_t, bf_pad, w_conv, g_conv, prev, f0, tri, pq, pq_t)


def _cache_kernel(k_ref, v_ref, l_ref, tri_ref, ka_ref, va_ref, ft_ref, fc_ref, *, tile):
    s = pl.program_id(1)

    @pl.when(s == 0)
    def _():
        fc_ref[...] = jnp.zeros((1, LANES), F32)

    lane = lax.broadcasted_iota(jnp.int32, (tile, LANES), 1)
    f128 = _cumsum_rows(l_ref[0], tri_ref, lane) + fc_ref[...]
    fc_ref[...] = f128[tile - 1:tile, :]
    ke_even, ke_odd = _k_extras(_f_parts(f128), lane)
    _store_aug(ka_ref, k_ref[0], lambda pair: ke_even, lambda pair: ke_odd, lane)
    _store_v_aug(va_ref, v_ref[0], lane)

    @pl.when(s == pl.num_programs(1) - 1)
    def _():
        ft_ref[0] = fc_ref[...]


def _cache_aug(ck, cv, cl_pad, tri, tile):
    b, p, _ = ck.shape
    tok = lambda n: pl.BlockSpec((1, tile, n), lambda i, j: (i, j, 0))
    aug = pl.BlockSpec((1, N_HEADS, tile, LANES), lambda i, j: (i, 0, j, 0))
    return pl.pallas_call(
        functools.partial(_cache_kernel, tile=tile),
        grid=(b, p // tile),
        in_specs=[tok(ATT_DIM), tok(ATT_DIM), tok(LANES), _const_spec((tile, tile))],
        out_specs=(aug, aug, pl.BlockSpec((1, 1, LANES), lambda i, j: (i, 0, 0))),
        out_shape=(jax.ShapeDtypeStruct((b, N_HEADS, p, LANES), BF16),
                   jax.ShapeDtypeStruct((b, N_HEADS, p, LANES), BF16),
                   jax.ShapeDtypeStruct((b, 1, LANES), F32)),
        scratch_shapes=[pltpu.VMEM((1, LANES), F32)],
        compiler_params=pltpu.CompilerParams(dimension_semantics=("arbitrary", "arbitrary"),
                                             vmem_limit_bytes=VMEM_LIMIT_BYTES),
        name="cache_aug",
    )(ck, cv, cl_pad, tri)


def _softmax_block(q, k, v, m, acc, mask):
    s = lax.dot_general(q, k, (((1,), (1,)), ((), ())), preferred_element_type=F32)
    if mask is not None:
        s = jnp.where(mask, s, -jnp.inf)
    m_new = jnp.maximum(m, jnp.max(s, axis=1, keepdims=True))
    p = jnp.exp2(s - m_new)
    acc = acc * jnp.exp2(m - m_new) + jnp.dot(p.astype(BF16), v, preferred_element_type=F32)
    return m_new, acc


def _normalise_pair(acc_even, acc_odd, lane):
    o_even = acc_even * (1.0 / acc_even[:, HEAD_DIM:HEAD_DIM + 1])
    o_odd = acc_odd * (1.0 / acc_odd[:, 0:1])
    return jnp.where(lane < HEAD_DIM, o_even, o_odd)


def _softmax_block_t(qt, k, vt, m, acc, mask):
    s = jnp.dot(k, qt, preferred_element_type=F32)
    if mask is not None:
        s = jnp.where(mask, s, -jnp.inf)
    m_new = jnp.maximum(m, jnp.max(s, axis=0, keepdims=True))
    p = jnp.exp2(s - m_new)
    acc = acc * jnp.exp2(m - m_new) + jnp.dot(vt, p.astype(BF16), preferred_element_type=F32)
    return m_new, acc


def _head_online(qt, kv_block, qi, causal, tq):
    def body(j, carry):
        k, vt = kv_block(j)
        return _softmax_block_t(qt, k, vt, carry[0], carry[1], None)

    init = (jnp.full((1, tq), -jnp.inf, F32), jnp.zeros((LANES, tq), F32))
    m, acc = lax.fori_loop(0, qi, body, init)
    k, vt = kv_block(qi)
    return _softmax_block_t(qt, k, vt, m, acc, causal)[1]


def _with_shift_rows(qt, m, parity):
    g0 = HEAD_DIM if parity == 0 else 0
    hi, mid, lo = _split3(-m)
    row = lax.broadcasted_iota(jnp.int32, (BF16_ROWS, m.shape[1]), 0)
    add = jnp.where(row == SHIFT_ROW, hi,
                    jnp.where(row == SHIFT_ROW + 1, mid, jnp.where(row == SHIFT_ROW + 2, lo, 0.0)))
    grp = (qt[g0:g0 + BF16_ROWS].astype(F32) + add).astype(BF16)
    pieces = ([qt[:g0]] if g0 else []) + [grp, qt[g0 + BF16_ROWS:]]
    return jnp.concatenate(pieces, axis=0)


def _heads_fixed_max(qts, kv_block, qi, causal, acc_ref, qs_ref):
    n = len(qts)
    dot = functools.partial(jnp.dot, preferred_element_type=F32)
    half = causal.shape[0]

    def diag(hh):
        k, qt = kv_block(hh, qi)[0], qts[hh]
        s_a = dot(k[:half], qt)
        s_a = jnp.concatenate([jnp.where(causal, s_a[:, :half], -jnp.inf), s_a[:, half:]], axis=1)
        return s_a, jnp.where(causal, dot(k[half:], qt[:, half:]), -jnp.inf)

    s_next = diag(0)
    for hh in range(n):
        (s_a, s_b), s_next = s_next, (diag(hh + 1) if hh + 1 < n else None)
        m_a = jnp.max(s_a, axis=0, keepdims=True)
        m_late = jnp.maximum(m_a[:, half:], jnp.max(s_b, axis=0, keepdims=True))
        m = jnp.concatenate([m_a[:, :half], m_late], axis=1)
        qs_ref[hh] = _with_shift_rows(qts[hh], m, hh % 2)
        vt = kv_block(hh, qi)[1]
        acc = dot(vt[:, :half], jnp.exp2(s_a - m).astype(BF16))
        late = acc[:, half:] + dot(vt[:, half:], jnp.exp2(s_b - m_late).astype(BF16))
        acc_ref[hh] = jnp.concatenate([acc[:, :half], late], axis=1)

    def body(j, carry):
        scores = lambda hh: dot(kv_block(hh, j)[0], qs_ref[hh])
        s_next = scores(0)
        for hh in range(n):
            s, s_next = s_next, (scores(hh + 1) if hh + 1 < n else None)
            acc_ref[hh] += dot(kv_block(hh, j)[1], jnp.exp2(s).astype(BF16))
        return carry

    lax.fori_loop(0, qi, body, 0)
    return [acc_ref[hh] for hh in range(n)]


def _attn_kernel(qt_ref, k_ref, vt_ref, o_ref, acc_ref, qs_ref, *, tq):
    qi = pl.program_id(2)

    def causal_mask(n):
        return (lax.broadcasted_iota(jnp.int32, (n, n), 0)
                <= lax.broadcasted_iota(jnp.int32, (n, n), 1))

    def kv_block(hh, j):
        off = pl.multiple_of(j * tq, tq)
        return k_ref[0, hh, pl.ds(off, tq), :], vt_ref[0, hh, :, pl.ds(off, tq)]

    heads = qt_ref.shape[1]
    qts = [qt_ref[0, hh] for hh in range(heads)]

    def finish(accs):
        outs = [acc[0:HEAD_DIM] * (1.0 / acc[HEAD_DIM:HEAD_DIM + 1]) for acc in accs]
        finite = None
        for pair in range(heads // 2):
            both = jnp.concatenate([outs[2 * pair], outs[2 * pair + 1]], axis=0)
            o_ref[0, :, pair * LANES:(pair + 1) * LANES] = both.T
            ok = jnp.where(jnp.isfinite(both), 1.0, 0.0)
            finite = ok if finite is None else jnp.minimum(finite, ok)
        return jnp.min(finite) > 0.5

    all_finite = finish(_heads_fixed_max(qts, kv_block, qi, causal_mask(tq // 2), acc_ref, qs_ref))

    @pl.when(jnp.logical_not(all_finite))
    def _():
        finish([_head_online(qts[hh], functools.partial(kv_block, hh), qi, causal_mask(tq), tq)
                for hh in range(heads)])


def _attn(qt, ka, vt, tq, heads):
    b, _, s, _ = ka.shape
    assert heads % 2 == 0 and N_HEADS % heads == 0
    return pl.pallas_call(
        functools.partial(_attn_kernel, tq=tq),
        grid=(b, N_HEADS // heads, s // tq),
        in_specs=[pl.BlockSpec((1, heads, LANES, tq), lambda i, p, j: (i, p, 0, j)),
                  pl.BlockSpec((1, heads, s, LANES), lambda i, p, j: (i, p, 0, 0)),
                  pl.BlockSpec((1, heads, LANES, s), lambda i, p, j: (i, p, 0, 0))],
        out_specs=pl.BlockSpec((1, tq, heads * HEAD_DIM), lambda i, p, j: (i, j, p)),
        out_shape=jax.ShapeDtypeStruct((b, s, ATT_DIM), F32),
        scratch_shapes=[pltpu.VMEM((heads, LANES, tq), F32), pltpu.VMEM((heads, LANES, tq), BF16)],
        compiler_params=pltpu.CompilerParams(
            dimension_semantics=("arbitrary", "arbitrary", "arbitrary"),
            vmem_limit_bytes=VMEM_LIMIT_BYTES),
        name="attn",
    )(qt, ka, vt)


def _attn_sample_kernel(q_ref, kc_ref, vc_ref, kn_ref, vn_ref, o_ref, *, t):
    lane = lax.broadcasted_iota(jnp.int32, (t, LANES), 1)
    causal = (lax.broadcasted_iota(jnp.int32, (t, LANES), 1)
              <= lax.broadcasted_iota(jnp.int32, (t, LANES), 0))
    pad = jnp.zeros((LANES - t, LANES), BF16)
    accs = []
    for hh in range(2):
        q = q_ref[0, hh]
        init = (jnp.full((t, 1), -jnp.inf, F32), jnp.zeros((t, LANES), F32))
        m, acc = _softmax_block(q, kc_ref[0, hh], vc_ref[0, hh], init[0], init[1], None)
        kn = jnp.concatenate([kn_ref[0, hh], pad], axis=0)
        vn = jnp.concatenate([vn_ref[0, hh], pad], axis=0)
        _, acc = _softmax_block(q, kn, vn, m, acc, causal)
        accs.append(acc)
    o_ref[0] = _normalise_pair(accs[0], accs[1], lane)


def _attn_sample(qa, kc, vc, kn, vn):
    b, _, t, _ = qa.shape
    p = kc.shape[2]
    new = pl.BlockSpec((1, 2, t, LANES), lambda i, h: (i, h, 0, 0))
    old = pl.BlockSpec((1, 2, p, LANES), lambda i, h: (i, h, 0, 0))
    return pl.pallas_call(
        functools.partial(_attn_sample_kernel, t=t),
        grid=(b, N_HEADS // 2),
        in_specs=[new, old, old, new, new],
        out_specs=pl.BlockSpec((1, t, LANES), lambda i, h: (i, 0, h)),
        out_shape=jax.ShapeDtypeStruct((b, t, ATT_DIM), F32),
        compiler_params=pltpu.CompilerParams(dimension_semantics=("arbitrary", "arbitrary"),
                                             vmem_limit_bytes=VMEM_LIMIT_BYTES),
        name="attn_sample",
    )(qa, kc, vc, kn, vn)


def _post_kernel(x_ref, att_ref, cn_ref, gt1_ref, sh2_ref, sc2_ref, gt2_ref, shf_ref, scf_ref,
                 gatt_ref, g2_ref, gf_ref, wo_ref, wu_ref, wd_ref, y_ref, *, ff_chunk):
    xa = (_rms(att_ref[0]) * gatt_ref[...]).astype(BF16)
    mixed = (jnp.dot(xa, wo_ref[0:ATT_DIM, :], preferred_element_type=F32)
             + jnp.dot(cn_ref[0], wo_ref[ATT_DIM:, :], preferred_element_type=F32))
    x1 = x_ref[0] + gt1_ref[0] * mixed
    h2 = ((_rms(x1) * g2_ref[...]) * (1.0 + sc2_ref[0]) + sh2_ref[0]).astype(BF16)
    mlp = None
    for c in range(D_FF // ff_chunk):
        up = jnp.dot(h2, wu_ref[:, c * ff_chunk:(c + 1) * ff_chunk], preferred_element_type=F32)
        act = jnp.square(jnp.maximum(up, 0.0)).astype(BF16)
        part = jnp.dot(act, wd_ref[c * ff_chunk:(c + 1) * ff_chunk, :], preferred_element_type=F32)
        mlp = part if mlp is None else mlp + part
    x2 = x1 + gt2_ref[0] * mlp
    y_ref[0] = (_rms(x2) * gf_ref[...]) * (1.0 + scf_ref[0]) + shf_ref[0]


def _post(x, att, cn, mods, g_att, g2, g_final, w_out, w_up, w_down, tile):
    b, s, _ = x.shape
    tok = lambda n: pl.BlockSpec((1, tile, n), lambda i, j: (i, j, 0))
    per_b = pl.BlockSpec((1, 1, D_MODEL), lambda i, j: (i, 0, 0))
    return pl.pallas_call(
        functools.partial(_post_kernel, ff_chunk=1024),
        grid=(b, s // tile),
        in_specs=[tok(D_MODEL), tok(ATT_DIM), tok(CONV_DIM)] + [per_b] * 6
                 + [_const_spec((1, ATT_DIM)), _const_spec((1, D_MODEL)), _const_spec((1, D_MODEL)),
                    _const_spec((D_MODEL, D_MODEL)), _const_spec((D_MODEL, D_FF)),
                    _const_spec((D_FF, D_MODEL))],
        out_specs=tok(D_MODEL),
        out_shape=jax.ShapeDtypeStruct((b, s, D_MODEL), F32),
        compiler_params=pltpu.CompilerParams(dimension_semantics=("arbitrary", "arbitrary"),
                                             vmem_limit_bytes=VMEM_LIMIT_BYTES),
        name="post",
    )(x, att, cn, *mods, g_att, g2, g_final, w_out, w_up, w_down)


def _placement():
    pq = jnp.zeros((LANES, N_HEADS * LANES), F32)
    pq_t = jnp.zeros((ATT_DIM, LANES), F32)
    for h in range(N_HEADS):
        base = h * LANES + (HEAD_DIM if h % 2 == 0 else 0)
        for part in range(N_SPLIT):
            src, own = part * N_HEADS + h, E_F0 + part * N_HEADS + h
            pq = pq.at[src, base + part].set(1.0)
            pq = pq.at[ONES_ROW, base + own].set(1.0)
            pq_t = pq_t.at[h * HEAD_DIM + part, src].set(1.0)
            pq_t = pq_t.at[h * HEAD_DIM + own, ONES_ROW].set(1.0)
    return pq.astype(BF16), pq_t.astype(BF16)


def _tri(n):
    return (jnp.arange(n)[None, :] <= jnp.arange(n)[:, None]).astype(BF16)


def _pack_w_in(w):
    a, c = ATT_DIM, CONV_DIM
    f0 = 3 * a
    b0 = f0 + N_HEADS
    pad = jnp.zeros((D_MODEL, LANES - N_HEADS), w.dtype)
    wq = w[:, :a] * (ATT_SCALE * LOG2E)
    packed = jnp.concatenate([wq, w[:, a:f0], w[:, b0:b0 + 3 * c], w[:, f0:b0], pad], axis=1)
    transposed = jnp.concatenate([wq, w[:, f0:b0], pad], axis=1).T
    return packed.astype(BF16), transposed.astype(BF16)


def kernel(x_prompt, x_sample, cache_k, cache_v, cache_logf, cache_conv, c_prompt, c_sample, w_ada, b_ada, g_norm1, g_norm2, w_in, b_f, w_conv, g_attn_out, g_conv_out, w_out, w_up, w_down, w_ada_final, b_ada_final, g_final):
    assert w_ada.shape[0] == 1, "single layer"
    nb, s, _ = x_prompt.shape
    db, t, _ = x_sample.shape
    p = cache_k.shape[2]
    tile_prompt, tile_cache, tq, heads_per_step = 512, 512, 512, 8

    c_all = jnp.concatenate([c_prompt, c_sample], axis=0)
    mod = _ada(c_all, w_ada[0], b_ada[0])
    mod_f = _ada(c_all, w_ada_final, b_ada_final)
    sh1, sc1, gt1, sh2, sc2, gt2 = [m[:, None, :] for m in jnp.split(mod, 6, axis=-1)]
    shf, scf = [m[:, None, :] for m in jnp.split(mod_f, 2, axis=-1)]

    w_pack, w_t = _pack_w_in(w_in[0])
    bf_pad = jnp.pad(b_f[0], (0, LANES - N_HEADS)).reshape(1, LANES)
    pq, pq_t = _placement()
    g1 = g_norm1[0].reshape(1, D_MODEL)
    g2 = g_norm2[0].reshape(1, D_MODEL)
    gf = g_final.reshape(1, D_MODEL)
    g_att = g_attn_out[0].reshape(1, ATT_DIM)
    g_conv = g_conv_out[0].reshape(1, CONV_DIM)
    wo, wu, wd = w_out[0].astype(BF16), w_up[0].astype(BF16), w_down[0].astype(BF16)

    def layer(x, rows, prev, f0, tile, transposed_qv, attend):
        sel = lambda m: m[rows]
        k, v, logf, qa, ka, va, cn, cs = _proj(x, sel(sh1), sel(sc1), g1, w_pack, w_t, bf_pad, w_conv[0],
                                               g_conv, prev, f0, _tri(tile), pq, pq_t, tile,
                                               transposed_qv)
        att = attend(qa, ka, va)
        mods = [sel(m) for m in (gt1, sh2, sc2, gt2, shf, scf)]
        y = _post(x, att, cn, mods, g_att, g2, gf, wo, wu, wd, tile)
        bsz, sl = x.shape[0], x.shape[1]
        heads = lambda a: a.reshape(1, bsz, sl, N_HEADS, HEAD_DIM)
        return y, heads(k), heads(v), logf[None], cs[None]

    zeros_prev = jnp.zeros((nb, CONV_WIDTH - 1, CONV_DIM), F32)
    zeros_f = jnp.zeros((nb, 1, LANES), F32)
    yp, kp, vp, lp, cp = layer(x_prompt, slice(0, nb), zeros_prev, zeros_f, tile_prompt, True,
                               lambda qt, ka, vt: _attn(qt, ka, vt, tq, heads_per_step))

    cl_pad = jnp.pad(cache_logf[0], ((0, 0), (0, 0), (0, LANES - N_HEADS)))
    kc, vc, f_tot = _cache_aug(cache_k[0].reshape(db, p, ATT_DIM), cache_v[0].reshape(db, p, ATT_DIM),
                               cl_pad, _tri(tile_cache), tile_cache)
    ys, ks, vs, ls, cs = layer(x_sample, slice(nb, nb + db), cache_conv[0], f_tot, t, False,
                               lambda qa, ka, va: _attn_sample(qa, kc, vc, ka, va))
    return (yp, ys, kp, vp, lp, cp, ks, vs, ls, cs)
```

```python
import functools

import jax
import jax.numpy as jnp
from jax import lax
from jax.experimental import pallas as pl
from jax.experimental.pallas import tpu as pltpu

F32 = jnp.float32
BF16 = jnp.bfloat16

D_MODEL = 1024
N_HEADS = 8
HEAD_DIM = 64
ATT_DIM = N_HEADS * HEAD_DIM
CONV_DIM = 512
CONV_WIDTH = 3
D_FF = 4 * D_MODEL
NORM_EPS = 1e-6
ATT_SCALE = HEAD_DIM ** -0.5

LANES = 128
CARRY_ROWS = 8
N_SPLIT = 3
ONES_ROW = N_SPLIT * N_HEADS
VMEM_LIMIT_BYTES = 56 * 1024 * 1024
BF16_ROWS = 16
SHIFT_ROW = 8
E_F0 = 16
ROW_CHUNKS = 2
POST_ROW_CHUNK = 256
LOG2E = 1.4426950408889634

SEC_Q, SEC_K, SEC_V, SEC_BG, SEC_CG, SEC_U = range(6)
SEC_W = 512
F_COL = 6 * SEC_W
PROJ_COLS = F_COL + LANES


def _const_spec(shape):
    zeros = (0,) * len(shape)
    return pl.BlockSpec(shape, lambda *_: zeros, pipeline_mode=pl.Buffered(1))


def _rms(x):
    return x * lax.rsqrt(jnp.mean(x * x, axis=-1, keepdims=True) + NORM_EPS)


def _split3(x):
    hi = x.astype(BF16).astype(F32)
    r = x - hi
    mid = r.astype(BF16).astype(F32)
    lo = (r - mid).astype(BF16).astype(F32)
    return hi, mid, lo


def _cumsum_rows(l128, tri_ref, lane):
    hi, mid, lo = _split3(l128)
    cat = hi + pltpu.roll(mid, N_HEADS, axis=1) + pltpu.roll(lo, 2 * N_HEADS, axis=1)
    sums = jnp.dot(tri_ref[...], cat.astype(BF16), preferred_element_type=F32)
    total = ((sums + pltpu.roll(sums, LANES - N_HEADS, axis=1))
             + pltpu.roll(sums, LANES - 2 * N_HEADS, axis=1))
    return jnp.where(lane < N_HEADS, total, 0.0)


def _f_parts(f128):
    hi, mid, lo = _split3(f128 * LOG2E)
    return hi + pltpu.roll(mid, N_HEADS, axis=1) + pltpu.roll(lo, 2 * N_HEADS, axis=1)


def _fcat(parts, lane):
    return (parts + jnp.where(lane == ONES_ROW, 1.0, 0.0)).astype(BF16)


def _k_extras(parts, lane):
    ones = jnp.where((lane < N_SPLIT) | ((lane >= SHIFT_ROW) & (lane < SHIFT_ROW + N_SPLIT)), 1.0, 0.0)
    odd = ones - pltpu.roll(parts, E_F0, axis=1)
    return pltpu.roll(odd, HEAD_DIM, axis=1), odd


def _store_aug(dst_ref, main, extras_even, extras_odd, lane):
    low = lane < HEAD_DIM
    for pair in range(N_HEADS // 2):
        m = main[:, pair * LANES:(pair + 1) * LANES]
        dst_ref[0, 2 * pair] = jnp.where(low, m, extras_even(pair)).astype(BF16)
        dst_ref[0, 2 * pair + 1] = jnp.where(low, extras_odd(pair), m).astype(BF16)


def _store_v_aug(dst_ref, v, lane):
    even = jnp.where(lane == HEAD_DIM, 1.0, 0.0)
    odd = jnp.where(lane == 0, 1.0, 0.0)
    _store_aug(dst_ref, v, lambda pair: even, lambda pair: odd, lane)


def _ada_kernel(c_ref, w_ref, b_ref, o_ref):
    c = c_ref[...]
    sc = (c * jax.nn.sigmoid(c)).astype(BF16)
    o_ref[...] = jnp.dot(sc, w_ref[...].astype(BF16), preferred_element_type=F32) + b_ref[...]


def _ada(c, w, b):
    rows, n = c.shape[0], w.shape[1]
    bn = 1024
    return pl.pallas_call(
        _ada_kernel,
        grid=(n // bn,),
        in_specs=[pl.BlockSpec((rows, D_MODEL), lambda j: (0, 0)),
                  pl.BlockSpec((D_MODEL, bn), lambda j: (0, j)),
                  pl.BlockSpec((1, bn), lambda j: (0, j))],
        out_specs=pl.BlockSpec((rows, bn), lambda j: (0, j)),
        out_shape=jax.ShapeDtypeStruct((rows, n), F32),
        name="ada",
    )(c, w, b.reshape(1, n))


def _store_qv_transposed(qa_ref, va_ref, q_t, v_t, eq_t, tile):
    ones_row = jnp.where(lax.broadcasted_iota(jnp.int32, (HEAD_DIM, tile), 0) == 0, 1.0, 0.0)
    for h in range(N_HEADS):
        rows = slice(h * HEAD_DIM, (h + 1) * HEAD_DIM)
        parts = [q_t[rows], eq_t[rows]]
        qa_ref[0, h] = jnp.concatenate(parts if h % 2 == 0 else parts[::-1], axis=0).astype(BF16)
        va_ref[0, h] = jnp.concatenate([v_t[rows], ones_row], axis=0).astype(BF16)


def _proj_kernel(x_ref, sh_ref, sc_ref, g1_ref, w_ref, wt_ref, bf_ref, wconv_ref, gconv_ref, prev_ref,
                 f0_ref, tri_ref, pq_ref, pqt_ref,
                 k_ref, v_ref, logf_ref, qa_ref, ka_ref, va_ref, cn_ref, cs_ref,
                 ue_ref, fc_ref, *, tile, transposed_qv):
    s = pl.program_id(1)

    @pl.when(s == 0)
    def _():
        ue_ref[0:CARRY_ROWS, :] = jnp.zeros((CARRY_ROWS, CONV_DIM), F32)
        ue_ref[CARRY_ROWS - (CONV_WIDTH - 1):CARRY_ROWS, :] = prev_ref[0]
        fc_ref[...] = f0_ref[0]

    lane = lax.broadcasted_iota(jnp.int32, (tile, LANES), 1)
    nt = (((1,), (1,)), ((), ()))
    gain = g1_ref[...] * (1.0 + sc_ref[0])
    shift = sh_ref[0]

    chunk = tile // ROW_CHUNKS if tile % (ROW_CHUNKS * BF16_ROWS) == 0 else tile
    starts = list(range(0, tile, chunk))

    def conv_matmuls(r0):
        hb_c = (_rms(x_ref[0, r0:r0 + chunk, :]) * gain + shift).astype(BF16)
        sec = lambda i: jnp.dot(hb_c, w_ref[:, i * SEC_W:(i + 1) * SEC_W], preferred_element_type=F32)
        return hb_c, sec(SEC_CG), sec(SEC_U), sec(SEC_BG)

    def conv_elementwise(r0, cg, u, bg):
        up = cg * u
        base = CARRY_ROWS + r0
        ue_ref[base:base + chunk, :] = up
        cv = (wconv_ref[0:1, :] * ue_ref[base - 2:base - 2 + chunk, :]
              + wconv_ref[1:2, :] * ue_ref[base - 1:base - 1 + chunk, :]
              + wconv_ref[2:3, :] * up)
        cn_ref[0, r0:r0 + chunk, :] = (_rms(bg * cv) * gconv_ref[...]).astype(BF16)

    hbs, pending = [], None
    for r0 in starts:
        hb_c, cg, u, bg = conv_matmuls(r0)
        hbs.append(hb_c)
        if pending is not None:
            conv_elementwise(*pending)
        pending = (r0, cg, u, bg)
    hb = jnp.concatenate(hbs, axis=0) if len(hbs) > 1 else hbs[0]

    def proj(sec, width=SEC_W):
        return jnp.dot(hb, w_ref[:, sec * SEC_W:sec * SEC_W + width], preferred_element_type=F32)

    if transposed_qv:
        fl = lax.dot_general(wt_ref[ATT_DIM:, :], hb, nt, preferred_element_type=F32).T
    else:
        fl = proj(6, LANES)
    k = proj(SEC_K)
    k_ref[0] = k
    conv_elementwise(*pending)
    ue_ref[0:CARRY_ROWS, :] = ue_ref[tile:tile + CARRY_ROWS, :]

    fl = fl + bf_ref[...]
    logf = jnp.minimum(fl, 0.0) - jnp.log1p(jnp.exp(-jnp.abs(fl)))
    logf = jnp.where(lane < N_HEADS, logf, 0.0)
    logf_ref[0] = logf[:, :N_HEADS]
    f128 = _cumsum_rows(logf, tri_ref, lane) + fc_ref[...]
    v = proj(SEC_V)
    v_ref[0] = v
    fc_ref[...] = f128[tile - 1:tile, :]
    parts = _f_parts(f128)
    fcat = _fcat(parts, lane)

    ke_even, ke_odd = _k_extras(parts, lane)
    _store_aug(ka_ref, k, lambda pair: ke_even, lambda pair: ke_odd, lane)
    if transposed_qv:
        eq_t = lax.dot_general(pqt_ref[...], fcat, nt, preferred_element_type=F32)
        q_t = lax.dot_general(wt_ref[:ATT_DIM, :], hb, nt, preferred_element_type=F32)
        _store_qv_transposed(qa_ref, va_ref, q_t, v.T, eq_t, tile)
    else:
        eq = jnp.dot(fcat, pq_ref[...], preferred_element_type=F32)
        _store_aug(qa_ref, proj(SEC_Q), lambda pair: eq[:, 2 * pair * LANES:(2 * pair + 1) * LANES],
                   lambda pair: eq[:, (2 * pair + 1) * LANES:(2 * pair + 2) * LANES], lane)
        _store_v_aug(va_ref, v, lane)

    @pl.when(s == pl.num_programs(1) - 1)
    def _():
        cs_ref[0] = ue_ref[CARRY_ROWS + tile - (CONV_WIDTH - 1):CARRY_ROWS + tile, :]


def _proj(x, sh, sc, g1, w_pack, w_t, bf_pad, w_conv, g_conv, prev, f0, tri, pq, pq_t, tile,
          transposed_qv):
    b, s, _ = x.shape
    assert s % tile == 0 and tile % CARRY_ROWS == 0
    ns = s // tile
    tok = lambda n: pl.BlockSpec((1, tile, n), lambda i, j: (i, j, 0))
    per_b = lambda r, n: pl.BlockSpec((1, r, n), lambda i, j: (i, 0, 0))
    aug = pl.BlockSpec((1, N_HEADS, tile, LANES), lambda i, j: (i, 0, j, 0))
    aug_sds = jax.ShapeDtypeStruct((b, N_HEADS, s, LANES), BF16)
    if transposed_qv:
        qv = pl.BlockSpec((1, N_HEADS, LANES, tile), lambda i, j: (i, 0, 0, j))
        qv_sds = jax.ShapeDtypeStruct((b, N_HEADS, LANES, s), BF16)
    else:
        qv, qv_sds = aug, aug_sds
    out_shape = (
        jax.ShapeDtypeStruct((b, s, ATT_DIM), F32),
        jax.ShapeDtypeStruct((b, s, ATT_DIM), F32),
        jax.ShapeDtypeStruct((b, s, N_HEADS), F32),
        qv_sds,
        aug_sds,
        qv_sds,
        jax.ShapeDtypeStruct((b, s, CONV_DIM), BF16),
        jax.ShapeDtypeStruct((b, CONV_WIDTH - 1, CONV_DIM), F32),
    )
    return pl.pallas_call(
        functools.partial(_proj_kernel, tile=tile, transposed_qv=transposed_qv),
        grid=(b, ns),
        in_specs=[tok(D_MODEL), per_b(1, D_MODEL), per_b(1, D_MODEL), _const_spec((1, D_MODEL)),
                  _const_spec((D_MODEL, PROJ_COLS)), _const_spec((ATT_DIM + LANES, D_MODEL)),
                  _const_spec((1, LANES)),
                  _const_spec((CONV_WIDTH, CONV_DIM)), _const_spec((1, CONV_DIM)),
                  per_b(CONV_WIDTH - 1, CONV_DIM), per_b(1, LANES),
                  _const_spec((tile, tile)), _const_spec((LANES, N_HEADS * LANES)),
                  _const_spec((ATT_DIM, LANES))],
        out_specs=(tok(ATT_DIM), tok(ATT_DIM), tok(N_HEADS), qv, aug, qv, tok(CONV_DIM),
                   per_b(CONV_WIDTH - 1, CONV_DIM)),
        out_shape=out_shape,
        scratch_shapes=[pltpu.VMEM((tile + CARRY_ROWS, CONV_DIM), F32), pltpu.VMEM((1, LANES), F32)],
        compiler_params=pltpu.CompilerParams(dimension_semantics=("arbitrary", "arbitrary"),
                                             vmem_limit_bytes=VMEM_LIMIT_BYTES),
        name="proj",
    )(x, sh, sc, g1, w_pack, w_t, bf_pad, w_conv, g_conv, prev, f0, tri, pq, pq_t)


def _cache_kernel(k_ref, v_ref, l_ref, tri_ref, ka_ref, va_ref, ft_ref, fc_ref, lp_ref, *, tile):
    s = pl.program_id(1)

    @pl.when(s == 0)
    def _():
        fc_ref[...] = jnp.zeros((1, LANES), F32)
        lp_ref[...] = jnp.zeros((tile, LANES), F32)

    lane = lax.broadcasted_iota(jnp.int32, (tile, LANES), 1)
    lp_ref[:, 0:N_HEADS] = l_ref[0]
    f128 = _cumsum_rows(lp_ref[...], tri_ref, lane) + fc_ref[...]
    fc_ref[...] = f128[tile - 1:tile, :]
    ke_even, ke_odd = _k_extras(_f_parts(f128), lane)
    _store_aug(ka_ref, k_ref[0], lambda pair: ke_even, lambda pair: ke_odd, lane)
    _store_v_aug(va_ref, v_ref[0], lane)

    @pl.when(s == pl.num_programs(1) - 1)
    def _():
        ft_ref[0] = fc_ref[...]


def _cache_aug(ck, cv, cl, tri, tile):
    b, p, _ = ck.shape
    tok = lambda n: pl.BlockSpec((1, tile, n), lambda i, j: (i, j, 0))
    aug = pl.BlockSpec((1, N_HEADS, tile, LANES), lambda i, j: (i, 0, j, 0))
    return pl.pallas_call(
        functools.partial(_cache_kernel, tile=tile),
        grid=(b, p // tile),
        in_specs=[tok(ATT_DIM), tok(ATT_DIM), tok(N_HEADS), _const_spec((tile, tile))],
        out_specs=(aug, aug, pl.BlockSpec((1, 1, LANES), lambda i, j: (i, 0, 0))),
        out_shape=(jax.ShapeDtypeStruct((b, N_HEADS, p, LANES), BF16),
                   jax.ShapeDtypeStruct((b, N_HEADS, p, LANES), BF16),
                   jax.ShapeDtypeStruct((b, 1, LANES), F32)),
        scratch_shapes=[pltpu.VMEM((1, LANES), F32), pltpu.VMEM((tile, LANES), F32)],
        compiler_params=pltpu.CompilerParams(dimension_semantics=("arbitrary", "arbitrary"),
                                             vmem_limit_bytes=VMEM_LIMIT_BYTES),
        name="cache_aug",
    )(ck, cv, cl, tri)


def _softmax_block(q, k, v, m, acc, mask):
    s = lax.dot_general(q, k, (((1,), (1,)), ((), ())), preferred_element_type=F32)
    if mask is not None:
        s = jnp.where(mask, s, -jnp.inf)
    m_new = jnp.maximum(m, jnp.max(s, axis=1, keepdims=True))
    p = jnp.exp2(s - m_new)
    acc = acc * jnp.exp2(m - m_new) + jnp.dot(p.astype(BF16), v, preferred_element_type=F32)
    return m_new, acc


def _normalise_pair(acc_even, acc_odd, lane):
    o_even = acc_even * (1.0 / acc_even[:, HEAD_DIM:HEAD_DIM + 1])
    o_odd = acc_odd * (1.0 / acc_odd[:, 0:1])
    return jnp.where(lane < HEAD_DIM, o_even, o_odd)


def _softmax_block_t(qt, k, vt, m, acc, mask):
    s = jnp.dot(k, qt, preferred_element_type=F32)
    if mask is not None:
        s = jnp.where(mask, s, -jnp.inf)
    m_new = jnp.maximum(m, jnp.max(s, axis=0, keepdims=True))
    p = jnp.exp2(s - m_new)
    acc = acc * jnp.exp2(m - m_new) + jnp.dot(vt, p.astype(BF16), preferred_element_type=F32)
    return m_new, acc


def _head_online(qt, kv_block, qi, causal, tq):
    def body(j, carry):
        k, vt = kv_block(j)
        return _softmax_block_t(qt, k, vt, carry[0], carry[1], None)

    init = (jnp.full((1, tq), -jnp.inf, F32), jnp.zeros((LANES, tq), F32))
    m, acc = lax.fori_loop(0, qi, body, init)
    k, vt = kv_block(qi)
    return _softmax_block_t(qt, k, vt, m, acc, causal)[1]


def _with_shift_rows(qt, m, parity):
    g0 = HEAD_DIM if parity == 0 else 0
    hi, mid, lo = _split3(-m)
    row = lax.broadcasted_iota(jnp.int32, (BF16_ROWS, m.shape[1]), 0)
    add = jnp.where(row == SHIFT_ROW, hi,
                    jnp.where(row == SHIFT_ROW + 1, mid, jnp.where(row == SHIFT_ROW + 2, lo, 0.0)))
    grp = (qt[g0:g0 + BF16_ROWS].astype(F32) + add).astype(BF16)
    pieces = ([qt[:g0]] if g0 else []) + [grp, qt[g0 + BF16_ROWS:]]
    return jnp.concatenate(pieces, axis=0)


def _heads_fixed_max(qts, kv_block, qi, causal, acc_ref, qs_ref):
    n = len(qts)
    dot = functools.partial(jnp.dot, preferred_element_type=F32)
    half = causal.shape[0]

    def diag(hh):
        k, qt = kv_block(hh, qi)[0], qts[hh]
        s_a = dot(k[:half], qt)
        s_a = jnp.concatenate([jnp.where(causal, s_a[:, :half], -jnp.inf), s_a[:, half:]], axis=1)
        return s_a, jnp.where(causal, dot(k[half:], qt[:, half:]), -jnp.inf)

    s_next = diag(0)
    for hh in range(n):
        (s_a, s_b), s_next = s_next, (diag(hh + 1) if hh + 1 < n else None)
        m_a = jnp.max(s_a, axis=0, keepdims=True)
        m_late = jnp.maximum(m_a[:, half:], jnp.max(s_b, axis=0, keepdims=True))
        m = jnp.concatenate([m_a[:, :half], m_late], axis=1)
        qs_ref[hh] = _with_shift_rows(qts[hh], m, hh % 2)
        vt = kv_block(hh, qi)[1]
        acc = dot(vt[:, :half], jnp.exp2(s_a - m).astype(BF16))
        late = acc[:, half:] + dot(vt[:, half:], jnp.exp2(s_b - m_late).astype(BF16))
        acc_ref[hh] = jnp.concatenate([acc[:, :half], late], axis=1)

    def body(j, carry):
        scores = lambda hh: dot(kv_block(hh, j)[0], qs_ref[hh])
        s_next = scores(0)
        for hh in range(n):
            s, s_next = s_next, (scores(hh + 1) if hh + 1 < n else None)
            acc_ref[hh] += dot(kv_block(hh, j)[1], jnp.exp2(s).astype(BF16))
        return carry

    lax.fori_loop(0, qi, body, 0)
    return [acc_ref[hh] for hh in range(n)]


def _attn_kernel(qt_ref, k_ref, vt_ref, o_ref, acc_ref, qs_ref, *, tq):
    qi = pl.program_id(2)

    def causal_mask(n):
        return (lax.broadcasted_iota(jnp.int32, (n, n), 0)
                <= lax.broadcasted_iota(jnp.int32, (n, n), 1))

    def kv_block(hh, j):
        off = pl.multiple_of(j * tq, tq)
        return k_ref[0, hh, pl.ds(off, tq), :], vt_ref[0, hh, :, pl.ds(off, tq)]

    heads = qt_ref.shape[1]
    qts = [qt_ref[0, hh] for hh in range(heads)]

    def finish(accs):
        outs = [acc[0:HEAD_DIM] * (1.0 / acc[HEAD_DIM:HEAD_DIM + 1]) for acc in accs]
        finite = None
        for pair in range(heads // 2):
            both = jnp.concatenate([outs[2 * pair], outs[2 * pair + 1]], axis=0)
            o_ref[0, :, pair * LANES:(pair + 1) * LANES] = both.T
            ok = jnp.where(jnp.isfinite(both), 1.0, 0.0)
            finite = ok if finite is None else jnp.minimum(finite, ok)
        return jnp.min(finite) > 0.5

    all_finite = finish(_heads_fixed_max(qts, kv_block, qi, causal_mask(tq // 2), acc_ref, qs_ref))

    @pl.when(jnp.logical_not(all_finite))
    def _():
        finish([_head_online(qts[hh], functools.partial(kv_block, hh), qi, causal_mask(tq), tq)
                for hh in range(heads)])


def _attn(qt, ka, vt, tq, heads):
    b, _, s, _ = ka.shape
    assert heads % 2 == 0 and N_HEADS % heads == 0
    return pl.pallas_call(
        functools.partial(_attn_kernel, tq=tq),
        grid=(b, N_HEADS // heads, s // tq),
        in_specs=[pl.BlockSpec((1, heads, LANES, tq), lambda i, p, j: (i, p, 0, j)),
                  pl.BlockSpec((1, heads, s, LANES), lambda i, p, j: (i, p, 0, 0)),
                  pl.BlockSpec((1, heads, LANES, s), lambda i, p, j: (i, p, 0, 0))],
        out_specs=pl.BlockSpec((1, tq, heads * HEAD_DIM), lambda i, p, j: (i, j, p)),
        out_shape=jax.ShapeDtypeStruct((b, s, ATT_DIM), F32),
        scratch_shapes=[pltpu.VMEM((heads, LANES, tq), F32), pltpu.VMEM((heads, LANES, tq), BF16)],
        compiler_params=pltpu.CompilerParams(
            dimension_semantics=("arbitrary", "arbitrary", "arbitrary"),
            vmem_limit_bytes=VMEM_LIMIT_BYTES),
        name="attn",
    )(qt, ka, vt)


def _attn_sample_kernel(q_ref, kc_ref, vc_ref, kn_ref, vn_ref, o_ref, *, t):
    lane = lax.broadcasted_iota(jnp.int32, (t, LANES), 1)
    causal = (lax.broadcasted_iota(jnp.int32, (t, LANES), 1)
              <= lax.broadcasted_iota(jnp.int32, (t, LANES), 0))
    pad = jnp.zeros((LANES - t, LANES), BF16)
    accs = []
    for hh in range(2):
        q = q_ref[0, hh]
        init = (jnp.full((t, 1), -jnp.inf, F32), jnp.zeros((t, LANES), F32))
        m, acc = _softmax_block(q, kc_ref[0, hh], vc_ref[0, hh], init[0], init[1], None)
        kn = jnp.concatenate([kn_ref[0, hh], pad], axis=0)
        vn = jnp.concatenate([vn_ref[0, hh], pad], axis=0)
        _, acc = _softmax_block(q, kn, vn, m, acc, causal)
        accs.append(acc)
    o_ref[0] = _normalise_pair(accs[0], accs[1], lane)


def _attn_sample(qa, kc, vc, kn, vn):
    b, _, t, _ = qa.shape
    p = kc.shape[2]
    new = pl.BlockSpec((1, 2, t, LANES), lambda i, h: (i, h, 0, 0))
    old = pl.BlockSpec((1, 2, p, LANES), lambda i, h: (i, h, 0, 0))
    return pl.pallas_call(
        functools.partial(_attn_sample_kernel, t=t),
        grid=(b, N_HEADS // 2),
        in_specs=[new, old, old, new, new],
        out_specs=pl.BlockSpec((1, t, LANES), lambda i, h: (i, 0, h)),
        out_shape=jax.ShapeDtypeStruct((b, t, ATT_DIM), F32),
        compiler_params=pltpu.CompilerParams(dimension_semantics=("arbitrary", "arbitrary"),
                                             vmem_limit_bytes=VMEM_LIMIT_BYTES),
        name="attn_sample",
    )(qa, kc, vc, kn, vn)


def _post_kernel(x_ref, att_ref, cn_ref, gt1_ref, sh2_ref, sc2_ref, gt2_ref, shf_ref, scf_ref,
                 gatt_ref, g2_ref, gf_ref, wo_ref, wu_ref, wd_ref, y_ref, *, ff_chunk, row_chunk):
    tile = x_ref.shape[1]
    chunks = [slice(r0, r0 + row_chunk) for r0 in range(0, tile, row_chunk)]
    gain2 = g2_ref[...] * (1.0 + sc2_ref[0])
    gain_f = gf_ref[...] * (1.0 + scf_ref[0])

    def normed_att(rows):
        return (_rms(att_ref[0, rows, :]) * gatt_ref[...]).astype(BF16)

    def out_proj(rows, xa):
        return (jnp.dot(xa, wo_ref[0:ATT_DIM, :], preferred_element_type=F32)
                + jnp.dot(cn_ref[0, rows, :], wo_ref[ATT_DIM:, :], preferred_element_type=F32))

    def residual1(rows, mixed):
        x1 = x_ref[0, rows, :] + gt1_ref[0] * mixed
        return x1, (_rms(x1) * gain2 + sh2_ref[0]).astype(BF16)

    def mlp(h2):
        total = None
        for c in range(D_FF // ff_chunk):
            up = jnp.dot(h2, wu_ref[:, c * ff_chunk:(c + 1) * ff_chunk], preferred_element_type=F32)
            act = jnp.square(jnp.maximum(up, 0.0)).astype(BF16)
            part = jnp.dot(act, wd_ref[c * ff_chunk:(c + 1) * ff_chunk, :], preferred_element_type=F32)
            total = part if total is None else total + part
        return total

    def finish(rows, x1, m):
        x2 = x1 + gt2_ref[0] * m
        y_ref[0, rows, :] = _rms(x2) * gain_f + shf_ref[0]

    n = len(chunks)
    xa = normed_att(chunks[0])
    mixed = []
    for c in range(n):
        mixed.append(out_proj(chunks[c], xa))
        if c + 1 < n:
            xa = normed_att(chunks[c + 1])
    x1, h2 = residual1(chunks[0], mixed[0])
    pending = None
    for c in range(n):
        m = mlp(h2)
        done = (chunks[c], x1, m)
        if c + 1 < n:
            x1, h2 = residual1(chunks[c + 1], mixed[c + 1])
        if pending is not None:
            finish(*pending)
        pending = done
    finish(*pending)


def _post(x, att, cn, mods, g_att, g2, g_final, w_out, w_up, w_down, tile):
    b, s, _ = x.shape
    row_chunk = POST_ROW_CHUNK if tile % POST_ROW_CHUNK == 0 else tile
    tok = lambda n: pl.BlockSpec((1, tile, n), lambda i, j: (i, j, 0))
    per_b = pl.BlockSpec((1, 1, D_MODEL), lambda i, j: (i, 0, 0))
    return pl.pallas_call(
        functools.partial(_post_kernel, ff_chunk=1024, row_chunk=row_chunk),
        grid=(b, s // tile),
        in_specs=[tok(D_MODEL), tok(ATT_DIM), tok(CONV_DIM)] + [per_b] * 6
                 + [_const_spec((1, ATT_DIM)), _const_spec((1, D_MODEL)), _const_spec((1, D_MODEL)),
                    _const_spec((D_MODEL, D_MODEL)), _const_spec((D_MODEL, D_FF)),
                    _const_spec((D_FF, D_MODEL))],
        out_specs=tok(D_MODEL),
        out_shape=jax.ShapeDtypeStruct((b, s, D_MODEL), F32),
        compiler_params=pltpu.CompilerParams(dimension_semantics=("arbitrary", "arbitrary"),
                                             vmem_limit_bytes=VMEM_LIMIT_BYTES),
        name="post",
    )(x, att, cn, *mods, g_att, g2, g_final, w_out, w_up, w_down)


def _placement():
    pq = jnp.zeros((LANES, N_HEADS * LANES), F32)
    pq_t = jnp.zeros((ATT_DIM, LANES), F32)
    for h in range(N_HEADS):
        base = h * LANES + (HEAD_DIM if h % 2 == 0 else 0)
        for part in range(N_SPLIT):
            src, own = part * N_HEADS + h, E_F0 + part * N_HEADS + h
            pq = pq.at[src, base + part].set(1.0)
            pq = pq.at[ONES_ROW, base + own].set(1.0)
            pq_t = pq_t.at[h * HEAD_DIM + part, src].set(1.0)
            pq_t = pq_t.at[h * HEAD_DIM + own, ONES_ROW].set(1.0)
    return pq.astype(BF16), pq_t.astype(BF16)


def _tri(n):
    return (jnp.arange(n)[None, :] <= jnp.arange(n)[:, None]).astype(BF16)


def _pack_w_in(w):
    a, c = ATT_DIM, CONV_DIM
    f0 = 3 * a
    b0 = f0 + N_HEADS
    pad = jnp.zeros((D_MODEL, LANES - N_HEADS), w.dtype)
    wq = w[:, :a] * (ATT_SCALE * LOG2E)
    packed = jnp.concatenate([wq, w[:, a:f0], w[:, b0:b0 + 3 * c], w[:, f0:b0], pad], axis=1)
    transposed = jnp.concatenate([wq, w[:, f0:b0], pad], axis=1).T
    return packed.astype(BF16), transposed.astype(BF16)


def kernel(x_prompt, x_sample, cache_k, cache_v, cache_logf, cache_conv, c_prompt, c_sample, w_ada, b_ada, g_norm1, g_norm2, w_in, b_f, w_conv, g_attn_out, g_conv_out, w_out, w_up, w_down, w_ada_final, b_ada_final, g_final):
    assert w_ada.shape[0] == 1, "single layer"
    nb, s, _ = x_prompt.shape
    db, t, _ = x_sample.shape
    p = cache_k.shape[2]
    tile_prompt, tile_cache, tq, heads_per_step, tile_post = 512, 512, 512, 8, 512

    c_all = jnp.concatenate([c_prompt, c_sample], axis=0)
    mod = _ada(c_all, w_ada[0], b_ada[0])
    mod_f = _ada(c_all, w_ada_final, b_ada_final)
    sh1, sc1, gt1, sh2, sc2, gt2 = [m[:, None, :] for m in jnp.split(mod, 6, axis=-1)]
    shf, scf = [m[:, None, :] for m in jnp.split(mod_f, 2, axis=-1)]

    w_pack, w_t = _pack_w_in(w_in[0])
    bf_pad = jnp.pad(b_f[0], (0, LANES - N_HEADS)).reshape(1, LANES)
    pq, pq_t = _placement()
    g1 = g_norm1[0].reshape(1, D_MODEL)
    g2 = g_norm2[0].reshape(1, D_MODEL)
    gf = g_final.reshape(1, D_MODEL)
    g_att = g_attn_out[0].reshape(1, ATT_DIM)
    g_conv = g_conv_out[0].reshape(1, CONV_DIM)
    wo, wu, wd = w_out[0].astype(BF16), w_up[0].astype(BF16), w_down[0].astype(BF16)

    def layer(x, rows, prev, f0, tile, transposed_qv, attend):
        sel = lambda m: m[rows]
        k, v, logf, qa, ka, va, cn, cs = _proj(x, sel(sh1), sel(sc1), g1, w_pack, w_t, bf_pad, w_conv[0],
                                               g_conv, prev, f0, _tri(tile), pq, pq_t, tile,
                                               transposed_qv)
        att = attend(qa, ka, va)
        mods = [sel(m) for m in (gt1, sh2, sc2, gt2, shf, scf)]
        y = _post(x, att, cn, mods, g_att, g2, gf, wo, wu, wd, min(tile_post, x.shape[1]))
        bsz, sl = x.shape[0], x.shape[1]
        heads = lambda a: a.reshape(1, bsz, sl, N_HEADS, HEAD_DIM)
        return y, heads(k), heads(v), logf[None], cs[None]

    zeros_prev = jnp.zeros((nb, CONV_WIDTH - 1, CONV_DIM), F32)
    zeros_f = jnp.zeros((nb, 1, LANES), F32)
    yp, kp, vp, lp, cp = layer(x_prompt, slice(0, nb), zeros_prev, zeros_f, tile_prompt, True,
                               lambda qt, ka, vt: _attn(qt, ka, vt, tq, heads_per_step))

    kc, vc, f_tot = _cache_aug(cache_k[0].reshape(db, p, ATT_DIM), cache_v[0].reshape(db, p, ATT_DIM),
                               cache_logf[0], _tri(tile_cache), tile_cache)
    ys, ks, vs, ls, cs = layer(x_sample, slice(nb, nb + db), cache_conv[0], f_tot, t, False,
                               lambda qa, ka, va: _attn_sample(qa, kc, vc, ka, va))
    return (yp, ys, kp, vp, lp, cp, ks, vs, ls, cs)
```

```python
import functools

import jax
import jax.numpy as jnp
from jax import lax
from jax.experimental import pallas as pl
from jax.experimental.pallas import tpu as pltpu

F32 = jnp.float32
BF16 = jnp.bfloat16

D_MODEL = 1024
N_HEADS = 8
HEAD_DIM = 64
ATT_DIM = N_HEADS * HEAD_DIM
CONV_DIM = 512
CONV_WIDTH = 3
D_FF = 4 * D_MODEL
NORM_EPS = 1e-6
ATT_SCALE = HEAD_DIM ** -0.5

LANES = 128
CARRY_ROWS = 8
N_SPLIT = 3
ONES_ROW = N_SPLIT * N_HEADS
VMEM_LIMIT_BYTES = 56 * 1024 * 1024
BF16_ROWS = 16
SHIFT_ROW = 8
E_F0 = 16
ROW_CHUNKS = 2
POST_ROW_CHUNK = 256
LOG2E = 1.4426950408889634

SEC_Q, SEC_K, SEC_V, SEC_BG, SEC_CG, SEC_U = range(6)
SEC_W = 512
F_COL = 6 * SEC_W
PROJ_COLS = F_COL + LANES


def _const_spec(shape):
    zeros = (0,) * len(shape)
    return pl.BlockSpec(shape, lambda *_: zeros, pipeline_mode=pl.Buffered(1))


def _rms(x):
    return x * lax.rsqrt(jnp.mean(x * x, axis=-1, keepdims=True) + NORM_EPS)


def _split3(x):
    hi = x.astype(BF16).astype(F32)
    r = x - hi
    mid = r.astype(BF16).astype(F32)
    lo = (r - mid).astype(BF16).astype(F32)
    return hi, mid, lo


def _cumsum_rows(l128, tri_ref, lane):
    hi, mid, lo = _split3(l128)
    cat = hi + pltpu.roll(mid, N_HEADS, axis=1) + pltpu.roll(lo, 2 * N_HEADS, axis=1)
    sums = jnp.dot(tri_ref[...], cat.astype(BF16), preferred_element_type=F32)
    total = ((sums + pltpu.roll(sums, LANES - N_HEADS, axis=1))
             + pltpu.roll(sums, LANES - 2 * N_HEADS, axis=1))
    return jnp.where(lane < N_HEADS, total, 0.0)


def _f_parts(f128):
    hi, mid, lo = _split3(f128 * LOG2E)
    return hi + pltpu.roll(mid, N_HEADS, axis=1) + pltpu.roll(lo, 2 * N_HEADS, axis=1)


def _fcat(parts, lane):
    return (parts + jnp.where(lane == ONES_ROW, 1.0, 0.0)).astype(BF16)


def _k_extras(parts, lane):
    ones = jnp.where((lane < N_SPLIT) | ((lane >= SHIFT_ROW) & (lane < SHIFT_ROW + N_SPLIT)), 1.0, 0.0)
    odd = ones - pltpu.roll(parts, E_F0, axis=1)
    return pltpu.roll(odd, HEAD_DIM, axis=1), odd


def _store_aug(dst_ref, main, extras_even, extras_odd, lane):
    low = lane < HEAD_DIM
    for pair in range(N_HEADS // 2):
        m = main[:, pair * LANES:(pair + 1) * LANES]
        dst_ref[0, 2 * pair] = jnp.where(low, m, extras_even(pair)).astype(BF16)
        dst_ref[0, 2 * pair + 1] = jnp.where(low, extras_odd(pair), m).astype(BF16)


def _store_v_aug(dst_ref, v, lane):
    even = jnp.where(lane == HEAD_DIM, 1.0, 0.0)
    odd = jnp.where(lane == 0, 1.0, 0.0)
    _store_aug(dst_ref, v, lambda pair: even, lambda pair: odd, lane)


def _ada_kernel(c_ref, w_ref, b_ref, o_ref):
    c = c_ref[...]
    sc = (c * jax.nn.sigmoid(c)).astype(BF16)
    o_ref[...] = jnp.dot(sc, w_ref[...].astype(BF16), preferred_element_type=F32) + b_ref[...]


def _ada(c, w, b):
    rows, n = c.shape[0], w.shape[1]
    bn = 1024
    return pl.pallas_call(
        _ada_kernel,
        grid=(n // bn,),
        in_specs=[pl.BlockSpec((rows, D_MODEL), lambda j: (0, 0)),
                  pl.BlockSpec((D_MODEL, bn), lambda j: (0, j)),
                  pl.BlockSpec((1, bn), lambda j: (0, j))],
        out_specs=pl.BlockSpec((rows, bn), lambda j: (0, j)),
        out_shape=jax.ShapeDtypeStruct((rows, n), F32),
        name="ada",
    )(c, w, b.reshape(1, n))


def _store_qv_transposed(qa_ref, va_ref, q_t, v_t, eq_t, tile):
    ones_row = jnp.where(lax.broadcasted_iota(jnp.int32, (HEAD_DIM, tile), 0) == 0, 1.0, 0.0)
    for h in range(N_HEADS):
        rows = slice(h * HEAD_DIM, (h + 1) * HEAD_DIM)
        parts = [q_t[rows], eq_t[rows]]
        qa_ref[0, h] = jnp.concatenate(parts if h % 2 == 0 else parts[::-1], axis=0).astype(BF16)
        va_ref[0, h] = jnp.concatenate([v_t[rows], ones_row], axis=0).astype(BF16)


def _proj_kernel(x_ref, sh_ref, sc_ref, g1_ref, w_ref, wt_ref, bf_ref, wconv_ref, gconv_ref, prev_ref,
                 f0_ref, tri_ref, pq_ref, pqt_ref,
                 k_ref, v_ref, logf_ref, qa_ref, ka_ref, va_ref, cn_ref, cs_ref,
                 ue_ref, fc_ref, *, tile, transposed_qv):
    s = pl.program_id(1)

    @pl.when(s == 0)
    def _():
        ue_ref[0:CARRY_ROWS, :] = jnp.zeros((CARRY_ROWS, CONV_DIM), F32)
        ue_ref[CARRY_ROWS - (CONV_WIDTH - 1):CARRY_ROWS, :] = prev_ref[0]
        fc_ref[...] = f0_ref[0]

    lane = lax.broadcasted_iota(jnp.int32, (tile, LANES), 1)
    nt = (((1,), (1,)), ((), ()))
    gain = g1_ref[...] * (1.0 + sc_ref[0])
    shift = sh_ref[0]

    chunk = tile // ROW_CHUNKS if tile % (ROW_CHUNKS * BF16_ROWS) == 0 else tile
    starts = list(range(0, tile, chunk))

    def conv_matmuls(r0):
        hb_c = (_rms(x_ref[0, r0:r0 + chunk, :]) * gain + shift).astype(BF16)
        sec = lambda i: jnp.dot(hb_c, w_ref[:, i * SEC_W:(i + 1) * SEC_W], preferred_element_type=F32)
        return hb_c, sec(SEC_CG), sec(SEC_U), sec(SEC_BG)

    def conv_elementwise(r0, cg, u, bg):
        up = cg * u
        base = CARRY_ROWS + r0
        ue_ref[base:base + chunk, :] = up
        cv = (wconv_ref[0:1, :] * ue_ref[base - 2:base - 2 + chunk, :]
              + wconv_ref[1:2, :] * ue_ref[base - 1:base - 1 + chunk, :]
              + wconv_ref[2:3, :] * up)
        cn_ref[0, r0:r0 + chunk, :] = (_rms(bg * cv) * gconv_ref[...]).astype(BF16)

    hbs, pending = [], None
    for r0 in starts:
        hb_c, cg, u, bg = conv_matmuls(r0)
        hbs.append(hb_c)
        if pending is not None:
            conv_elementwise(*pending)
        pending = (r0, cg, u, bg)
    hb = jnp.concatenate(hbs, axis=0) if len(hbs) > 1 else hbs[0]

    def proj(sec, width=SEC_W):
        return jnp.dot(hb, w_ref[:, sec * SEC_W:sec * SEC_W + width], preferred_element_type=F32)

    if transposed_qv:
        fl = lax.dot_general(wt_ref[ATT_DIM:, :], hb, nt, preferred_element_type=F32).T
    else:
        fl = proj(6, LANES)
    k = proj(SEC_K)
    k_ref[0] = k
    conv_elementwise(*pending)
    ue_ref[0:CARRY_ROWS, :] = ue_ref[tile:tile + CARRY_ROWS, :]

    fl = fl + bf_ref[...]
    logf = jnp.minimum(fl, 0.0) - jnp.log1p(jnp.exp(-jnp.abs(fl)))
    logf = jnp.where(lane < N_HEADS, logf, 0.0)
    logf_ref[0] = logf[:, :N_HEADS]
    f128 = _cumsum_rows(logf, tri_ref, lane) + fc_ref[...]
    v = proj(SEC_V)
    v_ref[0] = v
    fc_ref[...] = f128[tile - 1:tile, :]
    parts = _f_parts(f128)
    fcat = _fcat(parts, lane)

    ke_even, ke_odd = _k_extras(parts, lane)
    _store_aug(ka_ref, k, lambda pair: ke_even, lambda pair: ke_odd, lane)
    if transposed_qv:
        eq_t = lax.dot_general(pqt_ref[...], fcat, nt, preferred_element_type=F32)
        q_t = lax.dot_general(wt_ref[:ATT_DIM, :], hb, nt, preferred_element_type=F32)
        _store_qv_transposed(qa_ref, va_ref, q_t, v.T, eq_t, tile)
    else:
        eq = jnp.dot(fcat, pq_ref[...], preferred_element_type=F32)
        _store_aug(qa_ref, proj(SEC_Q), lambda pair: eq[:, 2 * pair * LANES:(2 * pair + 1) * LANES],
                   lambda pair: eq[:, (2 * pair + 1) * LANES:(2 * pair + 2) * LANES], lane)
        _store_v_aug(va_ref, v, lane)

    @pl.when(s == pl.num_programs(1) - 1)
    def _():
        cs_ref[0] = ue_ref[CARRY_ROWS + tile - (CONV_WIDTH - 1):CARRY_ROWS + tile, :]


def _proj(x, sh, sc, g1, w_pack, w_t, bf_pad, w_conv, g_conv, prev, f0, tri, pq, pq_t, tile,
          transposed_qv):
    b, s, _ = x.shape
    assert s % tile == 0 and tile % CARRY_ROWS == 0
    ns = s // tile
    tok = lambda n: pl.BlockSpec((1, tile, n), lambda i, j: (i, j, 0))
    per_b = lambda r, n: pl.BlockSpec((1, r, n), lambda i, j: (i, 0, 0))
    aug = pl.BlockSpec((1, N_HEADS, tile, LANES), lambda i, j: (i, 0, j, 0))
    aug_sds = jax.ShapeDtypeStruct((b, N_HEADS, s, LANES), BF16)
    if transposed_qv:
        qv = pl.BlockSpec((1, N_HEADS, LANES, tile), lambda i, j: (i, 0, 0, j))
        qv_sds = jax.ShapeDtypeStruct((b, N_HEADS, LANES, s), BF16)
    else:
        qv, qv_sds = aug, aug_sds
    out_shape = (
        jax.ShapeDtypeStruct((b, s, ATT_DIM), F32),
        jax.ShapeDtypeStruct((b, s, ATT_DIM), F32),
        jax.ShapeDtypeStruct((b, s, N_HEADS), F32),
        qv_sds,
        aug_sds,
        qv_sds,
        jax.ShapeDtypeStruct((b, s, CONV_DIM), BF16),
        jax.ShapeDtypeStruct((b, CONV_WIDTH - 1, CONV_DIM), F32),
    )
    return pl.pallas_call(
        functools.partial(_proj_kernel, tile=tile, transposed_qv=transposed_qv),
        grid=(b, ns),
        in_specs=[tok(D_MODEL), per_b(1, D_MODEL), per_b(1, D_MODEL), _const_spec((1, D_MODEL)),
                  _const_spec((D_MODEL, PROJ_COLS)), _const_spec((ATT_DIM + LANES, D_MODEL)),
                  _const_spec((1, LANES)),
                  _const_spec((CONV_WIDTH, CONV_DIM)), _const_spec((1, CONV_DIM)),
                  per_b(CONV_WIDTH - 1, CONV_DIM), per_b(1, LANES),
                  _const_spec((tile, tile)), _const_spec((LANES, N_HEADS * LANES)),
                  _const_spec((ATT_DIM, LANES))],
        out_specs=(tok(ATT_DIM), tok(ATT_DIM), tok(N_HEADS), qv, aug, qv, tok(CONV_DIM),
                   per_b(CONV_WIDTH - 1, CONV_DIM)),
        out_shape=out_shape,
        scratch_shapes=[pltpu.VMEM((tile + CARRY_ROWS, CONV_DIM), F32), pltpu.VMEM((1, LANES), F32)],
        compiler_params=pltpu.CompilerParams(dimension_semantics=("arbitrary", "arbitrary"),
                                             vmem_limit_bytes=VMEM_LIMIT_BYTES),
        name="proj",
    )(x, sh, sc, g1, w_pack, w_t, bf_pad, w_conv, g_conv, prev, f0, tri, pq, pq_t)


def _cache_cumsum_kernel(l_ref, triu_ref, f_ref, fc_ref, *, tile):
    s = pl.program_id(1)

    @pl.when(s == 0)
    def _():
        fc_ref[...] = jnp.zeros((N_HEADS, 1), F32)

    hi, mid, lo = _split3(l_ref[0])
    stacked = jnp.concatenate([hi, mid, lo, jnp.zeros_like(lo)], axis=0).astype(BF16)
    sums = jnp.dot(stacked, triu_ref[...], preferred_element_type=F32)
    f = ((sums[0:N_HEADS] + sums[N_HEADS:2 * N_HEADS]) + sums[2 * N_HEADS:3 * N_HEADS]) + fc_ref[...]
    f_ref[0] = f
    fc_ref[...] = f[:, tile - 1:tile]


def _cache_cumsum(cl_t, triu, tile):
    b, h, p = cl_t.shape
    blk = pl.BlockSpec((1, h, tile), lambda i, j: (i, 0, j))
    return pl.pallas_call(
        functools.partial(_cache_cumsum_kernel, tile=tile),
        grid=(b, p // tile),
        in_specs=[blk, _const_spec((tile, tile))],
        out_specs=blk,
        out_shape=jax.ShapeDtypeStruct((b, h, p), F32),
        scratch_shapes=[pltpu.VMEM((h, 1), F32)],
        compiler_params=pltpu.CompilerParams(dimension_semantics=("arbitrary", "arbitrary")),
        name="cache_cumsum",
    )(cl_t, triu)


def _softmax_block_t(qt, k, vt, m, acc, mask):
    s = jnp.dot(k, qt, preferred_element_type=F32)
    if mask is not None:
        s = jnp.where(mask, s, -jnp.inf)
    m_new = jnp.maximum(m, jnp.max(s, axis=0, keepdims=True))
    p = jnp.exp2(s - m_new)
    acc = acc * jnp.exp2(m - m_new) + jnp.dot(vt, p.astype(BF16), preferred_element_type=F32)
    return m_new, acc


def _head_online(qt, kv_block, qi, causal, tq):
    def body(j, carry):
        k, vt = kv_block(j)
        return _softmax_block_t(qt, k, vt, carry[0], carry[1], None)

    init = (jnp.full((1, tq), -jnp.inf, F32), jnp.zeros((LANES, tq), F32))
    m, acc = lax.fori_loop(0, qi, body, init)
    k, vt = kv_block(qi)
    return _softmax_block_t(qt, k, vt, m, acc, causal)[1]


def _with_shift_rows(qt, m, parity):
    g0 = HEAD_DIM if parity == 0 else 0
    hi, mid, lo = _split3(-m)
    row = lax.broadcasted_iota(jnp.int32, (BF16_ROWS, m.shape[1]), 0)
    add = jnp.where(row == SHIFT_ROW, hi,
                    jnp.where(row == SHIFT_ROW + 1, mid, jnp.where(row == SHIFT_ROW + 2, lo, 0.0)))
    grp = (qt[g0:g0 + BF16_ROWS].astype(F32) + add).astype(BF16)
    pieces = ([qt[:g0]] if g0 else []) + [grp, qt[g0 + BF16_ROWS:]]
    return jnp.concatenate(pieces, axis=0)


def _heads_fixed_max(qts, kv_block, qi, causal, acc_ref, qs_ref):
    n = len(qts)
    dot = functools.partial(jnp.dot, preferred_element_type=F32)
    half = causal.shape[0]

    def diag(hh):
        k, qt = kv_block(hh, qi)[0], qts[hh]
        s_a = dot(k[:half], qt)
        s_a = jnp.concatenate([jnp.where(causal, s_a[:, :half], -jnp.inf), s_a[:, half:]], axis=1)
        return s_a, jnp.where(causal, dot(k[half:], qt[:, half:]), -jnp.inf)

    s_next = diag(0)
    for hh in range(n):
        (s_a, s_b), s_next = s_next, (diag(hh + 1) if hh + 1 < n else None)
        m_a = jnp.max(s_a, axis=0, keepdims=True)
        m_late = jnp.maximum(m_a[:, half:], jnp.max(s_b, axis=0, keepdims=True))
        m = jnp.concatenate([m_a[:, :half], m_late], axis=1)
        qs_ref[hh] = _with_shift_rows(qts[hh], m, hh % 2)
        vt = kv_block(hh, qi)[1]
        acc = dot(vt[:, :half], jnp.exp2(s_a - m).astype(BF16))
        late = acc[:, half:] + dot(vt[:, half:], jnp.exp2(s_b - m_late).astype(BF16))
        acc_ref[hh] = jnp.concatenate([acc[:, :half], late], axis=1)

    def body(j, carry):
        scores = lambda hh: dot(kv_block(hh, j)[0], qs_ref[hh])
        s_next = scores(0)
        for hh in range(n):
            s, s_next = s_next, (scores(hh + 1) if hh + 1 < n else None)
            acc_ref[hh] += dot(kv_block(hh, j)[1], jnp.exp2(s).astype(BF16))
        return carry

    lax.fori_loop(0, qi, body, 0)
    return [acc_ref[hh] for hh in range(n)]


def _attn_kernel(qt_ref, k_ref, vt_ref, o_ref, acc_ref, qs_ref, *, tq):
    qi = pl.program_id(2)

    def causal_mask(n):
        return (lax.broadcasted_iota(jnp.int32, (n, n), 0)
                <= lax.broadcasted_iota(jnp.int32, (n, n), 1))

    def kv_block(hh, j):
        off = pl.multiple_of(j * tq, tq)
        return k_ref[0, hh, pl.ds(off, tq), :], vt_ref[0, hh, :, pl.ds(off, tq)]

    heads = qt_ref.shape[1]
    qts = [qt_ref[0, hh] for hh in range(heads)]

    def finish(accs):
        outs = [acc[0:HEAD_DIM] * (1.0 / acc[HEAD_DIM:HEAD_DIM + 1]) for acc in accs]
        finite = None
        for pair in range(heads // 2):
            both = jnp.concatenate([outs[2 * pair], outs[2 * pair + 1]], axis=0)
            o_ref[0, :, pair * LANES:(pair + 1) * LANES] = both.T
            ok = jnp.where(jnp.isfinite(both), 1.0, 0.0)
            finite = ok if finite is None else jnp.minimum(finite, ok)
        return jnp.min(finite) > 0.5

    all_finite = finish(_heads_fixed_max(qts, kv_block, qi, causal_mask(tq // 2), acc_ref, qs_ref))

    @pl.when(jnp.logical_not(all_finite))
    def _():
        finish([_head_online(qts[hh], functools.partial(kv_block, hh), qi, causal_mask(tq), tq)
                for hh in range(heads)])


def _attn(qt, ka, vt, tq, heads):
    b, _, s, _ = ka.shape
    assert heads % 2 == 0 and N_HEADS % heads == 0
    return pl.pallas_call(
        functools.partial(_attn_kernel, tq=tq),
        grid=(b, N_HEADS // heads, s // tq),
        in_specs=[pl.BlockSpec((1, heads, LANES, tq), lambda i, p, j: (i, p, 0, j)),
                  pl.BlockSpec((1, heads, s, LANES), lambda i, p, j: (i, p, 0, 0)),
                  pl.BlockSpec((1, heads, LANES, s), lambda i, p, j: (i, p, 0, 0))],
        out_specs=pl.BlockSpec((1, tq, heads * HEAD_DIM), lambda i, p, j: (i, j, p)),
        out_shape=jax.ShapeDtypeStruct((b, s, ATT_DIM), F32),
        scratch_shapes=[pltpu.VMEM((heads, LANES, tq), F32), pltpu.VMEM((heads, LANES, tq), BF16)],
        compiler_params=pltpu.CompilerParams(
            dimension_semantics=("arbitrary", "arbitrary", "arbitrary"),
            vmem_limit_bytes=VMEM_LIMIT_BYTES),
        name="attn",
    )(qt, ka, vt)


def _attn_sample_kernel(q_ref, kn_ref, vn_ref, kt_ref, vt_ref, fc_ref, o_ref, *, t):
    nt = (((1,), (1,)), ((), ()))
    causal = (lax.broadcasted_iota(jnp.int32, (t, LANES), 1)
              <= lax.broadcasted_iota(jnp.int32, (t, LANES), 0))
    pad = jnp.zeros((LANES - t, LANES), BF16)
    fk = fc_ref[0] * LOG2E
    outs = []
    for h in range(N_HEADS):
        head, extras = (0, HEAD_DIM) if h % 2 == 0 else (HEAD_DIM, 0)
        tile = q_ref[0, h]
        ex = tile[:, extras:extras + N_SPLIT].astype(F32)
        fq = (ex[:, 0:1] + ex[:, 1:2]) + ex[:, 2:3]
        s_old = jnp.dot(tile[:, head:head + HEAD_DIM], kt_ref[0, h].astype(BF16),
                        preferred_element_type=F32) + (fq - fk[h:h + 1, :])
        kn = jnp.concatenate([kn_ref[0, h], pad], axis=0)
        vn = jnp.concatenate([vn_ref[0, h], pad], axis=0)
        s_new = jnp.where(causal, lax.dot_general(tile, kn, nt, preferred_element_type=F32), -jnp.inf)
        m = jnp.maximum(jnp.max(s_old, axis=1, keepdims=True), jnp.max(s_new, axis=1, keepdims=True))
        p_old = jnp.exp2(s_old - m)
        acc_new = jnp.dot(jnp.exp2(s_new - m).astype(BF16), vn, preferred_element_type=F32)
        o_old = lax.dot_general(p_old.astype(BF16), vt_ref[0, h].astype(BF16), nt,
                                preferred_element_type=F32)
        denom = jnp.sum(p_old, axis=1, keepdims=True) + acc_new[:, extras:extras + 1]
        outs.append((o_old + acc_new[:, head:head + HEAD_DIM]) * (1.0 / denom))
    o_ref[0] = jnp.concatenate(outs, axis=1)


def _attn_sample(qa, kn, vn, kt, vt, fc):
    b, _, t, _ = qa.shape
    p = kt.shape[3]
    new = pl.BlockSpec((1, N_HEADS, t, LANES), lambda i: (i, 0, 0, 0))
    old = pl.BlockSpec((1, N_HEADS, HEAD_DIM, p), lambda i: (i, 0, 0, 0))
    return pl.pallas_call(
        functools.partial(_attn_sample_kernel, t=t),
        grid=(b,),
        in_specs=[new, new, new, old, old, pl.BlockSpec((1, N_HEADS, p), lambda i: (i, 0, 0))],
        out_specs=pl.BlockSpec((1, t, ATT_DIM), lambda i: (i, 0, 0)),
        out_shape=jax.ShapeDtypeStruct((b, t, ATT_DIM), F32),
        compiler_params=pltpu.CompilerParams(dimension_semantics=("arbitrary",),
                                             vmem_limit_bytes=VMEM_LIMIT_BYTES),
        name="attn_sample",
    )(qa, kn, vn, kt, vt, fc)


def _post_kernel(x_ref, att_ref, cn_ref, gt1_ref, sh2_ref, sc2_ref, gt2_ref, shf_ref, scf_ref,
                 gatt_ref, g2_ref, gf_ref, wo_ref, wu_ref, wd_ref, y_ref, *, ff_chunk, row_chunk):
    tile = x_ref.shape[1]
    chunks = [slice(r0, r0 + row_chunk) for r0 in range(0, tile, row_chunk)]
    gain2 = g2_ref[...] * (1.0 + sc2_ref[0])
    gain_f = gf_ref[...] * (1.0 + scf_ref[0])

    def normed_att(rows):
        return (_rms(att_ref[0, rows, :]) * gatt_ref[...]).astype(BF16)

    def out_proj(rows, xa):
        return (jnp.dot(xa, wo_ref[0:ATT_DIM, :], preferred_element_type=F32)
                + jnp.dot(cn_ref[0, rows, :], wo_ref[ATT_DIM:, :], preferred_element_type=F32))

    def residual1(rows, mixed):
        x1 = x_ref[0, rows, :] + gt1_ref[0] * mixed
        return x1, (_rms(x1) * gain2 + sh2_ref[0]).astype(BF16)

    def mlp(h2):
        total = None
        for c in range(D_FF // ff_chunk):
            up = jnp.dot(h2, wu_ref[:, c * ff_chunk:(c + 1) * ff_chunk], preferred_element_type=F32)
            act = jnp.square(jnp.maximum(up, 0.0)).astype(BF16)
            part = jnp.dot(act, wd_ref[c * ff_chunk:(c + 1) * ff_chunk, :], preferred_element_type=F32)
            total = part if total is None else total + part
        return total

    def finish(rows, x1, m):
        x2 = x1 + gt2_ref[0] * m
        y_ref[0, rows, :] = _rms(x2) * gain_f + shf_ref[0]

    n = len(chunks)
    xa = normed_att(chunks[0])
    mixed = []
    for c in range(n):
        mixed.append(out_proj(chunks[c], xa))
        if c + 1 < n:
            xa = normed_att(chunks[c + 1])
    x1, h2 = residual1(chunks[0], mixed[0])
    pending = None
    for c in range(n):
        m = mlp(h2)
        done = (chunks[c], x1, m)
        if c + 1 < n:
            x1, h2 = residual1(chunks[c + 1], mixed[c + 1])
        if pending is not None:
            finish(*pending)
        pending = done
    finish(*pending)


def _post(x, att, cn, mods, g_att, g2, g_final, w_out, w_up, w_down, tile):
    b, s, _ = x.shape
    row_chunk = POST_ROW_CHUNK if tile % POST_ROW_CHUNK == 0 else tile
    tok = lambda n: pl.BlockSpec((1, tile, n), lambda i, j: (i, j, 0))
    per_b = pl.BlockSpec((1, 1, D_MODEL), lambda i, j: (i, 0, 0))
    return pl.pallas_call(
        functools.partial(_post_kernel, ff_chunk=1024, row_chunk=row_chunk),
        grid=(b, s // tile),
        in_specs=[tok(D_MODEL), tok(ATT_DIM), tok(CONV_DIM)] + [per_b] * 6
                 + [_const_spec((1, ATT_DIM)), _const_spec((1, D_MODEL)), _const_spec((1, D_MODEL)),
                    _const_spec((D_MODEL, D_MODEL)), _const_spec((D_MODEL, D_FF)),
                    _const_spec((D_FF, D_MODEL))],
        out_specs=tok(D_MODEL),
        out_shape=jax.ShapeDtypeStruct((b, s, D_MODEL), F32),
        compiler_params=pltpu.CompilerParams(dimension_semantics=("arbitrary", "arbitrary"),
                                             vmem_limit_bytes=VMEM_LIMIT_BYTES),
        name="post",
    )(x, att, cn, *mods, g_att, g2, g_final, w_out, w_up, w_down)


def _placement():
    pq = jnp.zeros((LANES, N_HEADS * LANES), F32)
    pq_t = jnp.zeros((ATT_DIM, LANES), F32)
    for h in range(N_HEADS):
        base = h * LANES + (HEAD_DIM if h % 2 == 0 else 0)
        for part in range(N_SPLIT):
            src, own = part * N_HEADS + h, E_F0 + part * N_HEADS + h
            pq = pq.at[src, base + part].set(1.0)
            pq = pq.at[ONES_ROW, base + own].set(1.0)
            pq_t = pq_t.at[h * HEAD_DIM + part, src].set(1.0)
            pq_t = pq_t.at[h * HEAD_DIM + own, ONES_ROW].set(1.0)
    return pq.astype(BF16), pq_t.astype(BF16)


def _tri(n):
    return (jnp.arange(n)[None, :] <= jnp.arange(n)[:, None]).astype(BF16)


def _pack_w_in(w):
    a, c = ATT_DIM, CONV_DIM
    f0 = 3 * a
    b0 = f0 + N_HEADS
    pad = jnp.zeros((D_MODEL, LANES - N_HEADS), w.dtype)
    wq = w[:, :a] * (ATT_SCALE * LOG2E)
    packed = jnp.concatenate([wq, w[:, a:f0], w[:, b0:b0 + 3 * c], w[:, f0:b0], pad], axis=1)
    transposed = jnp.concatenate([wq, w[:, f0:b0], pad], axis=1).T
    return packed.astype(BF16), transposed.astype(BF16)


def kernel(x_prompt, x_sample, cache_k, cache_v, cache_logf, cache_conv, c_prompt, c_sample, w_ada, b_ada, g_norm1, g_norm2, w_in, b_f, w_conv, g_attn_out, g_conv_out, w_out, w_up, w_down, w_ada_final, b_ada_final, g_final):
    assert w_ada.shape[0] == 1, "single layer"
    nb, s, _ = x_prompt.shape
    db, t, _ = x_sample.shape
    p = cache_k.shape[2]
    tile_prompt, tile_cache, tq, heads_per_step, tile_post = 512, 512, 512, 8, 512

    c_all = jnp.concatenate([c_prompt, c_sample], axis=0)
    mod = _ada(c_all, w_ada[0], b_ada[0])
    mod_f = _ada(c_all, w_ada_final, b_ada_final)
    sh1, sc1, gt1, sh2, sc2, gt2 = [m[:, None, :] for m in jnp.split(mod, 6, axis=-1)]
    shf, scf = [m[:, None, :] for m in jnp.split(mod_f, 2, axis=-1)]

    w_pack, w_t = _pack_w_in(w_in[0])
    bf_pad = jnp.pad(b_f[0], (0, LANES - N_HEADS)).reshape(1, LANES)
    pq, pq_t = _placement()
    g1 = g_norm1[0].reshape(1, D_MODEL)
    g2 = g_norm2[0].reshape(1, D_MODEL)
    gf = g_final.reshape(1, D_MODEL)
    g_att = g_attn_out[0].reshape(1, ATT_DIM)
    g_conv = g_conv_out[0].reshape(1, CONV_DIM)
    wo, wu, wd = w_out[0].astype(BF16), w_up[0].astype(BF16), w_down[0].astype(BF16)

    def layer(x, rows, prev, f0, tile, transposed_qv, attend):
        sel = lambda m: m[rows]
        k, v, logf, qa, ka, va, cn, cs = _proj(x, sel(sh1), sel(sc1), g1, w_pack, w_t, bf_pad, w_conv[0],
                                               g_conv, prev, f0, _tri(tile), pq, pq_t, tile,
                                               transposed_qv)
        att = attend(qa, ka, va)
        mods = [sel(m) for m in (gt1, sh2, sc2, gt2, shf, scf)]
        y = _post(x, att, cn, mods, g_att, g2, gf, wo, wu, wd, min(tile_post, x.shape[1]))
        bsz, sl = x.shape[0], x.shape[1]
        heads = lambda a: a.reshape(1, bsz, sl, N_HEADS, HEAD_DIM)
        return y, heads(k), heads(v), logf[None], cs[None]

    zeros_prev = jnp.zeros((nb, CONV_WIDTH - 1, CONV_DIM), F32)
    zeros_f = jnp.zeros((nb, 1, LANES), F32)
    yp, kp, vp, lp, cp = layer(x_prompt, slice(0, nb), zeros_prev, zeros_f, tile_prompt, True,
                               lambda qt, ka, vt: _attn(qt, ka, vt, tq, heads_per_step))

    kt = jnp.transpose(cache_k[0], (0, 2, 3, 1))
    vt = jnp.transpose(cache_v[0], (0, 2, 3, 1))
    fc = _cache_cumsum(jnp.transpose(cache_logf[0], (0, 2, 1)), _tri(tile_cache).T, tile_cache)
    f_tot = jnp.pad(fc[:, :, p - 1], ((0, 0), (0, LANES - N_HEADS)))[:, None, :]
    ys, ks, vs, ls, cs = layer(x_sample, slice(nb, nb + db), cache_conv[0], f_tot, t, False,
                               lambda qa, ka, va: _attn_sample(qa, ka, va, kt, vt, fc))
    return (yp, ys, kp, vp, lp, cp, ks, vs, ls, cs)
```

```python
import functools

import jax
import jax.numpy as jnp
import numpy as np
from jax import lax
from jax.experimental import pallas as pl
from jax.experimental.pallas import tpu as pltpu

F32 = jnp.float32
BF16 = jnp.bfloat16

D_MODEL = 1024
N_HEADS = 8
HEAD_DIM = 64
ATT_DIM = N_HEADS * HEAD_DIM
CONV_DIM = 512
CONV_WIDTH = 3
D_FF = 4 * D_MODEL
NORM_EPS = 1e-6
ATT_SCALE = HEAD_DIM ** -0.5

LANES = 128
CARRY_ROWS = 8
N_SPLIT = 3
ONES_ROW = N_SPLIT * N_HEADS
VMEM_LIMIT_BYTES = 56 * 1024 * 1024
BF16_ROWS = 16
SHIFT_ROW = 8
E_F0 = 16
ROW_CHUNKS = 2
POST_ROW_CHUNK = 256
LOG2E = 1.4426950408889634

SEC_Q, SEC_K, SEC_V, SEC_BG, SEC_CG, SEC_U = range(6)
SEC_W = 512
F_COL = 6 * SEC_W
PROJ_COLS = F_COL + LANES


def _const_spec(shape):
    zeros = (0,) * len(shape)
    return pl.BlockSpec(shape, lambda *_: zeros, pipeline_mode=pl.Buffered(1))


def _rms(x):
    return x * lax.rsqrt(jnp.mean(x * x, axis=-1, keepdims=True) + NORM_EPS)


def _log_sigmoid(x):
    return jnp.minimum(x, 0.0) - jnp.log1p(jnp.exp(-jnp.abs(x)))


def _split3(x):
    hi = x.astype(BF16).astype(F32)
    r = x - hi
    mid = r.astype(BF16).astype(F32)
    lo = (r - mid).astype(BF16).astype(F32)
    return hi, mid, lo


def _cumsum_rows(l128, tri_ref, lane):
    hi, mid, lo = _split3(l128)
    cat = hi + pltpu.roll(mid, N_HEADS, axis=1) + pltpu.roll(lo, 2 * N_HEADS, axis=1)
    sums = jnp.dot(tri_ref[...], cat.astype(BF16), preferred_element_type=F32)
    total = ((sums + pltpu.roll(sums, LANES - N_HEADS, axis=1))
             + pltpu.roll(sums, LANES - 2 * N_HEADS, axis=1))
    return jnp.where(lane < N_HEADS, total, 0.0)


def _f_parts(f128):
    hi, mid, lo = _split3(f128 * LOG2E)
    return hi + pltpu.roll(mid, N_HEADS, axis=1) + pltpu.roll(lo, 2 * N_HEADS, axis=1)


def _fcat(parts, lane):
    return (parts + jnp.where(lane == ONES_ROW, 1.0, 0.0)).astype(BF16)


def _k_extras(parts, lane):
    ones = jnp.where((lane < N_SPLIT) | ((lane >= SHIFT_ROW) & (lane < SHIFT_ROW + N_SPLIT)), 1.0, 0.0)
    odd = ones - pltpu.roll(parts, E_F0, axis=1)
    return pltpu.roll(odd, HEAD_DIM, axis=1), odd


def _store_aug(dst_ref, main, extras_even, extras_odd, lane):
    low = lane < HEAD_DIM
    for pair in range(N_HEADS // 2):
        m = main[:, pair * LANES:(pair + 1) * LANES]
        dst_ref[0, 2 * pair] = jnp.where(low, m, extras_even(pair)).astype(BF16)
        dst_ref[0, 2 * pair + 1] = jnp.where(low, extras_odd(pair), m).astype(BF16)


def _store_v_aug(dst_ref, v, lane):
    even = jnp.where(lane == HEAD_DIM, 1.0, 0.0)
    odd = jnp.where(lane == 0, 1.0, 0.0)
    _store_aug(dst_ref, v, lambda pair: even, lambda pair: odd, lane)


def _ada_kernel(c_ref, w_ref, b_ref, o_ref):
    c = c_ref[...]
    sc = (c * jax.nn.sigmoid(c)).astype(BF16)
    o_ref[...] = jnp.dot(sc, w_ref[...].astype(BF16), preferred_element_type=F32) + b_ref[...]


def _ada(c, w, b):
    rows, n = c.shape[0], w.shape[1]
    bn = 1024
    return pl.pallas_call(
        _ada_kernel,
        grid=(n // bn,),
        in_specs=[pl.BlockSpec((rows, D_MODEL), lambda j: (0, 0)),
                  pl.BlockSpec((D_MODEL, bn), lambda j: (0, j)),
                  pl.BlockSpec((1, bn), lambda j: (0, j))],
        out_specs=pl.BlockSpec((rows, bn), lambda j: (0, j)),
        out_shape=jax.ShapeDtypeStruct((rows, n), F32),
        name="ada",
    )(c, w, b.reshape(1, n))


def _store_qv_transposed(qa_ref, va_ref, q_t, v_t, eq_t, tile):
    ones_row = jnp.where(lax.broadcasted_iota(jnp.int32, (HEAD_DIM, tile), 0) == 0, 1.0, 0.0)
    for h in range(N_HEADS):
        rows = slice(h * HEAD_DIM, (h + 1) * HEAD_DIM)
        parts = [q_t[rows], eq_t[rows]]
        qa_ref[0, h] = jnp.concatenate(parts if h % 2 == 0 else parts[::-1], axis=0).astype(BF16)
        va_ref[0, h] = jnp.concatenate([v_t[rows], ones_row], axis=0).astype(BF16)


def _proj_kernel(x_ref, sh_ref, sc_ref, g1_ref, w_ref, wt_ref, bf_ref, bfc_ref, wconv_ref, gconv_ref, prev_ref,
                 f0_ref, tri_ref, pq_ref, pqt_ref,
                 k_ref, v_ref, logf_ref, qa_ref, ka_ref, va_ref, cn_ref, cs_ref,
                 ue_ref, fc_ref, *, tile, transposed_qv):
    s = pl.program_id(1)

    @pl.when(s == 0)
    def _():
        ue_ref[0:CARRY_ROWS, :] = jnp.zeros((CARRY_ROWS, CONV_DIM), F32)
        ue_ref[CARRY_ROWS - (CONV_WIDTH - 1):CARRY_ROWS, :] = prev_ref[0]
        fc_ref[...] = f0_ref[0]

    lane = lax.broadcasted_iota(jnp.int32, (tile, LANES), 1)
    nt = (((1,), (1,)), ((), ()))
    gain = g1_ref[...] * (1.0 + sc_ref[0])
    shift = sh_ref[0]

    chunk = tile // ROW_CHUNKS if tile % (ROW_CHUNKS * BF16_ROWS) == 0 else tile
    starts = list(range(0, tile, chunk))

    def conv_matmuls(r0):
        hb_c = (_rms(x_ref[0, r0:r0 + chunk, :]) * gain + shift).astype(BF16)
        sec = lambda i: jnp.dot(hb_c, w_ref[:, i * SEC_W:(i + 1) * SEC_W], preferred_element_type=F32)
        return hb_c, sec(SEC_CG), sec(SEC_U), sec(SEC_BG)

    def conv_elementwise(r0, cg, u, bg):
        up = cg * u
        base = CARRY_ROWS + r0
        ue_ref[base:base + chunk, :] = up
        cv = (wconv_ref[0:1, :] * ue_ref[base - 2:base - 2 + chunk, :]
              + wconv_ref[1:2, :] * ue_ref[base - 1:base - 1 + chunk, :]
              + wconv_ref[2:3, :] * up)
        cn_ref[0, r0:r0 + chunk, :] = (_rms(bg * cv) * gconv_ref[...]).astype(BF16)

    hbs, pending = [], None
    for r0 in starts:
        hb_c, cg, u, bg = conv_matmuls(r0)
        hbs.append(hb_c)
        if pending is not None:
            conv_elementwise(*pending)
        pending = (r0, cg, u, bg)
    hb = jnp.concatenate(hbs, axis=0) if len(hbs) > 1 else hbs[0]

    def proj(sec, width=SEC_W):
        return jnp.dot(hb, w_ref[:, sec * SEC_W:sec * SEC_W + width], preferred_element_type=F32)

    if transposed_qv:
        fl_t = lax.dot_general(wt_ref[ATT_DIM:, :], hb, nt, preferred_element_type=F32)
        fl = fl_t.T
    else:
        fl = proj(6, LANES)
    k = proj(SEC_K)
    k_ref[0] = k
    conv_elementwise(*pending)
    ue_ref[0:CARRY_ROWS, :] = ue_ref[tile:tile + CARRY_ROWS, :]

    logf = jnp.where(lane < N_HEADS, _log_sigmoid(fl + bf_ref[...]), 0.0)
    if transposed_qv:
        logf_ref[0] = _log_sigmoid(fl_t[0:N_HEADS] + bfc_ref[:, 0:1])
    else:
        logf_ref[0] = logf[:, :N_HEADS]
    f128 = _cumsum_rows(logf, tri_ref, lane) + fc_ref[...]
    v = proj(SEC_V)
    v_ref[0] = v
    fc_ref[...] = f128[tile - 1:tile, :]
    parts = _f_parts(f128)
    fcat = _fcat(parts, lane)

    ke_even, ke_odd = _k_extras(parts, lane)
    _store_aug(ka_ref, k, lambda pair: ke_even, lambda pair: ke_odd, lane)
    if transposed_qv:
        eq_t = lax.dot_general(pqt_ref[...], fcat, nt, preferred_element_type=F32)
        q_t = lax.dot_general(wt_ref[:ATT_DIM, :], hb, nt, preferred_element_type=F32)
        _store_qv_transposed(qa_ref, va_ref, q_t, v.T, eq_t, tile)
    else:
        eq = jnp.dot(fcat, pq_ref[...], preferred_element_type=F32)
        _store_aug(qa_ref, proj(SEC_Q), lambda pair: eq[:, 2 * pair * LANES:(2 * pair + 1) * LANES],
                   lambda pair: eq[:, (2 * pair + 1) * LANES:(2 * pair + 2) * LANES], lane)
        _store_v_aug(va_ref, v, lane)

    @pl.when(s == pl.num_programs(1) - 1)
    def _():
        cs_ref[0] = ue_ref[CARRY_ROWS + tile - (CONV_WIDTH - 1):CARRY_ROWS + tile, :]


def _proj(x, sh, sc, g1, w_pack, w_t, bf_pad, bf_col, w_conv, g_conv, prev, f0, tri, pq, pq_t, tile,
          transposed_qv):
    b, s, _ = x.shape
    assert s % tile == 0 and tile % CARRY_ROWS == 0
    ns = s // tile
    tok = lambda n: pl.BlockSpec((1, tile, n), lambda i, j: (i, j, 0))
    per_b = lambda r, n: pl.BlockSpec((1, r, n), lambda i, j: (i, 0, 0))
    aug = pl.BlockSpec((1, N_HEADS, tile, LANES), lambda i, j: (i, 0, j, 0))
    aug_sds = jax.ShapeDtypeStruct((b, N_HEADS, s, LANES), BF16)
    if transposed_qv:
        qv = pl.BlockSpec((1, N_HEADS, LANES, tile), lambda i, j: (i, 0, 0, j))
        qv_sds = jax.ShapeDtypeStruct((b, N_HEADS, LANES, s), BF16)
        lf = pl.BlockSpec((1, N_HEADS, tile), lambda i, j: (i, 0, j))
        lf_sds = jax.ShapeDtypeStruct((b, N_HEADS, s), F32)
    else:
        qv, qv_sds = aug, aug_sds
        lf, lf_sds = tok(N_HEADS), jax.ShapeDtypeStruct((b, s, N_HEADS), F32)
    out_shape = (
        jax.ShapeDtypeStruct((b, s, ATT_DIM), F32),
        jax.ShapeDtypeStruct((b, s, ATT_DIM), F32),
        lf_sds,
        qv_sds,
        aug_sds,
        qv_sds,
        jax.ShapeDtypeStruct((b, s, CONV_DIM), BF16),
        jax.ShapeDtypeStruct((b, CONV_WIDTH - 1, CONV_DIM), F32),
    )
    return pl.pallas_call(
        functools.partial(_proj_kernel, tile=tile, transposed_qv=transposed_qv),
        grid=(b, ns),
        in_specs=[tok(D_MODEL), per_b(1, D_MODEL), per_b(1, D_MODEL), _const_spec((1, D_MODEL)),
                  _const_spec((D_MODEL, PROJ_COLS)), _const_spec((ATT_DIM + LANES, D_MODEL)),
                  _const_spec((1, LANES)), _const_spec((N_HEADS, LANES)),
                  _const_spec((CONV_WIDTH, CONV_DIM)), _const_spec((1, CONV_DIM)),
                  per_b(CONV_WIDTH - 1, CONV_DIM), per_b(1, LANES),
                  _const_spec((tile, tile)), _const_spec((LANES, N_HEADS * LANES)),
                  _const_spec((ATT_DIM, LANES))],
        out_specs=(tok(ATT_DIM), tok(ATT_DIM), lf, qv, aug, qv, tok(CONV_DIM),
                   per_b(CONV_WIDTH - 1, CONV_DIM)),
        out_shape=out_shape,
        scratch_shapes=[pltpu.VMEM((tile + CARRY_ROWS, CONV_DIM), F32), pltpu.VMEM((1, LANES), F32)],
        compiler_params=pltpu.CompilerParams(dimension_semantics=("arbitrary", "arbitrary"),
                                             vmem_limit_bytes=VMEM_LIMIT_BYTES),
        name="proj",
    )(x, sh, sc, g1, w_pack, w_t, bf_pad, bf_col, w_conv, g_conv, prev, f0, tri, pq, pq_t)


def _cache_cumsum_kernel(l_ref, triu_ref, f_ref, *, tile):
    n = l_ref.shape[2] // tile
    rows = []
    for c in range(n):
        rows.extend(_split3(l_ref[0, :, c * tile:(c + 1) * tile]))
    sums = jnp.dot(jnp.concatenate(rows, axis=0).astype(BF16), triu_ref[...],
                   preferred_element_type=F32)
    carry = jnp.zeros((N_HEADS, 1), F32)
    for c in range(n):
        hi, mid, lo = [sums[(N_SPLIT * c + i) * N_HEADS:(N_SPLIT * c + i + 1) * N_HEADS]
                       for i in range(N_SPLIT)]
        f = ((hi + mid) + lo) + carry
        f_ref[0, :, c * tile:(c + 1) * tile] = f
        carry = f[:, tile - 1:tile]


def _cache_cumsum(cl_t, triu, tile):
    b, h, p = cl_t.shape
    blk = pl.BlockSpec((1, h, p), lambda i: (i, 0, 0))
    return pl.pallas_call(
        functools.partial(_cache_cumsum_kernel, tile=tile),
        grid=(b,),
        in_specs=[blk, _const_spec((tile, tile))],
        out_specs=blk,
        out_shape=jax.ShapeDtypeStruct((b, h, p), F32),
        compiler_params=pltpu.CompilerParams(dimension_semantics=("arbitrary",)),
        name="cache_cumsum",
    )(cl_t, triu)


def _softmax_block_t(qt, k, vt, m, acc, mask):
    s = jnp.dot(k, qt, preferred_element_type=F32)
    if mask is not None:
        s = jnp.where(mask, s, -jnp.inf)
    m_new = jnp.maximum(m, jnp.max(s, axis=0, keepdims=True))
    p = jnp.exp2(s - m_new)
    acc = acc * jnp.exp2(m - m_new) + jnp.dot(vt, p.astype(BF16), preferred_element_type=F32)
    return m_new, acc


def _head_online(qt, kv_block, qi, causal, tq):
    def body(j, carry):
        k, vt = kv_block(j)
        return _softmax_block_t(qt, k, vt, carry[0], carry[1], None)

    init = (jnp.full((1, tq), -jnp.inf, F32), jnp.zeros((LANES, tq), F32))
    m, acc = lax.fori_loop(0, qi, body, init)
    k, vt = kv_block(qi)
    return _softmax_block_t(qt, k, vt, m, acc, causal)[1]


def _with_shift_rows(qt, m, parity):
    g0 = HEAD_DIM if parity == 0 else 0
    hi, mid, lo = _split3(-m)
    row = lax.broadcasted_iota(jnp.int32, (BF16_ROWS, m.shape[1]), 0)
    add = jnp.where(row == SHIFT_ROW, hi,
                    jnp.where(row == SHIFT_ROW + 1, mid, jnp.where(row == SHIFT_ROW + 2, lo, 0.0)))
    grp = (qt[g0:g0 + BF16_ROWS].astype(F32) + add).astype(BF16)
    pieces = ([qt[:g0]] if g0 else []) + [grp, qt[g0 + BF16_ROWS:]]
    return jnp.concatenate(pieces, axis=0)


def _heads_fixed_max(qts, kv_block, qi, causal, acc_ref, qs_ref):
    n = len(qts)
    dot = functools.partial(jnp.dot, preferred_element_type=F32)
    half = causal.shape[0]

    def diag(hh):
        k, qt = kv_block(hh, qi)[0], qts[hh]
        s_a = dot(k[:half], qt)
        s_a = jnp.concatenate([jnp.where(causal, s_a[:, :half], -jnp.inf), s_a[:, half:]], axis=1)
        return s_a, jnp.where(causal, dot(k[half:], qt[:, half:]), -jnp.inf)

    s_next = diag(0)
    for hh in range(n):
        (s_a, s_b), s_next = s_next, (diag(hh + 1) if hh + 1 < n else None)
        m_a = jnp.max(s_a, axis=0, keepdims=True)
        m_late = jnp.maximum(m_a[:, half:], jnp.max(s_b, axis=0, keepdims=True))
        m = jnp.concatenate([m_a[:, :half], m_late], axis=1)
        qs_ref[hh] = _with_shift_rows(qts[hh], m, hh % 2)
        vt = kv_block(hh, qi)[1]
        acc = dot(vt[:, :half], jnp.exp2(s_a - m).astype(BF16))
        late = acc[:, half:] + dot(vt[:, half:], jnp.exp2(s_b - m_late).astype(BF16))
        acc_ref[hh] = jnp.concatenate([acc[:, :half], late], axis=1)

    def body(j, carry):
        scores = lambda hh: dot(kv_block(hh, j)[0], qs_ref[hh])
        s_next = scores(0)
        for hh in range(n):
            s, s_next = s_next, (scores(hh + 1) if hh + 1 < n else None)
            acc_ref[hh] += dot(kv_block(hh, j)[1], jnp.exp2(s).astype(BF16))
        return carry

    lax.fori_loop(0, qi, body, 0)
    return [acc_ref[hh] for hh in range(n)]


def _attn_kernel(qt_ref, k_ref, vt_ref, o_ref, acc_ref, qs_ref, *, tq):
    qi = pl.program_id(2)

    def causal_mask(n):
        return (lax.broadcasted_iota(jnp.int32, (n, n), 0)
                <= lax.broadcasted_iota(jnp.int32, (n, n), 1))

    def kv_block(hh, j):
        off = pl.multiple_of(j * tq, tq)
        return k_ref[0, hh, pl.ds(off, tq), :], vt_ref[0, hh, :, pl.ds(off, tq)]

    heads = qt_ref.shape[1]
    qts = [qt_ref[0, hh] for hh in range(heads)]

    def finish(accs):
        outs = [acc[0:HEAD_DIM] * (1.0 / acc[HEAD_DIM:HEAD_DIM + 1]) for acc in accs]
        finite = None
        for pair in range(heads // 2):
            both = jnp.concatenate([outs[2 * pair], outs[2 * pair + 1]], axis=0)
            o_ref[0, :, pair * LANES:(pair + 1) * LANES] = both.T
            ok = jnp.where(jnp.isfinite(both), 1.0, 0.0)
            finite = ok if finite is None else jnp.minimum(finite, ok)
        return jnp.min(finite) > 0.5

    all_finite = finish(_heads_fixed_max(qts, kv_block, qi, causal_mask(tq // 2), acc_ref, qs_ref))

    @pl.when(jnp.logical_not(all_finite))
    def _():
        finish([_head_online(qts[hh], functools.partial(kv_block, hh), qi, causal_mask(tq), tq)
                for hh in range(heads)])


def _attn(qt, ka, vt, tq, heads):
    b, _, s, _ = ka.shape
    assert heads % 2 == 0 and N_HEADS % heads == 0
    return pl.pallas_call(
        functools.partial(_attn_kernel, tq=tq),
        grid=(b, N_HEADS // heads, s // tq),
        in_specs=[pl.BlockSpec((1, heads, LANES, tq), lambda i, p, j: (i, p, 0, j)),
                  pl.BlockSpec((1, heads, s, LANES), lambda i, p, j: (i, p, 0, 0)),
                  pl.BlockSpec((1, heads, LANES, s), lambda i, p, j: (i, p, 0, 0))],
        out_specs=pl.BlockSpec((1, tq, heads * HEAD_DIM), lambda i, p, j: (i, j, p)),
        out_shape=jax.ShapeDtypeStruct((b, s, ATT_DIM), F32),
        scratch_shapes=[pltpu.VMEM((heads, LANES, tq), F32), pltpu.VMEM((heads, LANES, tq), BF16)],
        compiler_params=pltpu.CompilerParams(
            dimension_semantics=("arbitrary", "arbitrary", "arbitrary"),
            vmem_limit_bytes=VMEM_LIMIT_BYTES),
        name="attn",
    )(qt, ka, vt)


def _attn_sample_kernel(q_ref, kn_ref, vn_ref, kt_ref, vt_ref, fc_ref, o_ref, *, t):
    nt = (((1,), (1,)), ((), ()))
    causal = (lax.broadcasted_iota(jnp.int32, (t, LANES), 1)
              <= lax.broadcasted_iota(jnp.int32, (t, LANES), 0))
    pad = jnp.zeros((LANES - t, LANES), BF16)
    fk = fc_ref[0] * LOG2E
    outs = []
    for h in range(N_HEADS):
        head, extras = (0, HEAD_DIM) if h % 2 == 0 else (HEAD_DIM, 0)
        tile = q_ref[0, h]
        ex = tile[:, extras:extras + N_SPLIT].astype(F32)
        fq = (ex[:, 0:1] + ex[:, 1:2]) + ex[:, 2:3]
        s_old = jnp.dot(tile[:, head:head + HEAD_DIM], kt_ref[0, h].astype(BF16),
                        preferred_element_type=F32) + (fq - fk[h:h + 1, :])
        kn = jnp.concatenate([kn_ref[0, h], pad], axis=0)
        vn = jnp.concatenate([vn_ref[0, h], pad], axis=0)
        s_new = jnp.where(causal, lax.dot_general(tile, kn, nt, preferred_element_type=F32), -jnp.inf)
        m = jnp.maximum(jnp.max(s_old, axis=1, keepdims=True), jnp.max(s_new, axis=1, keepdims=True))
        p_old = jnp.exp2(s_old - m)
        acc_new = jnp.dot(jnp.exp2(s_new - m).astype(BF16), vn, preferred_element_type=F32)
        o_old = lax.dot_general(p_old.astype(BF16), vt_ref[0, h].astype(BF16), nt,
                                preferred_element_type=F32)
        denom = jnp.sum(p_old, axis=1, keepdims=True) + acc_new[:, extras:extras + 1]
        outs.append((o_old + acc_new[:, head:head + HEAD_DIM]) * (1.0 / denom))
    o_ref[0] = jnp.concatenate(outs, axis=1)


def _attn_sample(qa, kn, vn, kt, vt, fc):
    b, _, t, _ = qa.shape
    p = kt.shape[3]
    new = pl.BlockSpec((1, N_HEADS, t, LANES), lambda i: (i, 0, 0, 0))
    old = pl.BlockSpec((1, N_HEADS, HEAD_DIM, p), lambda i: (i, 0, 0, 0))
    return pl.pallas_call(
        functools.partial(_attn_sample_kernel, t=t),
        grid=(b,),
        in_specs=[new, new, new, old, old, pl.BlockSpec((1, N_HEADS, p), lambda i: (i, 0, 0))],
        out_specs=pl.BlockSpec((1, t, ATT_DIM), lambda i: (i, 0, 0)),
        out_shape=jax.ShapeDtypeStruct((b, t, ATT_DIM), F32),
        compiler_params=pltpu.CompilerParams(dimension_semantics=("arbitrary",),
                                             vmem_limit_bytes=VMEM_LIMIT_BYTES),
        name="attn_sample",
    )(qa, kn, vn, kt, vt, fc)


def _post_kernel(x_ref, att_ref, cn_ref, gt1_ref, sh2_ref, sc2_ref, gt2_ref, shf_ref, scf_ref,
                 gatt_ref, g2_ref, gf_ref, wo_ref, wu_ref, wd_ref, y_ref, *, ff_chunk, row_chunk):
    tile = x_ref.shape[1]
    chunks = [slice(r0, r0 + row_chunk) for r0 in range(0, tile, row_chunk)]
    gain2 = g2_ref[...] * (1.0 + sc2_ref[0])
    gain_f = gf_ref[...] * (1.0 + scf_ref[0])

    def normed_att(rows):
        return (_rms(att_ref[0, rows, :]) * gatt_ref[...]).astype(BF16)

    def out_proj(rows, xa):
        return (jnp.dot(xa, wo_ref[0:ATT_DIM, :], preferred_element_type=F32)
                + jnp.dot(cn_ref[0, rows, :], wo_ref[ATT_DIM:, :], preferred_element_type=F32))

    def residual1(rows, mixed):
        x1 = x_ref[0, rows, :] + gt1_ref[0] * mixed
        return x1, (_rms(x1) * gain2 + sh2_ref[0]).astype(BF16)

    def mlp(h2):
        total = None
        for c in range(D_FF // ff_chunk):
            up = jnp.dot(h2, wu_ref[:, c * ff_chunk:(c + 1) * ff_chunk], preferred_element_type=F32)
            act = jnp.square(jnp.maximum(up, 0.0)).astype(BF16)
            part = jnp.dot(act, wd_ref[c * ff_chunk:(c + 1) * ff_chunk, :], preferred_element_type=F32)
            total = part if total is None else total + part
        return total

    def finish(rows, x1, m):
        x2 = x1 + gt2_ref[0] * m
        y_ref[0, rows, :] = _rms(x2) * gain_f + shf_ref[0]

    n = len(chunks)
    xa = normed_att(chunks[0])
    mixed = []
    for c in range(n):
        mixed.append(out_proj(chunks[c], xa))
        if c + 1 < n:
            xa = normed_att(chunks[c + 1])
    x1, h2 = residual1(chunks[0], mixed[0])
    pending = None
    for c in range(n):
        m = mlp(h2)
        done = (chunks[c], x1, m)
        if c + 1 < n:
            x1, h2 = residual1(chunks[c + 1], mixed[c + 1])
        if pending is not None:
            finish(*pending)
        pending = done
    finish(*pending)


def _post(x, att, cn, mods, g_att, g2, g_final, w_out, w_up, w_down, tile):
    b, s, _ = x.shape
    row_chunk = POST_ROW_CHUNK if tile % POST_ROW_CHUNK == 0 else tile
    tok = lambda n: pl.BlockSpec((1, tile, n), lambda i, j: (i, j, 0))
    per_b = pl.BlockSpec((1, 1, D_MODEL), lambda i, j: (i, 0, 0))
    return pl.pallas_call(
        functools.partial(_post_kernel, ff_chunk=1024, row_chunk=row_chunk),
        grid=(b, s // tile),
        in_specs=[tok(D_MODEL), tok(ATT_DIM), tok(CONV_DIM)] + [per_b] * 6
                 + [_const_spec((1, ATT_DIM)), _const_spec((1, D_MODEL)), _const_spec((1, D_MODEL)),
                    _const_spec((D_MODEL, D_MODEL)), _const_spec((D_MODEL, D_FF)),
                    _const_spec((D_FF, D_MODEL))],
        out_specs=tok(D_MODEL),
        out_shape=jax.ShapeDtypeStruct((b, s, D_MODEL), F32),
        compiler_params=pltpu.CompilerParams(dimension_semantics=("arbitrary", "arbitrary"),
                                             vmem_limit_bytes=VMEM_LIMIT_BYTES),
        name="post",
    )(x, att, cn, *mods, g_att, g2, g_final, w_out, w_up, w_down)


def _placement():
    pq = np.zeros((LANES, N_HEADS * LANES), np.float32)
    pq_t = np.zeros((ATT_DIM, LANES), np.float32)
    for h in range(N_HEADS):
        base = h * LANES + (HEAD_DIM if h % 2 == 0 else 0)
        for part in range(N_SPLIT):
            src, own = part * N_HEADS + h, E_F0 + part * N_HEADS + h
            pq[src, base + part] = 1.0
            pq[ONES_ROW, base + own] = 1.0
            pq_t[h * HEAD_DIM + part, src] = 1.0
            pq_t[h * HEAD_DIM + own, ONES_ROW] = 1.0
    return jnp.asarray(pq, BF16), jnp.asarray(pq_t, BF16)


def _tri(n):
    return jnp.asarray(np.tri(n, dtype=np.float32), BF16)


def _pack_w_in(w):
    a, c = ATT_DIM, CONV_DIM
    f0 = 3 * a
    b0 = f0 + N_HEADS
    pad = jnp.zeros((D_MODEL, LANES - N_HEADS), w.dtype)
    wq = w[:, :a] * (ATT_SCALE * LOG2E)
    packed = jnp.concatenate([wq, w[:, a:f0], w[:, b0:b0 + 3 * c], w[:, f0:b0], pad], axis=1)
    transposed = jnp.concatenate([wq, w[:, f0:b0], pad], axis=1).T
    return packed.astype(BF16), transposed.astype(BF16)


def kernel(x_prompt, x_sample, cache_k, cache_v, cache_logf, cache_conv, c_prompt, c_sample, w_ada, b_ada, g_norm1, g_norm2, w_in, b_f, w_conv, g_attn_out, g_conv_out, w_out, w_up, w_down, w_ada_final, b_ada_final, g_final):
    assert w_ada.shape[0] == 1, "single layer"
    nb, s, _ = x_prompt.shape
    db, t, _ = x_sample.shape
    p = cache_k.shape[2]
    tile_prompt, tile_cache, tq, heads_per_step, tile_post = 512, 512, 512, 8, 512

    c_all = jnp.concatenate([c_prompt, c_sample], axis=0)
    mod = _ada(c_all, w_ada[0], b_ada[0])
    mod_f = _ada(c_all, w_ada_final, b_ada_final)
    sh1, sc1, gt1, sh2, sc2, gt2 = [m[:, None, :] for m in jnp.split(mod, 6, axis=-1)]
    shf, scf = [m[:, None, :] for m in jnp.split(mod_f, 2, axis=-1)]

    w_pack, w_t = _pack_w_in(w_in[0])
    bf_pad = jnp.pad(b_f[0], (0, LANES - N_HEADS)).reshape(1, LANES)
    bf_col = jnp.broadcast_to(b_f[0][:, None], (N_HEADS, LANES))
    pq, pq_t = _placement()
    g1 = g_norm1[0].reshape(1, D_MODEL)
    g2 = g_norm2[0].reshape(1, D_MODEL)
    gf = g_final.reshape(1, D_MODEL)
    g_att = g_attn_out[0].reshape(1, ATT_DIM)
    g_conv = g_conv_out[0].reshape(1, CONV_DIM)
    wo, wu, wd = w_out[0].astype(BF16), w_up[0].astype(BF16), w_down[0].astype(BF16)

    def layer(x, rows, prev, f0, tile, transposed_qv, attend):
        sel = lambda m: m[rows]
        k, v, logf, qa, ka, va, cn, cs = _proj(x, sel(sh1), sel(sc1), g1, w_pack, w_t, bf_pad, bf_col, w_conv[0],
                                               g_conv, prev, f0, _tri(tile), pq, pq_t, tile,
                                               transposed_qv)
        att = attend(qa, ka, va)
        mods = [sel(m) for m in (gt1, sh2, sc2, gt2, shf, scf)]
        y = _post(x, att, cn, mods, g_att, g2, gf, wo, wu, wd, min(tile_post, x.shape[1]))
        bsz, sl = x.shape[0], x.shape[1]
        heads = lambda a: a.reshape(1, bsz, sl, N_HEADS, HEAD_DIM)
        if transposed_qv:
            logf = jnp.transpose(logf, (0, 2, 1))
        return y, heads(k), heads(v), logf[None], cs[None]

    zeros_prev = jnp.zeros((nb, CONV_WIDTH - 1, CONV_DIM), F32)
    zeros_f = jnp.zeros((nb, 1, LANES), F32)
    yp, kp, vp, lp, cp = layer(x_prompt, slice(0, nb), zeros_prev, zeros_f, tile_prompt, True,
                               lambda qt, ka, vt: _attn(qt, ka, vt, tq, heads_per_step))

    kt = jnp.transpose(cache_k[0], (0, 2, 3, 1))
    vt = jnp.transpose(cache_v[0], (0, 2, 3, 1))
    fc = _cache_cumsum(jnp.transpose(cache_logf[0], (0, 2, 1)), _tri(tile_cache).T, tile_cache)
    f_tot = jnp.pad(fc[:, :, p - 1], ((0, 0), (0, LANES - N_HEADS)))[:, None, :]
    ys, ks, vs, ls, cs = layer(x_sample, slice(nb, nb + db), cache_conv[0], f_tot, t, False,
                               lambda qa, ka, va: _attn_sample(qa, ka, va, kt, vt, fc))
    return (yp, ys, kp, vp, lp, cp, ks, vs, ls, cs)
```

```python
import functools

import jax
import jax.numpy as jnp
import numpy as np
from jax import lax
from jax.experimental import pallas as pl
from jax.experimental.pallas import tpu as pltpu

F32 = jnp.float32
BF16 = jnp.bfloat16

D_MODEL = 1024
N_HEADS = 8
HEAD_DIM = 64
ATT_DIM = N_HEADS * HEAD_DIM
CONV_DIM = 512
CONV_WIDTH = 3
D_FF = 4 * D_MODEL
NORM_EPS = 1e-6
ATT_SCALE = HEAD_DIM ** -0.5

LANES = 128
CARRY_ROWS = 8
N_SPLIT = 3
ONES_ROW = N_SPLIT * N_HEADS
VMEM_LIMIT_BYTES = 56 * 1024 * 1024
BF16_ROWS = 16
SHIFT_ROW = 8
E_F0 = 16
ROW_CHUNKS = 2
POST_ROW_CHUNK = 256
LOG2E = 1.4426950408889634

SEC_Q, SEC_K, SEC_V, SEC_BG, SEC_CG, SEC_U = range(6)
SEC_W = 512
F_COL = 6 * SEC_W
PROJ_COLS = F_COL + LANES


def _const_spec(shape):
    zeros = (0,) * len(shape)
    return pl.BlockSpec(shape, lambda *_: zeros, pipeline_mode=pl.Buffered(1))


def _rms(x):
    return x * lax.rsqrt(jnp.mean(x * x, axis=-1, keepdims=True) + NORM_EPS)


def _log_sigmoid(x):
    return jnp.minimum(x, 0.0) - jnp.log1p(jnp.exp(-jnp.abs(x)))


def _split3(x):
    hi = x.astype(BF16).astype(F32)
    r = x - hi
    mid = r.astype(BF16).astype(F32)
    lo = (r - mid).astype(BF16).astype(F32)
    return hi, mid, lo


def _cumsum_rows(l128, tri_ref, lane):
    hi, mid, lo = _split3(l128)
    cat = hi + pltpu.roll(mid, N_HEADS, axis=1) + pltpu.roll(lo, 2 * N_HEADS, axis=1)
    sums = jnp.dot(tri_ref[...], cat.astype(BF16), preferred_element_type=F32)
    total = ((sums + pltpu.roll(sums, LANES - N_HEADS, axis=1))
             + pltpu.roll(sums, LANES - 2 * N_HEADS, axis=1))
    return jnp.where(lane < N_HEADS, total, 0.0)


def _f_parts(f128):
    hi, mid, lo = _split3(f128 * LOG2E)
    return hi + pltpu.roll(mid, N_HEADS, axis=1) + pltpu.roll(lo, 2 * N_HEADS, axis=1)


def _fcat(parts, lane):
    return (parts + jnp.where(lane == ONES_ROW, 1.0, 0.0)).astype(BF16)


def _k_extras(parts, lane):
    ones = jnp.where((lane < N_SPLIT) | ((lane >= SHIFT_ROW) & (lane < SHIFT_ROW + N_SPLIT)), 1.0, 0.0)
    odd = ones - pltpu.roll(parts, E_F0, axis=1)
    return pltpu.roll(odd, HEAD_DIM, axis=1), odd


def _store_aug(dst_ref, main, extras_even, extras_odd, lane):
    low = lane < HEAD_DIM
    for pair in range(N_HEADS // 2):
        m = main[:, pair * LANES:(pair + 1) * LANES]
        dst_ref[0, 2 * pair] = jnp.where(low, m, extras_even(pair)).astype(BF16)
        dst_ref[0, 2 * pair + 1] = jnp.where(low, extras_odd(pair), m).astype(BF16)


def _store_v_aug(dst_ref, v, lane):
    even = jnp.where(lane == HEAD_DIM, 1.0, 0.0)
    odd = jnp.where(lane == 0, 1.0, 0.0)
    _store_aug(dst_ref, v, lambda pair: even, lambda pair: odd, lane)


def _ada_kernel(c_ref, w_ref, b_ref, o_ref):
    c = c_ref[...]
    sc = (c * jax.nn.sigmoid(c)).astype(BF16)
    o_ref[...] = jnp.dot(sc, w_ref[...].astype(BF16), preferred_element_type=F32) + b_ref[...]


def _ada(c, w, b):
    rows, n = c.shape[0], w.shape[1]
    bn = 1024
    return pl.pallas_call(
        _ada_kernel,
        grid=(n // bn,),
        in_specs=[pl.BlockSpec((rows, D_MODEL), lambda j: (0, 0)),
                  pl.BlockSpec((D_MODEL, bn), lambda j: (0, j)),
                  pl.BlockSpec((1, bn), lambda j: (0, j))],
        out_specs=pl.BlockSpec((rows, bn), lambda j: (0, j)),
        out_shape=jax.ShapeDtypeStruct((rows, n), F32),
        name="ada",
    )(c, w, b.reshape(1, n))


def _store_qv_transposed(qa_ref, va_ref, q_t, v_t, eq_t, tile):
    ones_row = jnp.where(lax.broadcasted_iota(jnp.int32, (HEAD_DIM, tile), 0) == 0, 1.0, 0.0)
    for h in range(N_HEADS):
        rows = slice(h * HEAD_DIM, (h + 1) * HEAD_DIM)
        parts = [q_t[rows], eq_t[rows]]
        qa_ref[0, h] = jnp.concatenate(parts if h % 2 == 0 else parts[::-1], axis=0).astype(BF16)
        va_ref[0, h] = jnp.concatenate([v_t[rows], ones_row], axis=0).astype(BF16)


def _proj_kernel(x_ref, sh_ref, sc_ref, g1_ref, w_ref, wt_ref, bf_ref, bfc_ref, wconv_ref, gconv_ref, prev_ref,
                 f0_ref, tri_ref, pq_ref, pqt_ref,
                 k_ref, v_ref, logf_ref, qa_ref, ka_ref, va_ref, cn_ref, cs_ref,
                 ue_ref, fc_ref, *, tile, transposed_qv):
    s = pl.program_id(1)

    @pl.when(s == 0)
    def _():
        ue_ref[0:CARRY_ROWS, :] = jnp.zeros((CARRY_ROWS, CONV_DIM), F32)
        ue_ref[CARRY_ROWS - (CONV_WIDTH - 1):CARRY_ROWS, :] = prev_ref[0]
        fc_ref[...] = f0_ref[0]

    lane = lax.broadcasted_iota(jnp.int32, (tile, LANES), 1)
    nt = (((1,), (1,)), ((), ()))
    gain = g1_ref[...] * (1.0 + sc_ref[0])
    shift = sh_ref[0]

    chunk = tile // ROW_CHUNKS if tile % (ROW_CHUNKS * BF16_ROWS) == 0 else tile
    starts = list(range(0, tile, chunk))

    def conv_matmuls(r0):
        hb_c = (_rms(x_ref[0, r0:r0 + chunk, :]) * gain + shift).astype(BF16)
        sec = lambda i: jnp.dot(hb_c, w_ref[:, i * SEC_W:(i + 1) * SEC_W], preferred_element_type=F32)
        return hb_c, sec(SEC_CG), sec(SEC_U), sec(SEC_BG)

    def conv_elementwise(r0, cg, u, bg):
        up = cg * u
        base = CARRY_ROWS + r0
        ue_ref[base:base + chunk, :] = up
        cv = (wconv_ref[0:1, :] * ue_ref[base - 2:base - 2 + chunk, :]
              + wconv_ref[1:2, :] * ue_ref[base - 1:base - 1 + chunk, :]
              + wconv_ref[2:3, :] * up)
        cn_ref[0, r0:r0 + chunk, :] = (_rms(bg * cv) * gconv_ref[...]).astype(BF16)

    hbs, pending = [], None
    for r0 in starts:
        hb_c, cg, u, bg = conv_matmuls(r0)
        hbs.append(hb_c)
        if pending is not None:
            conv_elementwise(*pending)
        pending = (r0, cg, u, bg)
    hb = jnp.concatenate(hbs, axis=0) if len(hbs) > 1 else hbs[0]

    def proj(sec, width=SEC_W):
        return jnp.dot(hb, w_ref[:, sec * SEC_W:sec * SEC_W + width], preferred_element_type=F32)

    if transposed_qv:
        qf_t = lax.dot_general(wt_ref[...], hb, nt, preferred_element_type=F32)
        q_t, fl_t = qf_t[:ATT_DIM], qf_t[ATT_DIM:]
        fl = jnp.concatenate([fl_t, jnp.zeros((LANES - BF16_ROWS, tile), F32)], axis=0).T
    else:
        fl = proj(6, LANES)
    k = proj(SEC_K)
    k_ref[0] = k
    conv_elementwise(*pending)
    ue_ref[0:CARRY_ROWS, :] = ue_ref[tile:tile + CARRY_ROWS, :]

    logf = jnp.where(lane < N_HEADS, _log_sigmoid(fl + bf_ref[...]), 0.0)
    if transposed_qv:
        logf_ref[0] = _log_sigmoid(fl_t[0:N_HEADS] + bfc_ref[:, 0:1])
    else:
        logf_ref[0] = logf[:, :N_HEADS]
    f128 = _cumsum_rows(logf, tri_ref, lane) + fc_ref[...]
    v = proj(SEC_V)
    v_ref[0] = v
    fc_ref[...] = f128[tile - 1:tile, :]
    parts = _f_parts(f128)
    fcat = _fcat(parts, lane)

    ke_even, ke_odd = _k_extras(parts, lane)
    _store_aug(ka_ref, k, lambda pair: ke_even, lambda pair: ke_odd, lane)
    if transposed_qv:
        eq_t = lax.dot_general(pqt_ref[...], fcat, nt, preferred_element_type=F32)
        _store_qv_transposed(qa_ref, va_ref, q_t, v.T, eq_t, tile)
    else:
        eq = jnp.dot(fcat, pq_ref[...], preferred_element_type=F32)
        _store_aug(qa_ref, proj(SEC_Q), lambda pair: eq[:, 2 * pair * LANES:(2 * pair + 1) * LANES],
                   lambda pair: eq[:, (2 * pair + 1) * LANES:(2 * pair + 2) * LANES], lane)
        _store_v_aug(va_ref, v, lane)

    @pl.when(s == pl.num_programs(1) - 1)
    def _():
        cs_ref[0] = ue_ref[CARRY_ROWS + tile - (CONV_WIDTH - 1):CARRY_ROWS + tile, :]


def _proj(x, sh, sc, g1, w_pack, w_t, bf_pad, bf_col, w_conv, g_conv, prev, f0, tri, pq, pq_t, tile,
          transposed_qv):
    b, s, _ = x.shape
    assert s % tile == 0 and tile % CARRY_ROWS == 0
    ns = s // tile
    tok = lambda n: pl.BlockSpec((1, tile, n), lambda i, j: (i, j, 0))
    per_b = lambda r, n: pl.BlockSpec((1, r, n), lambda i, j: (i, 0, 0))
    aug = pl.BlockSpec((1, N_HEADS, tile, LANES), lambda i, j: (i, 0, j, 0))
    aug_sds = jax.ShapeDtypeStruct((b, N_HEADS, s, LANES), BF16)
    if transposed_qv:
        qv = pl.BlockSpec((1, N_HEADS, LANES, tile), lambda i, j: (i, 0, 0, j))
        qv_sds = jax.ShapeDtypeStruct((b, N_HEADS, LANES, s), BF16)
        lf = pl.BlockSpec((1, N_HEADS, tile), lambda i, j: (i, 0, j))
        lf_sds = jax.ShapeDtypeStruct((b, N_HEADS, s), F32)
    else:
        qv, qv_sds = aug, aug_sds
        lf, lf_sds = tok(N_HEADS), jax.ShapeDtypeStruct((b, s, N_HEADS), F32)
    out_shape = (
        jax.ShapeDtypeStruct((b, s, ATT_DIM), F32),
        jax.ShapeDtypeStruct((b, s, ATT_DIM), F32),
        lf_sds,
        qv_sds,
        aug_sds,
        qv_sds,
        jax.ShapeDtypeStruct((b, s, CONV_DIM), BF16),
        jax.ShapeDtypeStruct((b, CONV_WIDTH - 1, CONV_DIM), F32),
    )
    return pl.pallas_call(
        functools.partial(_proj_kernel, tile=tile, transposed_qv=transposed_qv),
        grid=(b, ns),
        in_specs=[tok(D_MODEL), per_b(1, D_MODEL), per_b(1, D_MODEL), _const_spec((1, D_MODEL)),
                  _const_spec((D_MODEL, PROJ_COLS)), _const_spec((ATT_DIM + BF16_ROWS, D_MODEL)),
                  _const_spec((1, LANES)), _const_spec((N_HEADS, LANES)),
                  _const_spec((CONV_WIDTH, CONV_DIM)), _const_spec((1, CONV_DIM)),
                  per_b(CONV_WIDTH - 1, CONV_DIM), per_b(1, LANES),
                  _const_spec((tile, tile)), _const_spec((LANES, N_HEADS * LANES)),
                  _const_spec((ATT_DIM, LANES))],
        out_specs=(tok(ATT_DIM), tok(ATT_DIM), lf, qv, aug, qv, tok(CONV_DIM),
                   per_b(CONV_WIDTH - 1, CONV_DIM)),
        out_shape=out_shape,
        scratch_shapes=[pltpu.VMEM((tile + CARRY_ROWS, CONV_DIM), F32), pltpu.VMEM((1, LANES), F32)],
        compiler_params=pltpu.CompilerParams(dimension_semantics=("arbitrary", "arbitrary"),
                                             vmem_limit_bytes=VMEM_LIMIT_BYTES),
        name="proj",
    )(x, sh, sc, g1, w_pack, w_t, bf_pad, bf_col, w_conv, g_conv, prev, f0, tri, pq, pq_t)


def _cache_cumsum_kernel(l_ref, triu_ref, f_ref, *, tile):
    n = l_ref.shape[2] // tile
    rows = []
    for c in range(n):
        rows.extend(_split3(l_ref[0, :, c * tile:(c + 1) * tile]))
    sums = jnp.dot(jnp.concatenate(rows, axis=0).astype(BF16), triu_ref[...],
                   preferred_element_type=F32)
    carry = jnp.zeros((N_HEADS, 1), F32)
    for c in range(n):
        hi, mid, lo = [sums[(N_SPLIT * c + i) * N_HEADS:(N_SPLIT * c + i + 1) * N_HEADS]
                       for i in range(N_SPLIT)]
        f = ((hi + mid) + lo) + carry
        f_ref[0, :, c * tile:(c + 1) * tile] = f
        carry = f[:, tile - 1:tile]


def _cache_cumsum(cl_t, triu, tile):
    b, h, p = cl_t.shape
    blk = pl.BlockSpec((1, h, p), lambda i: (i, 0, 0))
    return pl.pallas_call(
        functools.partial(_cache_cumsum_kernel, tile=tile),
        grid=(b,),
        in_specs=[blk, _const_spec((tile, tile))],
        out_specs=blk,
        out_shape=jax.ShapeDtypeStruct((b, h, p), F32),
        compiler_params=pltpu.CompilerParams(dimension_semantics=("arbitrary",)),
        name="cache_cumsum",
    )(cl_t, triu)


def _softmax_block_t(qt, k, vt, m, acc, mask):
    s = jnp.dot(k, qt, preferred_element_type=F32)
    if mask is not None:
        s = jnp.where(mask, s, -jnp.inf)
    m_new = jnp.maximum(m, jnp.max(s, axis=0, keepdims=True))
    p = jnp.exp2(s - m_new)
    acc = acc * jnp.exp2(m - m_new) + jnp.dot(vt, p.astype(BF16), preferred_element_type=F32)
    return m_new, acc


def _head_online(qt, kv_block, qi, causal, tq):
    def body(j, carry):
        k, vt = kv_block(j)
        return _softmax_block_t(qt, k, vt, carry[0], carry[1], None)

    init = (jnp.full((1, tq), -jnp.inf, F32), jnp.zeros((LANES, tq), F32))
    m, acc = lax.fori_loop(0, qi, body, init)
    k, vt = kv_block(qi)
    return _softmax_block_t(qt, k, vt, m, acc, causal)[1]


def _with_shift_rows(qt, m, parity):
    g0 = HEAD_DIM if parity == 0 else 0
    hi, mid, lo = _split3(-m)
    row = lax.broadcasted_iota(jnp.int32, (BF16_ROWS, m.shape[1]), 0)
    add = jnp.where(row == SHIFT_ROW, hi,
                    jnp.where(row == SHIFT_ROW + 1, mid, jnp.where(row == SHIFT_ROW + 2, lo, 0.0)))
    grp = (qt[g0:g0 + BF16_ROWS].astype(F32) + add).astype(BF16)
    pieces = ([qt[:g0]] if g0 else []) + [grp, qt[g0 + BF16_ROWS:]]
    return jnp.concatenate(pieces, axis=0)


def _heads_fixed_max(qts, kv_block, qi, causal, acc_ref, qs_ref):
    n = len(qts)
    dot = functools.partial(jnp.dot, preferred_element_type=F32)
    half = causal.shape[0]

    def diag(hh):
        k, qt = kv_block(hh, qi)[0], qts[hh]
        s_a = dot(k[:half], qt)
        s_a = jnp.concatenate([jnp.where(causal, s_a[:, :half], -jnp.inf), s_a[:, half:]], axis=1)
        return s_a, jnp.where(causal, dot(k[half:], qt[:, half:]), -jnp.inf)

    s_next = diag(0)
    for hh in range(n):
        (s_a, s_b), s_next = s_next, (diag(hh + 1) if hh + 1 < n else None)
        m_a = jnp.max(s_a, axis=0, keepdims=True)
        m_late = jnp.maximum(m_a[:, half:], jnp.max(s_b, axis=0, keepdims=True))
        m = jnp.concatenate([m_a[:, :half], m_late], axis=1)
        qs_ref[hh] = _with_shift_rows(qts[hh], m, hh % 2)
        vt = kv_block(hh, qi)[1]
        acc = dot(vt[:, :half], jnp.exp2(s_a - m).astype(BF16))
        late = acc[:, half:] + dot(vt[:, half:], jnp.exp2(s_b - m_late).astype(BF16))
        acc_ref[hh] = jnp.concatenate([acc[:, :half], late], axis=1)

    def body(j, carry):
        scores = lambda hh: dot(kv_block(hh, j)[0], qs_ref[hh])
        s_next = scores(0)
        for hh in range(n):
            s, s_next = s_next, (scores(hh + 1) if hh + 1 < n else None)
            acc_ref[hh] += dot(kv_block(hh, j)[1], jnp.exp2(s).astype(BF16))
        return carry

    lax.fori_loop(0, qi, body, 0)
    return [acc_ref[hh] for hh in range(n)]


def _attn_kernel(qt_ref, k_ref, vt_ref, o_ref, acc_ref, qs_ref, *, tq):
    qi = pl.program_id(2)

    def causal_mask(n):
        return (lax.broadcasted_iota(jnp.int32, (n, n), 0)
                <= lax.broadcasted_iota(jnp.int32, (n, n), 1))

    def kv_block(hh, j):
        off = pl.multiple_of(j * tq, tq)
        return k_ref[0, hh, pl.ds(off, tq), :], vt_ref[0, hh, :, pl.ds(off, tq)]

    heads = qt_ref.shape[1]
    qts = [qt_ref[0, hh] for hh in range(heads)]

    def finish(accs):
        outs = [acc[0:HEAD_DIM] * (1.0 / acc[HEAD_DIM:HEAD_DIM + 1]) for acc in accs]
        finite = None
        for pair in range(heads // 2):
            both = jnp.concatenate([outs[2 * pair], outs[2 * pair + 1]], axis=0)
            o_ref[0, :, pair * LANES:(pair + 1) * LANES] = both.T
            ok = jnp.where(jnp.isfinite(both), 1.0, 0.0)
            finite = ok if finite is None else jnp.minimum(finite, ok)
        return jnp.min(finite) > 0.5

    all_finite = finish(_heads_fixed_max(qts, kv_block, qi, causal_mask(tq // 2), acc_ref, qs_ref))

    @pl.when(jnp.logical_not(all_finite))
    def _():
        finish([_head_online(qts[hh], functools.partial(kv_block, hh), qi, causal_mask(tq), tq)
                for hh in range(heads)])


def _attn(qt, ka, vt, tq, heads):
    b, _, s, _ = ka.shape
    assert heads % 2 == 0 and N_HEADS % heads == 0
    return pl.pallas_call(
        functools.partial(_attn_kernel, tq=tq),
        grid=(b, N_HEADS // heads, s // tq),
        in_specs=[pl.BlockSpec((1, heads, LANES, tq), lambda i, p, j: (i, p, 0, j)),
                  pl.BlockSpec((1, heads, s, LANES), lambda i, p, j: (i, p, 0, 0)),
                  pl.BlockSpec((1, heads, LANES, s), lambda i, p, j: (i, p, 0, 0))],
        out_specs=pl.BlockSpec((1, tq, heads * HEAD_DIM), lambda i, p, j: (i, j, p)),
        out_shape=jax.ShapeDtypeStruct((b, s, ATT_DIM), F32),
        scratch_shapes=[pltpu.VMEM((heads, LANES, tq), F32), pltpu.VMEM((heads, LANES, tq), BF16)],
        compiler_params=pltpu.CompilerParams(
            dimension_semantics=("arbitrary", "arbitrary", "arbitrary"),
            vmem_limit_bytes=VMEM_LIMIT_BYTES),
        name="attn",
    )(qt, ka, vt)


def _attn_sample_kernel(q_ref, kn_ref, vn_ref, kt_ref, vt_ref, fc_ref, o_ref, *, t):
    nt = (((1,), (1,)), ((), ()))
    causal = (lax.broadcasted_iota(jnp.int32, (t, LANES), 1)
              <= lax.broadcasted_iota(jnp.int32, (t, LANES), 0))
    pad = jnp.zeros((LANES - t, LANES), BF16)
    fk = fc_ref[0] * LOG2E
    outs = []
    for h in range(N_HEADS):
        head, extras = (0, HEAD_DIM) if h % 2 == 0 else (HEAD_DIM, 0)
        tile = q_ref[0, h]
        ex = tile[:, extras:extras + N_SPLIT].astype(F32)
        fq = (ex[:, 0:1] + ex[:, 1:2]) + ex[:, 2:3]
        s_old = jnp.dot(tile[:, head:head + HEAD_DIM], kt_ref[0, h].astype(BF16),
                        preferred_element_type=F32) + (fq - fk[h:h + 1, :])
        kn = jnp.concatenate([kn_ref[0, h], pad], axis=0)
        vn = jnp.concatenate([vn_ref[0, h], pad], axis=0)
        s_new = jnp.where(causal, lax.dot_general(tile, kn, nt, preferred_element_type=F32), -jnp.inf)
        m = jnp.maximum(jnp.max(s_old, axis=1, keepdims=True), jnp.max(s_new, axis=1, keepdims=True))
        p_old = jnp.exp2(s_old - m)
        acc_new = jnp.dot(jnp.exp2(s_new - m).astype(BF16), vn, preferred_element_type=F32)
        o_old = lax.dot_general(p_old.astype(BF16), vt_ref[0, h].astype(BF16), nt,
                                preferred_element_type=F32)
        denom = jnp.sum(p_old, axis=1, keepdims=True) + acc_new[:, extras:extras + 1]
        outs.append((o_old + acc_new[:, head:head + HEAD_DIM]) * (1.0 / denom))
    o_ref[0] = jnp.concatenate(outs, axis=1)


def _attn_sample(qa, kn, vn, kt, vt, fc):
    b, _, t, _ = qa.shape
    p = kt.shape[3]
    new = pl.BlockSpec((1, N_HEADS, t, LANES), lambda i: (i, 0, 0, 0))
    old = pl.BlockSpec((1, N_HEADS, HEAD_DIM, p), lambda i: (i, 0, 0, 0))
    return pl.pallas_call(
        functools.partial(_attn_sample_kernel, t=t),
        grid=(b,),
        in_specs=[new, new, new, old, old, pl.BlockSpec((1, N_HEADS, p), lambda i: (i, 0, 0))],
        out_specs=pl.BlockSpec((1, t, ATT_DIM), lambda i: (i, 0, 0)),
        out_shape=jax.ShapeDtypeStruct((b, t, ATT_DIM), F32),
        compiler_params=pltpu.CompilerParams(dimension_semantics=("arbitrary",),
                                             vmem_limit_bytes=VMEM_LIMIT_BYTES),
        name="attn_sample",
    )(qa, kn, vn, kt, vt, fc)


def _post_kernel(x_ref, att_ref, cn_ref, gt1_ref, sh2_ref, sc2_ref, gt2_ref, shf_ref, scf_ref,
                 gatt_ref, g2_ref, gf_ref, wo_ref, wu_ref, wd_ref, y_ref, *, ff_chunk, row_chunk):
    tile = x_ref.shape[1]
    chunks = [slice(r0, r0 + row_chunk) for r0 in range(0, tile, row_chunk)]
    gain2 = g2_ref[...] * (1.0 + sc2_ref[0])
    gain_f = gf_ref[...] * (1.0 + scf_ref[0])

    def normed_att(rows):
        return (_rms(att_ref[0, rows, :]) * gatt_ref[...]).astype(BF16)

    def out_proj(rows, xa):
        return (jnp.dot(xa, wo_ref[0:ATT_DIM, :], preferred_element_type=F32)
                + jnp.dot(cn_ref[0, rows, :], wo_ref[ATT_DIM:, :], preferred_element_type=F32))

    def residual1(rows, mixed):
        x1 = x_ref[0, rows, :] + gt1_ref[0] * mixed
        return x1, (_rms(x1) * gain2 + sh2_ref[0]).astype(BF16)

    def mlp(h2):
        total = None
        for c in range(D_FF // ff_chunk):
            up = jnp.dot(h2, wu_ref[:, c * ff_chunk:(c + 1) * ff_chunk], preferred_element_type=F32)
            act = jnp.square(jnp.maximum(up, 0.0)).astype(BF16)
            part = jnp.dot(act, wd_ref[c * ff_chunk:(c + 1) * ff_chunk, :], preferred_element_type=F32)
            total = part if total is None else total + part
        return total

    def finish(rows, x1, m):
        x2 = x1 + gt2_ref[0] * m
        y_ref[0, rows, :] = _rms(x2) * gain_f + shf_ref[0]

    n = len(chunks)
    xa = normed_att(chunks[0])
    mixed = []
    for c in range(n):
        mixed.append(out_proj(chunks[c], xa))
        if c + 1 < n:
            xa = normed_att(chunks[c + 1])
    x1, h2 = residual1(chunks[0], mixed[0])
    pending = None
    for c in range(n):
        m = mlp(h2)
        done = (chunks[c], x1, m)
        if c + 1 < n:
            x1, h2 = residual1(chunks[c + 1], mixed[c + 1])
        if pending is not None:
            finish(*pending)
        pending = done
    finish(*pending)


def _post(x, att, cn, mods, g_att, g2, g_final, w_out, w_up, w_down, tile):
    b, s, _ = x.shape
    row_chunk = POST_ROW_CHUNK if tile % POST_ROW_CHUNK == 0 else tile
    tok = lambda n: pl.BlockSpec((1, tile, n), lambda i, j: (i, j, 0))
    if mods[0].shape[1] == 1:
        per_b = pl.BlockSpec((1, 1, D_MODEL), lambda i, j: (i, 0, 0))
    else:
        assert row_chunk == tile
        per_b = tok(D_MODEL)
    return pl.pallas_call(
        functools.partial(_post_kernel, ff_chunk=1024, row_chunk=row_chunk),
        grid=(b, s // tile),
        in_specs=[tok(D_MODEL), tok(ATT_DIM), tok(CONV_DIM)] + [per_b] * 6
                 + [_const_spec((1, ATT_DIM)), _const_spec((1, D_MODEL)), _const_spec((1, D_MODEL)),
                    _const_spec((D_MODEL, D_MODEL)), _const_spec((D_MODEL, D_FF)),
                    _const_spec((D_FF, D_MODEL))],
        out_specs=tok(D_MODEL),
        out_shape=jax.ShapeDtypeStruct((b, s, D_MODEL), F32),
        compiler_params=pltpu.CompilerParams(dimension_semantics=("arbitrary", "arbitrary"),
                                             vmem_limit_bytes=VMEM_LIMIT_BYTES),
        name="post",
    )(x, att, cn, *mods, g_att, g2, g_final, w_out, w_up, w_down)


def _placement():
    pq = np.zeros((LANES, N_HEADS * LANES), np.float32)
    pq_t = np.zeros((ATT_DIM, LANES), np.float32)
    for h in range(N_HEADS):
        base = h * LANES + (HEAD_DIM if h % 2 == 0 else 0)
        for part in range(N_SPLIT):
            src, own = part * N_HEADS + h, E_F0 + part * N_HEADS + h
            pq[src, base + part] = 1.0
            pq[ONES_ROW, base + own] = 1.0
            pq_t[h * HEAD_DIM + part, src] = 1.0
            pq_t[h * HEAD_DIM + own, ONES_ROW] = 1.0
    return jnp.asarray(pq, BF16), jnp.asarray(pq_t, BF16)


def _tri(n):
    return jnp.asarray(np.tri(n, dtype=np.float32), BF16)


def _pack_w_in(w):
    a, c = ATT_DIM, CONV_DIM
    f0 = 3 * a
    b0 = f0 + N_HEADS
    pad = jnp.zeros((D_MODEL, LANES - N_HEADS), w.dtype)
    wq = w[:, :a] * (ATT_SCALE * LOG2E)
    packed = jnp.concatenate([wq, w[:, a:f0], w[:, b0:b0 + 3 * c], w[:, f0:b0], pad], axis=1)
    transposed = jnp.concatenate([wq, w[:, f0:b0], pad[:, :BF16_ROWS - N_HEADS]], axis=1).T
    return packed.astype(BF16), transposed.astype(BF16)


def kernel(x_prompt, x_sample, cache_k, cache_v, cache_logf, cache_conv, c_prompt, c_sample, w_ada, b_ada, g_norm1, g_norm2, w_in, b_f, w_conv, g_attn_out, g_conv_out, w_out, w_up, w_down, w_ada_final, b_ada_final, g_final):
    assert w_ada.shape[0] == 1, "single layer"
    nb, s, _ = x_prompt.shape
    db, t, _ = x_sample.shape
    p = cache_k.shape[2]
    tile_prompt, tile_cache, tq, heads_per_step, tile_post = 512, 512, 512, 8, 512

    c_all = jnp.concatenate([c_prompt, c_sample], axis=0)
    mod = _ada(c_all, w_ada[0], b_ada[0])
    mod_f = _ada(c_all, w_ada_final, b_ada_final)
    sh1, sc1, gt1, sh2, sc2, gt2 = [m[:, None, :] for m in jnp.split(mod, 6, axis=-1)]
    shf, scf = [m[:, None, :] for m in jnp.split(mod_f, 2, axis=-1)]

    w_pack, w_t = _pack_w_in(w_in[0])
    bf_pad = jnp.pad(b_f[0], (0, LANES - N_HEADS)).reshape(1, LANES)
    bf_col = jnp.broadcast_to(b_f[0][:, None], (N_HEADS, LANES))
    pq, pq_t = _placement()
    g1 = g_norm1[0].reshape(1, D_MODEL)
    g2 = g_norm2[0].reshape(1, D_MODEL)
    gf = g_final.reshape(1, D_MODEL)
    g_att = g_attn_out[0].reshape(1, ATT_DIM)
    g_conv = g_conv_out[0].reshape(1, CONV_DIM)
    wo, wu, wd = w_out[0].astype(BF16), w_up[0].astype(BF16), w_down[0].astype(BF16)

    def layer(x, rows, prev, f0, tile, transposed_qv, attend):
        sel = lambda m: m[rows]
        k, v, logf, qa, ka, va, cn, cs = _proj(x, sel(sh1), sel(sc1), g1, w_pack, w_t, bf_pad, bf_col, w_conv[0],
                                               g_conv, prev, f0, _tri(tile), pq, pq_t, tile,
                                               transposed_qv)
        att = attend(qa, ka, va)
        mods = [sel(m) for m in (gt1, sh2, sc2, gt2, shf, scf)]
        bsz, sl = x.shape[0], x.shape[1]
        if sl >= tile_post:
            y = _post(x, att, cn, mods, g_att, g2, gf, wo, wu, wd, tile_post)
        else:
            flat = lambda a: a.reshape(1, bsz * sl, a.shape[-1])
            rows_of = lambda m: flat(jnp.broadcast_to(m, (bsz, sl, D_MODEL)))
            y = _post(flat(x), flat(att), flat(cn), [rows_of(m) for m in mods], g_att, g2, gf,
                      wo, wu, wd, bsz * sl).reshape(bsz, sl, D_MODEL)
        heads = lambda a: a.reshape(1, bsz, sl, N_HEADS, HEAD_DIM)
        if transposed_qv:
            logf = jnp.transpose(logf, (0, 2, 1))
        return y, heads(k), heads(v), logf[None], cs[None]

    zeros_prev = jnp.zeros((nb, CONV_WIDTH - 1, CONV_DIM), F32)
    zeros_f = jnp.zeros((nb, 1, LANES), F32)
    yp, kp, vp, lp, cp = layer(x_prompt, slice(0, nb), zeros_prev, zeros_f, tile_prompt, True,
                               lambda qt, ka, vt: _attn(qt, ka, vt, tq, heads_per_step))

    kt = jnp.transpose(cache_k[0], (0, 2, 3, 1))
    vt = jnp.transpose(cache_v[0], (0, 2, 3, 1))
    fc = _cache_cumsum(jnp.transpose(cache_logf[0], (0, 2, 1)), _tri(tile_cache).T, tile_cache)
    f_tot = jnp.pad(fc[:, :, p - 1], ((0, 0), (0, LANES - N_HEADS)))[:, None, :]
    ys, ks, vs, ls, cs = layer(x_sample, slice(nb, nb + db), cache_conv[0], f_tot, t, False,
                               lambda qa, ka, va: _attn_sample(qa, ka, va, kt, vt, fc))
    return (yp, ys, kp, vp, lp, cp, ks, vs, ls, cs)
```

```python
import functools

import jax
import jax.numpy as jnp
import numpy as np
from jax import lax
from jax.experimental import pallas as pl
from jax.experimental.pallas import tpu as pltpu

F32 = jnp.float32
BF16 = jnp.bfloat16

D_MODEL = 1024
N_HEADS = 8
HEAD_DIM = 64
ATT_DIM = N_HEADS * HEAD_DIM
CONV_DIM = 512
CONV_WIDTH = 3
D_FF = 4 * D_MODEL
NORM_EPS = 1e-6
ATT_SCALE = HEAD_DIM ** -0.5

LANES = 128
CARRY_ROWS = 8
N_SPLIT = 3
ONES_ROW = N_SPLIT * N_HEADS
VMEM_LIMIT_BYTES = 56 * 1024 * 1024
BF16_ROWS = 16
SHIFT_ROW = 8
E_F0 = 16
ROW_CHUNKS = 2
POST_ROW_CHUNK = 256
LOG2E = 1.4426950408889634

SEC_Q, SEC_K, SEC_V, SEC_BG, SEC_CG, SEC_U = range(6)
SEC_W = 512
F_COL = 6 * SEC_W
PROJ_COLS = F_COL + LANES


def _const_spec(shape):
    zeros = (0,) * len(shape)
    return pl.BlockSpec(shape, lambda *_: zeros, pipeline_mode=pl.Buffered(1))


def _rms(x):
    return x * lax.rsqrt(jnp.mean(x * x, axis=-1, keepdims=True) + NORM_EPS)


def _log_sigmoid(x):
    return jnp.minimum(x, 0.0) - jnp.log1p(jnp.exp(-jnp.abs(x)))


def _split3(x):
    hi = x.astype(BF16).astype(F32)
    r = x - hi
    mid = r.astype(BF16).astype(F32)
    lo = (r - mid).astype(BF16).astype(F32)
    return hi, mid, lo


def _cumsum_rows(l128, tri_ref, lane):
    hi, mid, lo = _split3(l128)
    cat = hi + pltpu.roll(mid, N_HEADS, axis=1) + pltpu.roll(lo, 2 * N_HEADS, axis=1)
    sums = jnp.dot(tri_ref[...], cat.astype(BF16), preferred_element_type=F32)
    total = ((sums + pltpu.roll(sums, LANES - N_HEADS, axis=1))
             + pltpu.roll(sums, LANES - 2 * N_HEADS, axis=1))
    return jnp.where(lane < N_HEADS, total, 0.0)


def _f_parts(f128):
    hi, mid, lo = _split3(f128 * LOG2E)
    return hi + pltpu.roll(mid, N_HEADS, axis=1) + pltpu.roll(lo, 2 * N_HEADS, axis=1)


def _fcat(parts, lane):
    return (parts + jnp.where(lane == ONES_ROW, 1.0, 0.0)).astype(BF16)


def _k_extras(parts, lane):
    ones = jnp.where((lane < N_SPLIT) | ((lane >= SHIFT_ROW) & (lane < SHIFT_ROW + N_SPLIT)), 1.0, 0.0)
    odd = ones - pltpu.roll(parts, E_F0, axis=1)
    return pltpu.roll(odd, HEAD_DIM, axis=1), odd


def _store_aug(dst_ref, main, extras_even, extras_odd, lane):
    low = lane < HEAD_DIM
    for pair in range(N_HEADS // 2):
        m = main[:, pair * LANES:(pair + 1) * LANES]
        dst_ref[0, 2 * pair] = jnp.where(low, m, extras_even(pair)).astype(BF16)
        dst_ref[0, 2 * pair + 1] = jnp.where(low, extras_odd(pair), m).astype(BF16)


def _store_v_aug(dst_ref, v, lane):
    even = jnp.where(lane == HEAD_DIM, 1.0, 0.0)
    odd = jnp.where(lane == 0, 1.0, 0.0)
    _store_aug(dst_ref, v, lambda pair: even, lambda pair: odd, lane)


def _ada_kernel(c_ref, w_ref, b_ref, o_ref):
    c = c_ref[...]
    sc = (c * jax.nn.sigmoid(c)).astype(BF16)
    o_ref[...] = jnp.dot(sc, w_ref[...].astype(BF16), preferred_element_type=F32) + b_ref[...]


def _ada(c, w, b):
    rows, n = c.shape[0], w.shape[1]
    bn = 1024
    return pl.pallas_call(
        _ada_kernel,
        grid=(n // bn,),
        in_specs=[pl.BlockSpec((rows, D_MODEL), lambda j: (0, 0)),
                  pl.BlockSpec((D_MODEL, bn), lambda j: (0, j)),
                  pl.BlockSpec((1, bn), lambda j: (0, j))],
        out_specs=pl.BlockSpec((rows, bn), lambda j: (0, j)),
        out_shape=jax.ShapeDtypeStruct((rows, n), F32),
        name="ada",
    )(c, w, b.reshape(1, n))


def _store_qv_transposed(qa_ref, va_ref, q_t, v_t, eq_t, tile):
    ones_row = jnp.where(lax.broadcasted_iota(jnp.int32, (HEAD_DIM, tile), 0) == 0, 1.0, 0.0)
    for h in range(N_HEADS):
        rows = slice(h * HEAD_DIM, (h + 1) * HEAD_DIM)
        parts = [q_t[rows], eq_t[rows]]
        qa_ref[0, h] = jnp.concatenate(parts if h % 2 == 0 else parts[::-1], axis=0).astype(BF16)
        va_ref[0, h] = jnp.concatenate([v_t[rows], ones_row], axis=0).astype(BF16)


def _proj_kernel(x_ref, sh_ref, sc_ref, g1_ref, w_ref, wt_ref, bf_ref, bfc_ref, wconv_ref, gconv_ref, prev_ref,
                 f0_ref, tri_ref, pq_ref, pqt_ref,
                 k_ref, v_ref, logf_ref, qa_ref, ka_ref, va_ref, cn_ref, cs_ref,
                 ue_ref, fc_ref, *, tile, transposed_qv):
    s = pl.program_id(1)

    @pl.when(s == 0)
    def _():
        ue_ref[0:CARRY_ROWS, :] = jnp.zeros((CARRY_ROWS, CONV_DIM), F32)
        ue_ref[CARRY_ROWS - (CONV_WIDTH - 1):CARRY_ROWS, :] = prev_ref[0]
        fc_ref[...] = f0_ref[0]

    lane = lax.broadcasted_iota(jnp.int32, (tile, LANES), 1)
    nt = (((1,), (1,)), ((), ()))
    gain = g1_ref[...] * (1.0 + sc_ref[0])
    shift = sh_ref[0]

    chunk = tile // ROW_CHUNKS if tile % (ROW_CHUNKS * BF16_ROWS) == 0 else tile
    starts = list(range(0, tile, chunk))

    def conv_matmuls(r0):
        hb_c = (_rms(x_ref[0, r0:r0 + chunk, :]) * gain + shift).astype(BF16)
        sec = lambda i: jnp.dot(hb_c, w_ref[:, i * SEC_W:(i + 1) * SEC_W], preferred_element_type=F32)
        return hb_c, sec(SEC_CG), sec(SEC_U), sec(SEC_BG)

    def conv_elementwise(r0, cg, u, bg):
        up = cg * u
        base = CARRY_ROWS + r0
        ue_ref[base:base + chunk, :] = up
        cv = (wconv_ref[0:1, :] * ue_ref[base - 2:base - 2 + chunk, :]
              + wconv_ref[1:2, :] * ue_ref[base - 1:base - 1 + chunk, :]
              + wconv_ref[2:3, :] * up)
        cn_ref[0, r0:r0 + chunk, :] = (_rms(bg * cv) * gconv_ref[...]).astype(BF16)

    hbs, pending = [], None
    for r0 in starts:
        hb_c, cg, u, bg = conv_matmuls(r0)
        hbs.append(hb_c)
        if pending is not None:
            conv_elementwise(*pending)
        pending = (r0, cg, u, bg)
    hb = jnp.concatenate(hbs, axis=0) if len(hbs) > 1 else hbs[0]

    def proj(sec, width=SEC_W):
        return jnp.dot(hb, w_ref[:, sec * SEC_W:sec * SEC_W + width], preferred_element_type=F32)

    if transposed_qv:
        qf_t = lax.dot_general(wt_ref[...], hb, nt, preferred_element_type=F32)
        q_t, fl_t = qf_t[:ATT_DIM], qf_t[ATT_DIM:]
        fl = jnp.concatenate([fl_t, jnp.zeros((LANES - BF16_ROWS, tile), F32)], axis=0).T
    else:
        fl = proj(6, LANES)
    k = proj(SEC_K)
    k_ref[0] = k
    conv_elementwise(*pending)
    ue_ref[0:CARRY_ROWS, :] = ue_ref[tile:tile + CARRY_ROWS, :]

    logf = jnp.where(lane < N_HEADS, _log_sigmoid(fl + bf_ref[...]), 0.0)
    if transposed_qv:
        logf_ref[0] = _log_sigmoid(fl_t[0:N_HEADS] + bfc_ref[:, 0:1])
    else:
        logf_ref[0] = logf[:, :N_HEADS]
    f128 = _cumsum_rows(logf, tri_ref, lane) + fc_ref[...]
    v = proj(SEC_V)
    v_ref[0] = v
    fc_ref[...] = f128[tile - 1:tile, :]
    parts = _f_parts(f128)
    fcat = _fcat(parts, lane)

    ke_even, ke_odd = _k_extras(parts, lane)
    _store_aug(ka_ref, k, lambda pair: ke_even, lambda pair: ke_odd, lane)
    if transposed_qv:
        eq_t = lax.dot_general(pqt_ref[...], fcat, nt, preferred_element_type=F32)
        _store_qv_transposed(qa_ref, va_ref, q_t, v.T, eq_t, tile)
    else:
        eq = jnp.dot(fcat, pq_ref[...], preferred_element_type=F32)
        _store_aug(qa_ref, proj(SEC_Q), lambda pair: eq[:, 2 * pair * LANES:(2 * pair + 1) * LANES],
                   lambda pair: eq[:, (2 * pair + 1) * LANES:(2 * pair + 2) * LANES], lane)
        _store_v_aug(va_ref, v, lane)

    @pl.when(s == pl.num_programs(1) - 1)
    def _():
        cs_ref[0] = ue_ref[CARRY_ROWS + tile - (CONV_WIDTH - 1):CARRY_ROWS + tile, :]


def _proj(x, sh, sc, g1, w_pack, w_t, bf_pad, bf_col, w_conv, g_conv, prev, f0, tri, pq, pq_t, tile,
          transposed_qv):
    b, s, _ = x.shape
    assert s % tile == 0 and tile % CARRY_ROWS == 0
    ns = s // tile
    tok = lambda n: pl.BlockSpec((1, tile, n), lambda i, j: (i, j, 0))
    per_b = lambda r, n: pl.BlockSpec((1, r, n), lambda i, j: (i, 0, 0))
    aug = pl.BlockSpec((1, N_HEADS, tile, LANES), lambda i, j: (i, 0, j, 0))
    aug_sds = jax.ShapeDtypeStruct((b, N_HEADS, s, LANES), BF16)
    if transposed_qv:
        qv = pl.BlockSpec((1, N_HEADS, LANES, tile), lambda i, j: (i, 0, 0, j))
        qv_sds = jax.ShapeDtypeStruct((b, N_HEADS, LANES, s), BF16)
        lf = pl.BlockSpec((1, N_HEADS, tile), lambda i, j: (i, 0, j))
        lf_sds = jax.ShapeDtypeStruct((b, N_HEADS, s), F32)
    else:
        qv, qv_sds = aug, aug_sds
        lf, lf_sds = tok(N_HEADS), jax.ShapeDtypeStruct((b, s, N_HEADS), F32)
    out_shape = (
        jax.ShapeDtypeStruct((b, s, ATT_DIM), F32),
        jax.ShapeDtypeStruct((b, s, ATT_DIM), F32),
        lf_sds,
        qv_sds,
        aug_sds,
        qv_sds,
        jax.ShapeDtypeStruct((b, s, CONV_DIM), BF16),
        jax.ShapeDtypeStruct((b, CONV_WIDTH - 1, CONV_DIM), F32),
    )
    return pl.pallas_call(
        functools.partial(_proj_kernel, tile=tile, transposed_qv=transposed_qv),
        grid=(b, ns),
        in_specs=[tok(D_MODEL), per_b(1, D_MODEL), per_b(1, D_MODEL), _const_spec((1, D_MODEL)),
                  _const_spec((D_MODEL, PROJ_COLS)), _const_spec((ATT_DIM + BF16_ROWS, D_MODEL)),
                  _const_spec((1, LANES)), _const_spec((N_HEADS, LANES)),
                  _const_spec((CONV_WIDTH, CONV_DIM)), _const_spec((1, CONV_DIM)),
                  per_b(CONV_WIDTH - 1, CONV_DIM), per_b(1, LANES),
                  _const_spec((tile, tile)), _const_spec((LANES, N_HEADS * LANES)),
                  _const_spec((ATT_DIM, LANES))],
        out_specs=(tok(ATT_DIM), tok(ATT_DIM), lf, qv, aug, qv, tok(CONV_DIM),
                   per_b(CONV_WIDTH - 1, CONV_DIM)),
        out_shape=out_shape,
        scratch_shapes=[pltpu.VMEM((tile + CARRY_ROWS, CONV_DIM), F32), pltpu.VMEM((1, LANES), F32)],
        compiler_params=pltpu.CompilerParams(dimension_semantics=("arbitrary", "arbitrary"),
                                             vmem_limit_bytes=VMEM_LIMIT_BYTES),
        name="proj",
    )(x, sh, sc, g1, w_pack, w_t, bf_pad, bf_col, w_conv, g_conv, prev, f0, tri, pq, pq_t)


def _cache_cumsum_kernel(l_ref, triu_ref, f_ref, *, tile):
    n = l_ref.shape[2] // tile
    rows = []
    for c in range(n):
        rows.extend(_split3(l_ref[0, :, c * tile:(c + 1) * tile]))
    sums = jnp.dot(jnp.concatenate(rows, axis=0).astype(BF16), triu_ref[...],
                   preferred_element_type=F32)
    carry = jnp.zeros((N_HEADS, 1), F32)
    for c in range(n):
        hi, mid, lo = [sums[(N_SPLIT * c + i) * N_HEADS:(N_SPLIT * c + i + 1) * N_HEADS]
                       for i in range(N_SPLIT)]
        f = ((hi + mid) + lo) + carry
        f_ref[0, :, c * tile:(c + 1) * tile] = f
        carry = f[:, tile - 1:tile]


def _cache_cumsum(cl_t, triu, tile):
    b, h, p = cl_t.shape
    blk = pl.BlockSpec((1, h, p), lambda i: (i, 0, 0))
    return pl.pallas_call(
        functools.partial(_cache_cumsum_kernel, tile=tile),
        grid=(b,),
        in_specs=[blk, _const_spec((tile, tile))],
        out_specs=blk,
        out_shape=jax.ShapeDtypeStruct((b, h, p), F32),
        compiler_params=pltpu.CompilerParams(dimension_semantics=("arbitrary",)),
        name="cache_cumsum",
    )(cl_t, triu)


def _softmax_block_t(qt, k, vt, m, acc, mask):
    s = jnp.dot(k, qt, preferred_element_type=F32)
    if mask is not None:
        s = jnp.where(mask, s, -jnp.inf)
    m_new = jnp.maximum(m, jnp.max(s, axis=0, keepdims=True))
    p = jnp.exp2(s - m_new)
    acc = acc * jnp.exp2(m - m_new) + jnp.dot(vt, p.astype(BF16), preferred_element_type=F32)
    return m_new, acc


def _head_online(qt, kv_block, qi, causal, tq):
    def body(j, carry):
        k, vt = kv_block(j)
        return _softmax_block_t(qt, k, vt, carry[0], carry[1], None)

    init = (jnp.full((1, tq), -jnp.inf, F32), jnp.zeros((LANES, tq), F32))
    m, acc = lax.fori_loop(0, qi, body, init)
    k, vt = kv_block(qi)
    return _softmax_block_t(qt, k, vt, m, acc, causal)[1]


def _with_shift_rows(qt, m, parity):
    g0 = HEAD_DIM if parity == 0 else 0
    hi, mid, lo = _split3(-m)
    row = lax.broadcasted_iota(jnp.int32, (BF16_ROWS, m.shape[1]), 0)
    add = jnp.where(row == SHIFT_ROW, hi,
                    jnp.where(row == SHIFT_ROW + 1, mid, jnp.where(row == SHIFT_ROW + 2, lo, 0.0)))
    grp = (qt[g0:g0 + BF16_ROWS].astype(F32) + add).astype(BF16)
    pieces = ([qt[:g0]] if g0 else []) + [grp, qt[g0 + BF16_ROWS:]]
    return jnp.concatenate(pieces, axis=0)


def _heads_fixed_max(qts, kv_block, qi, causal, acc_ref, qs_ref):
    n = len(qts)
    dot = functools.partial(jnp.dot, preferred_element_type=F32)
    half = causal.shape[0]

    def diag(hh):
        k, qt = kv_block(hh, qi)[0], qts[hh]
        s_a = dot(k[:half], qt)
        s_a = jnp.concatenate([jnp.where(causal, s_a[:, :half], -jnp.inf), s_a[:, half:]], axis=1)
        return s_a, jnp.where(causal, dot(k[half:], qt[:, half:]), -jnp.inf)

    s_next = diag(0)
    for hh in range(n):
        (s_a, s_b), s_next = s_next, (diag(hh + 1) if hh + 1 < n else None)
        m_a = jnp.max(s_a, axis=0, keepdims=True)
        m_late = jnp.maximum(m_a[:, half:], jnp.max(s_b, axis=0, keepdims=True))
        m = jnp.concatenate([m_a[:, :half], m_late], axis=1)
        qs_ref[hh] = _with_shift_rows(qts[hh], m, hh % 2)
        vt = kv_block(hh, qi)[1]
        acc = dot(vt[:, :half], jnp.exp2(s_a - m).astype(BF16))
        late = acc[:, half:] + dot(vt[:, half:], jnp.exp2(s_b - m_late).astype(BF16))
        acc_ref[hh] = jnp.concatenate([acc[:, :half], late], axis=1)

    def run(blocks):
        items = [(j, hh) for j in blocks for hh in range(n)]
        scores = lambda item: dot(kv_block(item[1], item[0])[0], qs_ref[item[1]])
        s_next = scores(items[0])
        for idx, (j, hh) in enumerate(items):
            s, s_next = s_next, (scores(items[idx + 1]) if idx + 1 < len(items) else None)
            acc_ref[hh] += dot(kv_block(hh, j)[1], jnp.exp2(s).astype(BF16))

    def body(i, carry):
        run([2 * i, 2 * i + 1])
        return carry

    lax.fori_loop(0, qi // 2, body, 0)

    @pl.when(qi % 2 == 1)
    def _():
        run([qi - 1])

    return [acc_ref[hh] for hh in range(n)]


def _attn_kernel(qt_ref, k_ref, vt_ref, o_ref, acc_ref, qs_ref, *, tq):
    qi = pl.program_id(2)

    def causal_mask(n):
        return (lax.broadcasted_iota(jnp.int32, (n, n), 0)
                <= lax.broadcasted_iota(jnp.int32, (n, n), 1))

    def kv_block(hh, j):
        off = pl.multiple_of(j * tq, tq)
        return k_ref[0, hh, pl.ds(off, tq), :], vt_ref[0, hh, :, pl.ds(off, tq)]

    heads = qt_ref.shape[1]
    qts = [qt_ref[0, hh] for hh in range(heads)]

    def finish(accs):
        norm = lambda acc: acc[0:HEAD_DIM] * (1.0 / acc[HEAD_DIM:HEAD_DIM + 1])
        finite = None
        for pair in range(heads // 2):
            both = jnp.concatenate([norm(accs[2 * pair]), norm(accs[2 * pair + 1])], axis=0)
            o_ref[0, :, pair * LANES:(pair + 1) * LANES] = both.T
            ok = jnp.where(jnp.isfinite(both), 1.0, 0.0)
            finite = ok if finite is None else jnp.minimum(finite, ok)
        return jnp.min(finite) > 0.5

    all_finite = finish(_heads_fixed_max(qts, kv_block, qi, causal_mask(tq // 2), acc_ref, qs_ref))

    @pl.when(jnp.logical_not(all_finite))
    def _():
        finish([_head_online(qts[hh], functools.partial(kv_block, hh), qi, causal_mask(tq), tq)
                for hh in range(heads)])


def _attn(qt, ka, vt, tq, heads):
    b, _, s, _ = ka.shape
    assert heads % 2 == 0 and N_HEADS % heads == 0
    return pl.pallas_call(
        functools.partial(_attn_kernel, tq=tq),
        grid=(b, N_HEADS // heads, s // tq),
        in_specs=[pl.BlockSpec((1, heads, LANES, tq), lambda i, p, j: (i, p, 0, j)),
                  pl.BlockSpec((1, heads, s, LANES), lambda i, p, j: (i, p, 0, 0)),
                  pl.BlockSpec((1, heads, LANES, s), lambda i, p, j: (i, p, 0, 0))],
        out_specs=pl.BlockSpec((1, tq, heads * HEAD_DIM), lambda i, p, j: (i, j, p)),
        out_shape=jax.ShapeDtypeStruct((b, s, ATT_DIM), F32),
        scratch_shapes=[pltpu.VMEM((heads, LANES, tq), F32), pltpu.VMEM((heads, LANES, tq), BF16)],
        compiler_params=pltpu.CompilerParams(
            dimension_semantics=("arbitrary", "arbitrary", "arbitrary"),
            vmem_limit_bytes=VMEM_LIMIT_BYTES),
        name="attn",
    )(qt, ka, vt)


def _attn_sample_kernel(q_ref, kn_ref, vn_ref, kt_ref, vt_ref, fc_ref, o_ref, *, t):
    nt = (((1,), (1,)), ((), ()))
    causal = (lax.broadcasted_iota(jnp.int32, (t, LANES), 1)
              <= lax.broadcasted_iota(jnp.int32, (t, LANES), 0))
    pad = jnp.zeros((LANES - t, LANES), BF16)
    fk = fc_ref[0] * LOG2E
    outs = []
    for h in range(N_HEADS):
        head, extras = (0, HEAD_DIM) if h % 2 == 0 else (HEAD_DIM, 0)
        tile = q_ref[0, h]
        ex = tile[:, extras:extras + N_SPLIT].astype(F32)
        fq = (ex[:, 0:1] + ex[:, 1:2]) + ex[:, 2:3]
        s_old = jnp.dot(tile[:, head:head + HEAD_DIM], kt_ref[0, h].astype(BF16),
                        preferred_element_type=F32) + (fq - fk[h:h + 1, :])
        kn = jnp.concatenate([kn_ref[0, h], pad], axis=0)
        vn = jnp.concatenate([vn_ref[0, h], pad], axis=0)
        s_new = jnp.where(causal, lax.dot_general(tile, kn, nt, preferred_element_type=F32), -jnp.inf)
        m = jnp.maximum(jnp.max(s_old, axis=1, keepdims=True), jnp.max(s_new, axis=1, keepdims=True))
        p_old = jnp.exp2(s_old - m)
        acc_new = jnp.dot(jnp.exp2(s_new - m).astype(BF16), vn, preferred_element_type=F32)
        o_old = lax.dot_general(p_old.astype(BF16), vt_ref[0, h].astype(BF16), nt,
                                preferred_element_type=F32)
        denom = jnp.sum(p_old, axis=1, keepdims=True) + acc_new[:, extras:extras + 1]
        outs.append((o_old + acc_new[:, head:head + HEAD_DIM]) * (1.0 / denom))
    o_ref[0] = jnp.concatenate(outs, axis=1)


def _attn_sample(qa, kn, vn, kt, vt, fc):
    b, _, t, _ = qa.shape
    p = kt.shape[3]
    new = pl.BlockSpec((1, N_HEADS, t, LANES), lambda i: (i, 0, 0, 0))
    old = pl.BlockSpec((1, N_HEADS, HEAD_DIM, p), lambda i: (i, 0, 0, 0))
    return pl.pallas_call(
        functools.partial(_attn_sample_kernel, t=t),
        grid=(b,),
        in_specs=[new, new, new, old, old, pl.BlockSpec((1, N_HEADS, p), lambda i: (i, 0, 0))],
        out_specs=pl.BlockSpec((1, t, ATT_DIM), lambda i: (i, 0, 0)),
        out_shape=jax.ShapeDtypeStruct((b, t, ATT_DIM), F32),
        compiler_params=pltpu.CompilerParams(dimension_semantics=("arbitrary",),
                                             vmem_limit_bytes=VMEM_LIMIT_BYTES),
        name="attn_sample",
    )(qa, kn, vn, kt, vt, fc)


def _post_kernel(x_ref, att_ref, cn_ref, gt1_ref, sh2_ref, sc2_ref, gt2_ref, shf_ref, scf_ref,
                 gatt_ref, g2_ref, gf_ref, wo_ref, wu_ref, wd_ref, y_ref, *, ff_chunk, row_chunk):
    tile = x_ref.shape[1]
    chunks = [slice(r0, r0 + row_chunk) for r0 in range(0, tile, row_chunk)]
    gain2 = g2_ref[...] * (1.0 + sc2_ref[0])
    gain_f = gf_ref[...] * (1.0 + scf_ref[0])

    def normed_att(rows):
        return (_rms(att_ref[0, rows, :]) * gatt_ref[...]).astype(BF16)

    def out_proj(rows, xa):
        return (jnp.dot(xa, wo_ref[0:ATT_DIM, :], preferred_element_type=F32)
                + jnp.dot(cn_ref[0, rows, :], wo_ref[ATT_DIM:, :], preferred_element_type=F32))

    def residual1(rows, mixed):
        x1 = x_ref[0, rows, :] + gt1_ref[0] * mixed
        return x1, (_rms(x1) * gain2 + sh2_ref[0]).astype(BF16)

    def mlp(h2):
        total = None
        for c in range(D_FF // ff_chunk):
            up = jnp.dot(h2, wu_ref[:, c * ff_chunk:(c + 1) * ff_chunk], preferred_element_type=F32)
            act = jnp.square(jnp.maximum(up, 0.0)).astype(BF16)
            part = jnp.dot(act, wd_ref[c * ff_chunk:(c + 1) * ff_chunk, :], preferred_element_type=F32)
            total = part if total is None else total + part
        return total

    def finish(rows, x1, m):
        x2 = x1 + gt2_ref[0] * m
        y_ref[0, rows, :] = _rms(x2) * gain_f + shf_ref[0]

    n = len(chunks)
    xa = normed_att(chunks[0])
    mixed = []
    for c in range(n):
        mixed.append(out_proj(chunks[c], xa))
        if c + 1 < n:
            xa = normed_att(chunks[c + 1])
    x1, h2 = residual1(chunks[0], mixed[0])
    pending = None
    for c in range(n):
        m = mlp(h2)
        done = (chunks[c], x1, m)
        if c + 1 < n:
            x1, h2 = residual1(chunks[c + 1], mixed[c + 1])
        if pending is not None:
            finish(*pending)
        pending = done
    finish(*pending)


def _post(x, att, cn, mods, g_att, g2, g_final, w_out, w_up, w_down, tile):
    b, s, _ = x.shape
    row_chunk = POST_ROW_CHUNK if tile % POST_ROW_CHUNK == 0 else tile
    tok = lambda n: pl.BlockSpec((1, tile, n), lambda i, j: (i, j, 0))
    if mods[0].shape[1] == 1:
        per_b = pl.BlockSpec((1, 1, D_MODEL), lambda i, j: (i, 0, 0))
    else:
        assert row_chunk == tile
        per_b = tok(D_MODEL)
    return pl.pallas_call(
        functools.partial(_post_kernel, ff_chunk=1024, row_chunk=row_chunk),
        grid=(b, s // tile),
        in_specs=[tok(D_MODEL), tok(ATT_DIM), tok(CONV_DIM)] + [per_b] * 6
                 + [_const_spec((1, ATT_DIM)), _const_spec((1, D_MODEL)), _const_spec((1, D_MODEL)),
                    _const_spec((D_MODEL, D_MODEL)), _const_spec((D_MODEL, D_FF)),
                    _const_spec((D_FF, D_MODEL))],
        out_specs=tok(D_MODEL),
        out_shape=jax.ShapeDtypeStruct((b, s, D_MODEL), F32),
        compiler_params=pltpu.CompilerParams(dimension_semantics=("arbitrary", "arbitrary"),
                                             vmem_limit_bytes=VMEM_LIMIT_BYTES),
        name="post",
    )(x, att, cn, *mods, g_att, g2, g_final, w_out, w_up, w_down)


def _placement():
    pq = np.zeros((LANES, N_HEADS * LANES), np.float32)
    pq_t = np.zeros((ATT_DIM, LANES), np.float32)
    for h in range(N_HEADS):
        base = h * LANES + (HEAD_DIM if h % 2 == 0 else 0)
        for part in range(N_SPLIT):
            src, own = part * N_HEADS + h, E_F0 + part * N_HEADS + h
            pq[src, base + part] = 1.0
            pq[ONES_ROW, base + own] = 1.0
            pq_t[h * HEAD_DIM + part, src] = 1.0
            pq_t[h * HEAD_DIM + own, ONES_ROW] = 1.0
    return jnp.asarray(pq, BF16), jnp.asarray(pq_t, BF16)


def _tri(n):
    return jnp.asarray(np.tri(n, dtype=np.float32), BF16)


def _pack_w_in(w):
    a, c = ATT_DIM, CONV_DIM
    f0 = 3 * a
    b0 = f0 + N_HEADS
    pad = jnp.zeros((D_MODEL, LANES - N_HEADS), w.dtype)
    wq = w[:, :a] * (ATT_SCALE * LOG2E)
    packed = jnp.concatenate([wq, w[:, a:f0], w[:, b0:b0 + 3 * c], w[:, f0:b0], pad], axis=1)
    transposed = jnp.concatenate([wq, w[:, f0:b0], pad[:, :BF16_ROWS - N_HEADS]], axis=1).T
    return packed.astype(BF16), transposed.astype(BF16)


def kernel(x_prompt, x_sample, cache_k, cache_v, cache_logf, cache_conv, c_prompt, c_sample, w_ada, b_ada, g_norm1, g_norm2, w_in, b_f, w_conv, g_attn_out, g_conv_out, w_out, w_up, w_down, w_ada_final, b_ada_final, g_final):
    assert w_ada.shape[0] == 1, "single layer"
    nb, s, _ = x_prompt.shape
    db, t, _ = x_sample.shape
    p = cache_k.shape[2]
    tile_prompt, tile_cache, tq, heads_per_step, tile_post = 512, 512, 512, 8, 512

    c_all = jnp.concatenate([c_prompt, c_sample], axis=0)
    mod = _ada(c_all, w_ada[0], b_ada[0])
    mod_f = _ada(c_all, w_ada_final, b_ada_final)
    sh1, sc1, gt1, sh2, sc2, gt2 = [m[:, None, :] for m in jnp.split(mod, 6, axis=-1)]
    shf, scf = [m[:, None, :] for m in jnp.split(mod_f, 2, axis=-1)]

    w_pack, w_t = _pack_w_in(w_in[0])
    bf_pad = jnp.pad(b_f[0], (0, LANES - N_HEADS)).reshape(1, LANES)
    bf_col = jnp.broadcast_to(b_f[0][:, None], (N_HEADS, LANES))
    pq, pq_t = _placement()
    g1 = g_norm1[0].reshape(1, D_MODEL)
    g2 = g_norm2[0].reshape(1, D_MODEL)
    gf = g_final.reshape(1, D_MODEL)
    g_att = g_attn_out[0].reshape(1, ATT_DIM)
    g_conv = g_conv_out[0].reshape(1, CONV_DIM)
    wo, wu, wd = w_out[0].astype(BF16), w_up[0].astype(BF16), w_down[0].astype(BF16)

    def layer(x, rows, prev, f0, tile, transposed_qv, attend):
        sel = lambda m: m[rows]
        k, v, logf, qa, ka, va, cn, cs = _proj(x, sel(sh1), sel(sc1), g1, w_pack, w_t, bf_pad, bf_col, w_conv[0],
                                               g_conv, prev, f0, _tri(tile), pq, pq_t, tile,
                                               transposed_qv)
        att = attend(qa, ka, va)
        mods = [sel(m) for m in (gt1, sh2, sc2, gt2, shf, scf)]
        bsz, sl = x.shape[0], x.shape[1]
        if sl >= tile_post:
            y = _post(x, att, cn, mods, g_att, g2, gf, wo, wu, wd, tile_post)
        else:
            flat = lambda a: a.reshape(1, bsz * sl, a.shape[-1])
            rows_of = lambda m: flat(jnp.broadcast_to(m, (bsz, sl, D_MODEL)))
            y = _post(flat(x), flat(att), flat(cn), [rows_of(m) for m in mods], g_att, g2, gf,
                      wo, wu, wd, bsz * sl).reshape(bsz, sl, D_MODEL)
        heads = lambda a: a.reshape(1, bsz, sl, N_HEADS, HEAD_DIM)
        if transposed_qv:
            logf = jnp.transpose(logf, (0, 2, 1))
        return y, heads(k), heads(v), logf[None], cs[None]

    zeros_prev = jnp.zeros((nb, CONV_WIDTH - 1, CONV_DIM), F32)
    zeros_f = jnp.zeros((nb, 1, LANES), F32)
    yp, kp, vp, lp, cp = layer(x_prompt, slice(0, nb), zeros_prev, zeros_f, tile_prompt, True,
                               lambda qt, ka, vt: _attn(qt, ka, vt, tq, heads_per_step))

    kt = jnp.transpose(cache_k[0], (0, 2, 3, 1))
    vt = jnp.transpose(cache_v[0], (0, 2, 3, 1))
    fc = _cache_cumsum(jnp.transpose(cache_logf[0], (0, 2, 1)), _tri(tile_cache).T, tile_cache)
    f_tot = jnp.pad(fc[:, :, p - 1], ((0, 0), (0, LANES - N_HEADS)))[:, None, :]
    ys, ks, vs, ls, cs = layer(x_sample, slice(nb, nb + db), cache_conv[0], f_tot, t, False,
                               lambda qa, ka, va: _attn_sample(qa, ka, va, kt, vt, fc))
    return (yp, ys, kp, vp, lp, cp, ks, vs, ls, cs)
```

```python
import functools

import jax
import jax.numpy as jnp
import numpy as np
from jax import lax
from jax.experimental import pallas as pl
from jax.experimental.pallas import tpu as pltpu

F32 = jnp.float32
BF16 = jnp.bfloat16

D_MODEL = 1024
N_HEADS = 8
HEAD_DIM = 64
ATT_DIM = N_HEADS * HEAD_DIM
CONV_DIM = 512
CONV_WIDTH = 3
D_FF = 4 * D_MODEL
NORM_EPS = 1e-6
ATT_SCALE = HEAD_DIM ** -0.5

LANES = 128
CARRY_ROWS = 8
N_SPLIT = 3
ONES_ROW = N_SPLIT * N_HEADS
VMEM_LIMIT_BYTES = 56 * 1024 * 1024
BF16_ROWS = 16
SHIFT_ROW = 8
E_F0 = 16
ROW_CHUNKS = 2
POST_ROW_CHUNK = 256
LOG2E = 1.4426950408889634

SEC_Q, SEC_K, SEC_V, SEC_BG, SEC_CG, SEC_U = range(6)
SEC_W = 512
F_COL = 6 * SEC_W
PROJ_COLS = F_COL + LANES


def _const_spec(shape):
    zeros = (0,) * len(shape)
    return pl.BlockSpec(shape, lambda *_: zeros, pipeline_mode=pl.Buffered(1))


def _rms(x):
    return x * lax.rsqrt(jnp.mean(x * x, axis=-1, keepdims=True) + NORM_EPS)


def _log_sigmoid(x):
    return jnp.minimum(x, 0.0) - jnp.log1p(jnp.exp(-jnp.abs(x)))


def _split3(x):
    hi = x.astype(BF16).astype(F32)
    r = x - hi
    mid = r.astype(BF16).astype(F32)
    lo = (r - mid).astype(BF16).astype(F32)
    return hi, mid, lo


def _cumsum_rows(l128, tri_ref, lane):
    hi, mid, lo = _split3(l128)
    cat = hi + pltpu.roll(mid, N_HEADS, axis=1) + pltpu.roll(lo, 2 * N_HEADS, axis=1)
    sums = jnp.dot(tri_ref[...], cat.astype(BF16), preferred_element_type=F32)
    total = ((sums + pltpu.roll(sums, LANES - N_HEADS, axis=1))
             + pltpu.roll(sums, LANES - 2 * N_HEADS, axis=1))
    return jnp.where(lane < N_HEADS, total, 0.0)


def _f_parts(f128):
    hi, mid, lo = _split3(f128 * LOG2E)
    return hi + pltpu.roll(mid, N_HEADS, axis=1) + pltpu.roll(lo, 2 * N_HEADS, axis=1)


def _fcat(parts, lane):
    return (parts + jnp.where(lane == ONES_ROW, 1.0, 0.0)).astype(BF16)


def _k_extras(parts, lane):
    ones = jnp.where((lane < N_SPLIT) | ((lane >= SHIFT_ROW) & (lane < SHIFT_ROW + N_SPLIT)), 1.0, 0.0)
    odd = ones - pltpu.roll(parts, E_F0, axis=1)
    return pltpu.roll(odd, HEAD_DIM, axis=1), odd


def _store_aug(dst_ref, main, extras_even, extras_odd, lane):
    low = lane < HEAD_DIM
    for pair in range(N_HEADS // 2):
        m = main[:, pair * LANES:(pair + 1) * LANES]
        dst_ref[0, 2 * pair] = jnp.where(low, m, extras_even(pair)).astype(BF16)
        dst_ref[0, 2 * pair + 1] = jnp.where(low, extras_odd(pair), m).astype(BF16)


def _store_v_aug(dst_ref, v, lane):
    even = jnp.where(lane == HEAD_DIM, 1.0, 0.0)
    odd = jnp.where(lane == 0, 1.0, 0.0)
    _store_aug(dst_ref, v, lambda pair: even, lambda pair: odd, lane)


def _ada_kernel(c_ref, w_ref, b_ref, o_ref):
    c = c_ref[...]
    sc = (c * jax.nn.sigmoid(c)).astype(BF16)
    o_ref[...] = jnp.dot(sc, w_ref[...].astype(BF16), preferred_element_type=F32) + b_ref[...]


def _ada(c, w, b):
    rows, n = c.shape[0], w.shape[1]
    bn = 1024
    return pl.pallas_call(
        _ada_kernel,
        grid=(n // bn,),
        in_specs=[pl.BlockSpec((rows, D_MODEL), lambda j: (0, 0)),
                  pl.BlockSpec((D_MODEL, bn), lambda j: (0, j)),
                  pl.BlockSpec((1, bn), lambda j: (0, j))],
        out_specs=pl.BlockSpec((rows, bn), lambda j: (0, j)),
        out_shape=jax.ShapeDtypeStruct((rows, n), F32),
        name="ada",
    )(c, w, b.reshape(1, n))


def _store_qv_transposed(qa_ref, va_ref, q_t, v_t, eq_t, tile):
    ones_row = jnp.where(lax.broadcasted_iota(jnp.int32, (HEAD_DIM, tile), 0) == 0, 1.0, 0.0)
    for h in range(N_HEADS):
        rows = slice(h * HEAD_DIM, (h + 1) * HEAD_DIM)
        parts = [q_t[rows], eq_t[rows]]
        qa_ref[0, h] = jnp.concatenate(parts if h % 2 == 0 else parts[::-1], axis=0).astype(BF16)
        va_ref[0, h] = jnp.concatenate([v_t[rows], ones_row], axis=0).astype(BF16)


def _proj_kernel(x_ref, sh_ref, sc_ref, g1_ref, w_ref, wt_ref, bf_ref, bfc_ref, wconv_ref, gconv_ref, prev_ref,
                 f0_ref, tri_ref, pq_ref, pqt_ref,
                 k_ref, v_ref, logf_ref, qa_ref, ka_ref, va_ref, cn_ref, cs_ref,
                 ue_ref, fc_ref, *, tile, transposed_qv):
    s = pl.program_id(1)

    @pl.when(s == 0)
    def _():
        ue_ref[0:CARRY_ROWS, :] = jnp.zeros((CARRY_ROWS, CONV_DIM), F32)
        ue_ref[CARRY_ROWS - (CONV_WIDTH - 1):CARRY_ROWS, :] = prev_ref[0]
        fc_ref[...] = f0_ref[0]

    lane = lax.broadcasted_iota(jnp.int32, (tile, LANES), 1)
    nt = (((1,), (1,)), ((), ()))
    gain = g1_ref[...] * (1.0 + sc_ref[0])
    shift = sh_ref[0]

    chunk = tile // ROW_CHUNKS if tile % (ROW_CHUNKS * BF16_ROWS) == 0 else tile
    starts = list(range(0, tile, chunk))

    def conv_matmuls(r0):
        hb_c = (_rms(x_ref[0, r0:r0 + chunk, :]) * gain + shift).astype(BF16)
        sec = lambda i: jnp.dot(hb_c, w_ref[:, i * SEC_W:(i + 1) * SEC_W], preferred_element_type=F32)
        return hb_c, sec(SEC_CG), sec(SEC_U), sec(SEC_BG)

    def conv_elementwise(r0, cg, u, bg):
        up = cg * u
        base = CARRY_ROWS + r0
        ue_ref[base:base + chunk, :] = up
        cv = (wconv_ref[0:1, :] * ue_ref[base - 2:base - 2 + chunk, :]
              + wconv_ref[1:2, :] * ue_ref[base - 1:base - 1 + chunk, :]
              + wconv_ref[2:3, :] * up)
        cn_ref[0, r0:r0 + chunk, :] = (_rms(bg * cv) * gconv_ref[...]).astype(BF16)

    hbs, pending = [], None
    for r0 in starts:
        hb_c, cg, u, bg = conv_matmuls(r0)
        hbs.append(hb_c)
        if pending is not None:
            conv_elementwise(*pending)
        pending = (r0, cg, u, bg)
    hb = jnp.concatenate(hbs, axis=0) if len(hbs) > 1 else hbs[0]

    def proj(sec, width=SEC_W):
        return jnp.dot(hb, w_ref[:, sec * SEC_W:sec * SEC_W + width], preferred_element_type=F32)

    if transposed_qv:
        qf_t = lax.dot_general(wt_ref[...], hb, nt, preferred_element_type=F32)
        q_t, fl_t = qf_t[:ATT_DIM], qf_t[ATT_DIM:]
        fl = jnp.concatenate([fl_t, jnp.zeros((LANES - BF16_ROWS, tile), F32)], axis=0).T
    else:
        fl = proj(6, LANES)
    k = proj(SEC_K)
    k_ref[0] = k
    conv_elementwise(*pending)
    ue_ref[0:CARRY_ROWS, :] = ue_ref[tile:tile + CARRY_ROWS, :]

    logf = jnp.where(lane < N_HEADS, _log_sigmoid(fl + bf_ref[...]), 0.0)
    if transposed_qv:
        logf_ref[0] = _log_sigmoid(fl_t[0:N_HEADS] + bfc_ref[:, 0:1])
    else:
        logf_ref[0] = logf[:, :N_HEADS]
    f128 = _cumsum_rows(logf, tri_ref, lane) + fc_ref[...]
    v = proj(SEC_V)
    v_ref[0] = v
    fc_ref[...] = f128[tile - 1:tile, :]
    parts = _f_parts(f128)
    fcat = _fcat(parts, lane)

    ke_even, ke_odd = _k_extras(parts, lane)
    _store_aug(ka_ref, k, lambda pair: ke_even, lambda pair: ke_odd, lane)
    if transposed_qv:
        eq_t = lax.dot_general(pqt_ref[...], fcat, nt, preferred_element_type=F32)
        _store_qv_transposed(qa_ref, va_ref, q_t, v.T, eq_t, tile)
    else:
        eq = jnp.dot(fcat, pq_ref[...], preferred_element_type=F32)
        _store_aug(qa_ref, proj(SEC_Q), lambda pair: eq[:, 2 * pair * LANES:(2 * pair + 1) * LANES],
                   lambda pair: eq[:, (2 * pair + 1) * LANES:(2 * pair + 2) * LANES], lane)
        _store_v_aug(va_ref, v, lane)

    @pl.when(s == pl.num_programs(1) - 1)
    def _():
        cs_ref[0] = ue_ref[CARRY_ROWS + tile - (CONV_WIDTH - 1):CARRY_ROWS + tile, :]


def _proj(x, sh, sc, g1, w_pack, w_t, bf_pad, bf_col, w_conv, g_conv, prev, f0, tri, pq, pq_t, tile,
          transposed_qv):
    b, s, _ = x.shape
    assert s % tile == 0 and tile % CARRY_ROWS == 0
    ns = s // tile
    tok = lambda n: pl.BlockSpec((1, tile, n), lambda i, j: (i, j, 0))
    per_b = lambda r, n: pl.BlockSpec((1, r, n), lambda i, j: (i, 0, 0))
    aug = pl.BlockSpec((1, N_HEADS, tile, LANES), lambda i, j: (i, 0, j, 0))
    aug_sds = jax.ShapeDtypeStruct((b, N_HEADS, s, LANES), BF16)
    if transposed_qv:
        qv = pl.BlockSpec((1, N_HEADS, LANES, tile), lambda i, j: (i, 0, 0, j))
        qv_sds = jax.ShapeDtypeStruct((b, N_HEADS, LANES, s), BF16)
        lf = pl.BlockSpec((1, N_HEADS, tile), lambda i, j: (i, 0, j))
        lf_sds = jax.ShapeDtypeStruct((b, N_HEADS, s), F32)
    else:
        qv, qv_sds = aug, aug_sds
        lf, lf_sds = tok(N_HEADS), jax.ShapeDtypeStruct((b, s, N_HEADS), F32)
    out_shape = (
        jax.ShapeDtypeStruct((b, s, ATT_DIM), F32),
        jax.ShapeDtypeStruct((b, s, ATT_DIM), F32),
        lf_sds,
        qv_sds,
        aug_sds,
        qv_sds,
        jax.ShapeDtypeStruct((b, s, CONV_DIM), BF16),
        jax.ShapeDtypeStruct((b, CONV_WIDTH - 1, CONV_DIM), F32),
    )
    return pl.pallas_call(
        functools.partial(_proj_kernel, tile=tile, transposed_qv=transposed_qv),
        grid=(b, ns),
        in_specs=[tok(D_MODEL), per_b(1, D_MODEL), per_b(1, D_MODEL), _const_spec((1, D_MODEL)),
                  _const_spec((D_MODEL, PROJ_COLS)), _const_spec((ATT_DIM + BF16_ROWS, D_MODEL)),
                  _const_spec((1, LANES)), _const_spec((N_HEADS, LANES)),
                  _const_spec((CONV_WIDTH, CONV_DIM)), _const_spec((1, CONV_DIM)),
                  per_b(CONV_WIDTH - 1, CONV_DIM), per_b(1, LANES),
                  _const_spec((tile, tile)), _const_spec((LANES, N_HEADS * LANES)),
                  _const_spec((ATT_DIM, LANES))],
        out_specs=(tok(ATT_DIM), tok(ATT_DIM), lf, qv, aug, qv, tok(CONV_DIM),
                   per_b(CONV_WIDTH - 1, CONV_DIM)),
        out_shape=out_shape,
        scratch_shapes=[pltpu.VMEM((tile + CARRY_ROWS, CONV_DIM), F32), pltpu.VMEM((1, LANES), F32)],
        compiler_params=pltpu.CompilerParams(dimension_semantics=("arbitrary", "arbitrary"),
                                             vmem_limit_bytes=VMEM_LIMIT_BYTES),
        name="proj",
    )(x, sh, sc, g1, w_pack, w_t, bf_pad, bf_col, w_conv, g_conv, prev, f0, tri, pq, pq_t)


def _cache_cumsum_kernel(l_ref, triu_ref, f_ref, *, tile):
    n = l_ref.shape[2] // tile
    rows = []
    for c in range(n):
        rows.extend(_split3(l_ref[0, :, c * tile:(c + 1) * tile]))
    sums = jnp.dot(jnp.concatenate(rows, axis=0).astype(BF16), triu_ref[...],
                   preferred_element_type=F32)
    carry = jnp.zeros((N_HEADS, 1), F32)
    for c in range(n):
        hi, mid, lo = [sums[(N_SPLIT * c + i) * N_HEADS:(N_SPLIT * c + i + 1) * N_HEADS]
                       for i in range(N_SPLIT)]
        f = ((hi + mid) + lo) + carry
        f_ref[0, :, c * tile:(c + 1) * tile] = f
        carry = f[:, tile - 1:tile]


def _cache_cumsum(cl_t, triu, tile):
    b, h, p = cl_t.shape
    blk = pl.BlockSpec((1, h, p), lambda i: (i, 0, 0))
    return pl.pallas_call(
        functools.partial(_cache_cumsum_kernel, tile=tile),
        grid=(b,),
        in_specs=[blk, _const_spec((tile, tile))],
        out_specs=blk,
        out_shape=jax.ShapeDtypeStruct((b, h, p), F32),
        compiler_params=pltpu.CompilerParams(dimension_semantics=("arbitrary",)),
        name="cache_cumsum",
    )(cl_t, triu)


def _softmax_block_t(qt, k, vt, m, acc, mask):
    s = jnp.dot(k, qt, preferred_element_type=F32)
    if mask is not None:
        s = jnp.where(mask, s, -jnp.inf)
    m_new = jnp.maximum(m, jnp.max(s, axis=0, keepdims=True))
    p = jnp.exp2(s - m_new)
    acc = acc * jnp.exp2(m - m_new) + jnp.dot(vt, p.astype(BF16), preferred_element_type=F32)
    return m_new, acc


def _head_online(qt, kv_block, qi, causal, tq):
    def body(j, carry):
        k, vt = kv_block(j)
        return _softmax_block_t(qt, k, vt, carry[0], carry[1], None)

    init = (jnp.full((1, tq), -jnp.inf, F32), jnp.zeros((LANES, tq), F32))
    m, acc = lax.fori_loop(0, qi, body, init)
    k, vt = kv_block(qi)
    return _softmax_block_t(qt, k, vt, m, acc, causal)[1]


def _with_shift_rows(qt, m, parity):
    g0 = HEAD_DIM if parity == 0 else 0
    hi, mid, lo = _split3(-m)
    row = lax.broadcasted_iota(jnp.int32, (BF16_ROWS, m.shape[1]), 0)
    add = jnp.where(row == SHIFT_ROW, hi,
                    jnp.where(row == SHIFT_ROW + 1, mid, jnp.where(row == SHIFT_ROW + 2, lo, 0.0)))
    grp = (qt[g0:g0 + BF16_ROWS].astype(F32) + add).astype(BF16)
    pieces = ([qt[:g0]] if g0 else []) + [grp, qt[g0 + BF16_ROWS:]]
    return jnp.concatenate(pieces, axis=0)


def _heads_fixed_max(qts, kv_block, qi, causal, acc_ref, qs_ref):
    n = len(qts)
    dot = functools.partial(jnp.dot, preferred_element_type=F32)
    half = causal.shape[0]

    def diag(hh):
        k, qt = kv_block(hh, qi)[0], qts[hh]
        s_a = dot(k[:half], qt)
        s_a = jnp.concatenate([jnp.where(causal, s_a[:, :half], -jnp.inf), s_a[:, half:]], axis=1)
        return s_a, jnp.where(causal, dot(k[half:], qt[:, half:]), -jnp.inf)

    s_next = diag(0)
    for hh in range(n):
        (s_a, s_b), s_next = s_next, (diag(hh + 1) if hh + 1 < n else None)
        m_a = jnp.max(s_a, axis=0, keepdims=True)
        m_late = jnp.maximum(m_a[:, half:], jnp.max(s_b, axis=0, keepdims=True))
        m = jnp.concatenate([m_a[:, :half], m_late], axis=1)
        qs_ref[hh] = _with_shift_rows(qts[hh], m, hh % 2)
        vt = kv_block(hh, qi)[1]
        acc = dot(vt[:, :half], jnp.exp2(s_a - m).astype(BF16))
        late = acc[:, half:] + dot(vt[:, half:], jnp.exp2(s_b - m_late).astype(BF16))
        acc_ref[hh] = jnp.concatenate([acc[:, :half], late], axis=1)

    def run(blocks):
        items = [(j, hh) for j in blocks for hh in range(n)]
        scores = lambda item: dot(kv_block(item[1], item[0])[0], qs_ref[item[1]])
        s_next = scores(items[0])
        for idx, (j, hh) in enumerate(items):
            s, s_next = s_next, (scores(items[idx + 1]) if idx + 1 < len(items) else None)
            acc_ref[hh] += dot(kv_block(hh, j)[1], jnp.exp2(s).astype(BF16))

    def body(i, carry):
        run([2 * i, 2 * i + 1])
        return carry

    lax.fori_loop(0, qi // 2, body, 0)

    @pl.when(qi % 2 == 1)
    def _():
        run([qi - 1])

    return [acc_ref[hh] for hh in range(n)]


def _attn_kernel(qt_ref, k_ref, vt_ref, o_ref, acc_ref, qs_ref, *, tq):
    qi = pl.program_id(2)

    def causal_mask(n):
        return (lax.broadcasted_iota(jnp.int32, (n, n), 0)
                <= lax.broadcasted_iota(jnp.int32, (n, n), 1))

    def kv_block(hh, j):
        off = pl.multiple_of(j * tq, tq)
        return k_ref[0, hh, pl.ds(off, tq), :], vt_ref[0, hh, :, pl.ds(off, tq)]

    heads = qt_ref.shape[1]
    qts = [qt_ref[0, hh] for hh in range(heads)]

    def finish(accs):
        norm = lambda acc: acc[0:HEAD_DIM] * (1.0 / acc[HEAD_DIM:HEAD_DIM + 1])
        finite = None
        for pair in range(heads // 2):
            both = jnp.concatenate([norm(accs[2 * pair]), norm(accs[2 * pair + 1])], axis=0)
            o_ref[0, :, pair * LANES:(pair + 1) * LANES] = both.T
            ok = jnp.where(jnp.isfinite(both), 1.0, 0.0)
            finite = ok if finite is None else jnp.minimum(finite, ok)
        return jnp.min(finite) > 0.5

    all_finite = finish(_heads_fixed_max(qts, kv_block, qi, causal_mask(tq // 2), acc_ref, qs_ref))

    @pl.when(jnp.logical_not(all_finite))
    def _():
        finish([_head_online(qts[hh], functools.partial(kv_block, hh), qi, causal_mask(tq), tq)
                for hh in range(heads)])


def _attn(qt, ka, vt, tq, heads):
    b, _, s, _ = ka.shape
    assert heads % 2 == 0 and N_HEADS % heads == 0
    return pl.pallas_call(
        functools.partial(_attn_kernel, tq=tq),
        grid=(b, N_HEADS // heads, s // tq),
        in_specs=[pl.BlockSpec((1, heads, LANES, tq), lambda i, p, j: (i, p, 0, j)),
                  pl.BlockSpec((1, heads, s, LANES), lambda i, p, j: (i, p, 0, 0)),
                  pl.BlockSpec((1, heads, LANES, s), lambda i, p, j: (i, p, 0, 0))],
        out_specs=pl.BlockSpec((1, tq, heads * HEAD_DIM), lambda i, p, j: (i, j, p)),
        out_shape=jax.ShapeDtypeStruct((b, s, ATT_DIM), F32),
        scratch_shapes=[pltpu.VMEM((heads, LANES, tq), F32), pltpu.VMEM((heads, LANES, tq), BF16)],
        compiler_params=pltpu.CompilerParams(
            dimension_semantics=("arbitrary", "arbitrary", "arbitrary"),
            vmem_limit_bytes=VMEM_LIMIT_BYTES),
        name="attn",
    )(qt, ka, vt)


def _attn_sample_kernel(q_ref, kn_ref, vn_ref, kt_ref, vt_ref, fc_ref, o_ref, *, t):
    nt = (((1,), (1,)), ((), ()))
    causal = (lax.broadcasted_iota(jnp.int32, (t, LANES), 1)
              <= lax.broadcasted_iota(jnp.int32, (t, LANES), 0))
    pad = jnp.zeros((LANES - t, LANES), BF16)
    fk = fc_ref[0] * LOG2E
    outs = []
    for h in range(N_HEADS):
        head, extras = (0, HEAD_DIM) if h % 2 == 0 else (HEAD_DIM, 0)
        tile = q_ref[0, h]
        ex = tile[:, extras:extras + N_SPLIT].astype(F32)
        fq = (ex[:, 0:1] + ex[:, 1:2]) + ex[:, 2:3]
        s_old = jnp.dot(tile[:, head:head + HEAD_DIM], kt_ref[0, h].astype(BF16),
                        preferred_element_type=F32) + (fq - fk[h:h + 1, :])
        kn = jnp.concatenate([kn_ref[0, h], pad], axis=0)
        vn = jnp.concatenate([vn_ref[0, h], pad], axis=0)
        s_new = jnp.where(causal, lax.dot_general(tile, kn, nt, preferred_element_type=F32), -jnp.inf)
        m = jnp.maximum(jnp.max(s_old, axis=1, keepdims=True), jnp.max(s_new, axis=1, keepdims=True))
        p_old = jnp.exp2(s_old - m)
        acc_new = jnp.dot(jnp.exp2(s_new - m).astype(BF16), vn, preferred_element_type=F32)
        o_old = lax.dot_general(p_old.astype(BF16), vt_ref[0, h].astype(BF16), nt,
                                preferred_element_type=F32)
        denom = jnp.sum(p_old, axis=1, keepdims=True) + acc_new[:, extras:extras + 1]
        outs.append((o_old + acc_new[:, head:head + HEAD_DIM]) * (1.0 / denom))
    o_ref[0] = jnp.concatenate(outs, axis=1)


def _attn_sample(qa, kn, vn, kt, vt, fc):
    b, _, t, _ = qa.shape
    p = kt.shape[3]
    new = pl.BlockSpec((1, N_HEADS, t, LANES), lambda i: (i, 0, 0, 0))
    old = pl.BlockSpec((1, N_HEADS, HEAD_DIM, p), lambda i: (i, 0, 0, 0))
    return pl.pallas_call(
        functools.partial(_attn_sample_kernel, t=t),
        grid=(b,),
        in_specs=[new, new, new, old, old, pl.BlockSpec((1, N_HEADS, p), lambda i: (i, 0, 0))],
        out_specs=pl.BlockSpec((1, t, ATT_DIM), lambda i: (i, 0, 0)),
        out_shape=jax.ShapeDtypeStruct((b, t, ATT_DIM), F32),
        compiler_params=pltpu.CompilerParams(dimension_semantics=("arbitrary",),
                                             vmem_limit_bytes=VMEM_LIMIT_BYTES),
        name="attn_sample",
    )(qa, kn, vn, kt, vt, fc)


def _post_kernel(x_ref, att_ref, cn_ref, gt1_ref, sh2_ref, sc2_ref, gt2_ref, shf_ref, scf_ref,
                 gatt_ref, g2_ref, gf_ref, wo_ref, wu_ref, wd_ref, y_ref, *, ff_chunk, row_chunk):
    tile = x_ref.shape[1]
    chunks = [slice(r0, r0 + row_chunk) for r0 in range(0, tile, row_chunk)]
    gain2 = g2_ref[...] * (1.0 + sc2_ref[0])
    gain_f = gf_ref[...] * (1.0 + scf_ref[0])

    def normed_att(rows):
        return (_rms(att_ref[0, rows, :]) * gatt_ref[...]).astype(BF16)

    def out_proj(rows, xa):
        return (jnp.dot(xa, wo_ref[0:ATT_DIM, :], preferred_element_type=F32)
                + jnp.dot(cn_ref[0, rows, :], wo_ref[ATT_DIM:, :], preferred_element_type=F32))

    def residual1(rows, mixed):
        x1 = x_ref[0, rows, :] + gt1_ref[0] * mixed
        return x1, (_rms(x1) * gain2 + sh2_ref[0]).astype(BF16)

    def mlp(h2):
        total = None
        for c in range(D_FF // ff_chunk):
            up = jnp.dot(h2, wu_ref[:, c * ff_chunk:(c + 1) * ff_chunk], preferred_element_type=F32)
            act = jnp.square(jnp.maximum(up, 0.0)).astype(BF16)
            part = jnp.dot(act, wd_ref[c * ff_chunk:(c + 1) * ff_chunk, :], preferred_element_type=F32)
            total = part if total is None else total + part
        return total

    def finish(rows, x1, m):
        x2 = x1 + gt2_ref[0] * m
        y_ref[0, rows, :] = _rms(x2) * gain_f + shf_ref[0]

    n = len(chunks)
    mixed = {0: out_proj(chunks[0], normed_att(chunks[0]))}
    pending = None
    for c in range(n):
        if c + 1 < n:
            mixed[c + 1] = out_proj(chunks[c + 1], normed_att(chunks[c + 1]))
        x1, h2 = residual1(chunks[c], mixed.pop(c))
        m = mlp(h2)
        if pending is not None:
            finish(*pending)
        pending = (chunks[c], x1, m)
    finish(*pending)


def _post(x, att, cn, mods, g_att, g2, g_final, w_out, w_up, w_down, tile):
    b, s, _ = x.shape
    row_chunk = POST_ROW_CHUNK if tile % POST_ROW_CHUNK == 0 else tile
    tok = lambda n: pl.BlockSpec((1, tile, n), lambda i, j: (i, j, 0))
    if mods[0].shape[1] == 1:
        per_b = pl.BlockSpec((1, 1, D_MODEL), lambda i, j: (i, 0, 0))
    else:
        assert row_chunk == tile
        per_b = tok(D_MODEL)
    return pl.pallas_call(
        functools.partial(_post_kernel, ff_chunk=1024, row_chunk=row_chunk),
        grid=(b, s // tile),
        in_specs=[tok(D_MODEL), tok(ATT_DIM), tok(CONV_DIM)] + [per_b] * 6
                 + [_const_spec((1, ATT_DIM)), _const_spec((1, D_MODEL)), _const_spec((1, D_MODEL)),
                    _const_spec((D_MODEL, D_MODEL)), _const_spec((D_MODEL, D_FF)),
                    _const_spec((D_FF, D_MODEL))],
        out_specs=tok(D_MODEL),
        out_shape=jax.ShapeDtypeStruct((b, s, D_MODEL), F32),
        compiler_params=pltpu.CompilerParams(dimension_semantics=("arbitrary", "arbitrary"),
                                             vmem_limit_bytes=VMEM_LIMIT_BYTES),
        name="post",
    )(x, att, cn, *mods, g_att, g2, g_final, w_out, w_up, w_down)


def _placement():
    pq = np.zeros((LANES, N_HEADS * LANES), np.float32)
    pq_t = np.zeros((ATT_DIM, LANES), np.float32)
    for h in range(N_HEADS):
        base = h * LANES + (HEAD_DIM if h % 2 == 0 else 0)
        for part in range(N_SPLIT):
            src, own = part * N_HEADS + h, E_F0 + part * N_HEADS + h
            pq[src, base + part] = 1.0
            pq[ONES_ROW, base + own] = 1.0
            pq_t[h * HEAD_DIM + part, src] = 1.0
            pq_t[h * HEAD_DIM + own, ONES_ROW] = 1.0
    return jnp.asarray(pq, BF16), jnp.asarray(pq_t, BF16)


def _tri(n):
    return jnp.asarray(np.tri(n, dtype=np.float32), BF16)


def _pack_w_in(w):
    a, c = ATT_DIM, CONV_DIM
    f0 = 3 * a
    b0 = f0 + N_HEADS
    pad = jnp.zeros((D_MODEL, LANES - N_HEADS), w.dtype)
    wq = w[:, :a] * (ATT_SCALE * LOG2E)
    packed = jnp.concatenate([wq, w[:, a:f0], w[:, b0:b0 + 3 * c], w[:, f0:b0], pad], axis=1)
    transposed = jnp.concatenate([wq, w[:, f0:b0], pad[:, :BF16_ROWS - N_HEADS]], axis=1).T
    return packed.astype(BF16), transposed.astype(BF16)


def kernel(x_prompt, x_sample, cache_k, cache_v, cache_logf, cache_conv, c_prompt, c_sample, w_ada, b_ada, g_norm1, g_norm2, w_in, b_f, w_conv, g_attn_out, g_conv_out, w_out, w_up, w_down, w_ada_final, b_ada_final, g_final):
    assert w_ada.shape[0] == 1, "single layer"
    nb, s, _ = x_prompt.shape
    db, t, _ = x_sample.shape
    p = cache_k.shape[2]
    tile_prompt, tile_cache, tq, heads_per_step, tile_post = 512, 512, 512, 8, 512

    c_all = jnp.concatenate([c_prompt, c_sample], axis=0)
    mod = _ada(c_all, w_ada[0], b_ada[0])
    mod_f = _ada(c_all, w_ada_final, b_ada_final)
    sh1, sc1, gt1, sh2, sc2, gt2 = [m[:, None, :] for m in jnp.split(mod, 6, axis=-1)]
    shf, scf = [m[:, None, :] for m in jnp.split(mod_f, 2, axis=-1)]

    w_pack, w_t = _pack_w_in(w_in[0])
    bf_pad = jnp.pad(b_f[0], (0, LANES - N_HEADS)).reshape(1, LANES)
    bf_col = jnp.broadcast_to(b_f[0][:, None], (N_HEADS, LANES))
    pq, pq_t = _placement()
    g1 = g_norm1[0].reshape(1, D_MODEL)
    g2 = g_norm2[0].reshape(1, D_MODEL)
    gf = g_final.reshape(1, D_MODEL)
    g_att = g_attn_out[0].reshape(1, ATT_DIM)
    g_conv = g_conv_out[0].reshape(1, CONV_DIM)
    wo, wu, wd = w_out[0].astype(BF16), w_up[0].astype(BF16), w_down[0].astype(BF16)

    def layer(x, rows, prev, f0, tile, transposed_qv, attend):
        sel = lambda m: m[rows]
        k, v, logf, qa, ka, va, cn, cs = _proj(x, sel(sh1), sel(sc1), g1, w_pack, w_t, bf_pad, bf_col, w_conv[0],
                                               g_conv, prev, f0, _tri(tile), pq, pq_t, tile,
                                               transposed_qv)
        att = attend(qa, ka, va)
        mods = [sel(m) for m in (gt1, sh2, sc2, gt2, shf, scf)]
        bsz, sl = x.shape[0], x.shape[1]
        if sl >= tile_post:
            y = _post(x, att, cn, mods, g_att, g2, gf, wo, wu, wd, tile_post)
        else:
            flat = lambda a: a.reshape(1, bsz * sl, a.shape[-1])
            rows_of = lambda m: flat(jnp.broadcast_to(m, (bsz, sl, D_MODEL)))
            y = _post(flat(x), flat(att), flat(cn), [rows_of(m) for m in mods], g_att, g2, gf,
                      wo, wu, wd, bsz * sl).reshape(bsz, sl, D_MODEL)
        heads = lambda a: a.reshape(1, bsz, sl, N_HEADS, HEAD_DIM)
        if transposed_qv:
            logf = jnp.transpose(logf, (0, 2, 1))
        return y, heads(k), heads(v), logf[None], cs[None]

    zeros_prev = jnp.zeros((nb, CONV_WIDTH - 1, CONV_DIM), F32)
    zeros_f = jnp.zeros((nb, 1, LANES), F32)
    yp, kp, vp, lp, cp = layer(x_prompt, slice(0, nb), zeros_prev, zeros_f, tile_prompt, True,
                               lambda qt, ka, vt: _attn(qt, ka, vt, tq, heads_per_step))

    kt = jnp.transpose(cache_k[0], (0, 2, 3, 1))
    vt = jnp.transpose(cache_v[0], (0, 2, 3, 1))
    fc = _cache_cumsum(jnp.transpose(cache_logf[0], (0, 2, 1)), _tri(tile_cache).T, tile_cache)
    f_tot = jnp.pad(fc[:, :, p - 1], ((0, 0), (0, LANES - N_HEADS)))[:, None, :]
    ys, ks, vs, ls, cs = layer(x_sample, slice(nb, nb + db), cache_conv[0], f_tot, t, False,
                               lambda qa, ka, va: _attn_sample(qa, ka, va, kt, vt, fc))
    return (yp, ys, kp, vp, lp, cp, ks, vs, ls, cs)
```

```python
import functools

import jax
import jax.numpy as jnp
import numpy as np
from jax import lax
from jax.experimental import pallas as pl
from jax.experimental.pallas import tpu as pltpu

F32 = jnp.float32
BF16 = jnp.bfloat16

D_MODEL = 1024
N_HEADS = 8
HEAD_DIM = 64
ATT_DIM = N_HEADS * HEAD_DIM
CONV_DIM = 512
CONV_WIDTH = 3
D_FF = 4 * D_MODEL
NORM_EPS = 1e-6
ATT_SCALE = HEAD_DIM ** -0.5

LANES = 128
CARRY_ROWS = 8
N_SPLIT = 3
ONES_ROW = N_SPLIT * N_HEADS
VMEM_LIMIT_BYTES = 56 * 1024 * 1024
BF16_ROWS = 16
SHIFT_ROW = 8
E_F0 = 16
ROW_CHUNKS = 2
POST_ROW_CHUNK = 256
LOG2E = 1.4426950408889634

SEC_Q, SEC_K, SEC_V, SEC_BG, SEC_CG, SEC_U = range(6)
SEC_W = 512
F_COL = 6 * SEC_W
PROJ_COLS = F_COL + LANES


def _const_spec(shape):
    zeros = (0,) * len(shape)
    return pl.BlockSpec(shape, lambda *_: zeros, pipeline_mode=pl.Buffered(1))


def _rms(x):
    return x * lax.rsqrt(jnp.mean(x * x, axis=-1, keepdims=True) + NORM_EPS)


def _log_sigmoid(x):
    return jnp.minimum(x, 0.0) - jnp.log1p(jnp.exp(-jnp.abs(x)))


def _split3(x):
    hi = x.astype(BF16).astype(F32)
    r = x - hi
    mid = r.astype(BF16).astype(F32)
    lo = (r - mid).astype(BF16).astype(F32)
    return hi, mid, lo


def _cumsum_rows(l128, tri_ref, lane):
    hi, mid, lo = _split3(l128)
    cat = hi + pltpu.roll(mid, N_HEADS, axis=1) + pltpu.roll(lo, 2 * N_HEADS, axis=1)
    sums = jnp.dot(tri_ref[...], cat.astype(BF16), preferred_element_type=F32)
    total = ((sums + pltpu.roll(sums, LANES - N_HEADS, axis=1))
             + pltpu.roll(sums, LANES - 2 * N_HEADS, axis=1))
    return jnp.where(lane < N_HEADS, total, 0.0)


def _f_parts(f128):
    hi, mid, lo = _split3(f128 * LOG2E)
    return hi + pltpu.roll(mid, N_HEADS, axis=1) + pltpu.roll(lo, 2 * N_HEADS, axis=1)


def _fcat(parts, lane):
    return (parts + jnp.where(lane == ONES_ROW, 1.0, 0.0)).astype(BF16)


def _k_extras(parts, lane):
    ones = jnp.where((lane < N_SPLIT) | ((lane >= SHIFT_ROW) & (lane < SHIFT_ROW + N_SPLIT)), 1.0, 0.0)
    odd = ones - pltpu.roll(parts, E_F0, axis=1)
    return pltpu.roll(odd, HEAD_DIM, axis=1), odd


def _store_aug(dst_ref, main, extras_even, extras_odd, lane):
    low = lane < HEAD_DIM
    for pair in range(N_HEADS // 2):
        m = main[:, pair * LANES:(pair + 1) * LANES]
        dst_ref[0, 2 * pair] = jnp.where(low, m, extras_even(pair)).astype(BF16)
        dst_ref[0, 2 * pair + 1] = jnp.where(low, extras_odd(pair), m).astype(BF16)


def _store_v_aug(dst_ref, v, lane):
    even = jnp.where(lane == HEAD_DIM, 1.0, 0.0)
    odd = jnp.where(lane == 0, 1.0, 0.0)
    _store_aug(dst_ref, v, lambda pair: even, lambda pair: odd, lane)


def _ada_kernel(c_ref, w_ref, b_ref, o_ref):
    c = c_ref[...]
    sc = (c * jax.nn.sigmoid(c)).astype(BF16)
    o_ref[...] = jnp.dot(sc, w_ref[...].astype(BF16), preferred_element_type=F32) + b_ref[...]


def _ada(c, w, b):
    rows, n = c.shape[0], w.shape[1]
    bn = 1024
    return pl.pallas_call(
        _ada_kernel,
        grid=(n // bn,),
        in_specs=[pl.BlockSpec((rows, D_MODEL), lambda j: (0, 0)),
                  pl.BlockSpec((D_MODEL, bn), lambda j: (0, j)),
                  pl.BlockSpec((1, bn), lambda j: (0, j))],
        out_specs=pl.BlockSpec((rows, bn), lambda j: (0, j)),
        out_shape=jax.ShapeDtypeStruct((rows, n), F32),
        name="ada",
    )(c, w, b.reshape(1, n))


def _store_qv_transposed(qa_ref, va_ref, q_t, v_t, eq_t, tile):
    ones_row = jnp.where(lax.broadcasted_iota(jnp.int32, (HEAD_DIM, tile), 0) == 0, 1.0, 0.0)
    for h in range(N_HEADS):
        rows = slice(h * HEAD_DIM, (h + 1) * HEAD_DIM)
        parts = [q_t[rows], eq_t[rows]]
        qa_ref[0, h] = jnp.concatenate(parts if h % 2 == 0 else parts[::-1], axis=0).astype(BF16)
        va_ref[0, h] = jnp.concatenate([v_t[rows], ones_row], axis=0).astype(BF16)


def _proj_kernel(x_ref, sh_ref, sc_ref, g1_ref, w_ref, wt_ref, bf_ref, bfc_ref, wconv_ref, gconv_ref, prev_ref,
                 f0_ref, tri_ref, pq_ref, pqt_ref,
                 k_ref, v_ref, logf_ref, qa_ref, ka_ref, va_ref, cn_ref, cs_ref,
                 ue_ref, fc_ref, *, tile, transposed_qv):
    s = pl.program_id(1)

    @pl.when(s == 0)
    def _():
        ue_ref[0:CARRY_ROWS, :] = jnp.zeros((CARRY_ROWS, CONV_DIM), F32)
        ue_ref[CARRY_ROWS - (CONV_WIDTH - 1):CARRY_ROWS, :] = prev_ref[0]
        fc_ref[...] = f0_ref[0]

    lane = lax.broadcasted_iota(jnp.int32, (tile, LANES), 1)
    nt = (((1,), (1,)), ((), ()))
    gain = g1_ref[...] * (1.0 + sc_ref[0])
    shift = sh_ref[0]

    chunk = tile // ROW_CHUNKS if tile % (ROW_CHUNKS * BF16_ROWS) == 0 else tile
    starts = list(range(0, tile, chunk))

    def conv_matmuls(r0):
        hb_c = (_rms(x_ref[0, r0:r0 + chunk, :]) * gain + shift).astype(BF16)
        sec = lambda i: jnp.dot(hb_c, w_ref[:, i * SEC_W:(i + 1) * SEC_W], preferred_element_type=F32)
        return hb_c, sec(SEC_CG), sec(SEC_U), sec(SEC_BG)

    def conv_elementwise(r0, cg, u, bg):
        up = cg * u
        base = CARRY_ROWS + r0
        ue_ref[base:base + chunk, :] = up
        cv = (wconv_ref[0:1, :] * ue_ref[base - 2:base - 2 + chunk, :]
              + wconv_ref[1:2, :] * ue_ref[base - 1:base - 1 + chunk, :]
              + wconv_ref[2:3, :] * up)
        cn_ref[0, r0:r0 + chunk, :] = (_rms(bg * cv) * gconv_ref[...]).astype(BF16)

    hbs, pending = [], None
    for r0 in starts:
        hb_c, cg, u, bg = conv_matmuls(r0)
        hbs.append(hb_c)
        if pending is not None:
            conv_elementwise(*pending)
        pending = (r0, cg, u, bg)
    hb = jnp.concatenate(hbs, axis=0) if len(hbs) > 1 else hbs[0]

    def proj(sec, width=SEC_W):
        return jnp.dot(hb, w_ref[:, sec * SEC_W:sec * SEC_W + width], preferred_element_type=F32)

    if transposed_qv:
        qf_t = lax.dot_general(wt_ref[...], hb, nt, preferred_element_type=F32)
        q_t, fl_t = qf_t[:ATT_DIM], qf_t[ATT_DIM:]
        fl = jnp.concatenate([fl_t, jnp.zeros((LANES - BF16_ROWS, tile), F32)], axis=0).T
    else:
        fl = proj(6, LANES)
    k = proj(SEC_K)
    k_ref[0] = k
    conv_elementwise(*pending)
    ue_ref[0:CARRY_ROWS, :] = ue_ref[tile:tile + CARRY_ROWS, :]

    logf = jnp.where(lane < N_HEADS, _log_sigmoid(fl + bf_ref[...]), 0.0)
    if transposed_qv:
        logf_ref[0] = _log_sigmoid(fl_t[0:N_HEADS] + bfc_ref[:, 0:1])
    else:
        logf_ref[0] = logf[:, :N_HEADS]
    f128 = _cumsum_rows(logf, tri_ref, lane) + fc_ref[...]
    v = proj(SEC_V)
    v_ref[0] = v
    fc_ref[...] = f128[tile - 1:tile, :]
    parts = _f_parts(f128)
    fcat = _fcat(parts, lane)

    ke_even, ke_odd = _k_extras(parts, lane)
    _store_aug(ka_ref, k, lambda pair: ke_even, lambda pair: ke_odd, lane)
    if transposed_qv:
        eq_t = lax.dot_general(pqt_ref[...], fcat, nt, preferred_element_type=F32)
        _store_qv_transposed(qa_ref, va_ref, q_t, v.T, eq_t, tile)
    else:
        eq = jnp.dot(fcat, pq_ref[...], preferred_element_type=F32)
        _store_aug(qa_ref, proj(SEC_Q), lambda pair: eq[:, 2 * pair * LANES:(2 * pair + 1) * LANES],
                   lambda pair: eq[:, (2 * pair + 1) * LANES:(2 * pair + 2) * LANES], lane)
        _store_v_aug(va_ref, v, lane)

    @pl.when(s == pl.num_programs(1) - 1)
    def _():
        cs_ref[0] = ue_ref[CARRY_ROWS + tile - (CONV_WIDTH - 1):CARRY_ROWS + tile, :]


def _proj(x, sh, sc, g1, w_pack, w_t, bf_pad, bf_col, w_conv, g_conv, prev, f0, tri, pq, pq_t, tile,
          transposed_qv):
    b, s, _ = x.shape
    assert s % tile == 0 and tile % CARRY_ROWS == 0
    ns = s // tile
    tok = lambda n: pl.BlockSpec((1, tile, n), lambda i, j: (i, j, 0))
    per_b = lambda r, n: pl.BlockSpec((1, r, n), lambda i, j: (i, 0, 0))
    aug = pl.BlockSpec((1, N_HEADS, tile, LANES), lambda i, j: (i, 0, j, 0))
    aug_sds = jax.ShapeDtypeStruct((b, N_HEADS, s, LANES), BF16)
    if transposed_qv:
        qv = pl.BlockSpec((1, N_HEADS, LANES, tile), lambda i, j: (i, 0, 0, j))
        qv_sds = jax.ShapeDtypeStruct((b, N_HEADS, LANES, s), BF16)
        lf = pl.BlockSpec((1, N_HEADS, tile), lambda i, j: (i, 0, j))
        lf_sds = jax.ShapeDtypeStruct((b, N_HEADS, s), F32)
    else:
        qv, qv_sds = aug, aug_sds
        lf, lf_sds = tok(N_HEADS), jax.ShapeDtypeStruct((b, s, N_HEADS), F32)
    out_shape = (
        jax.ShapeDtypeStruct((b, s, ATT_DIM), F32),
        jax.ShapeDtypeStruct((b, s, ATT_DIM), F32),
        lf_sds,
        qv_sds,
        aug_sds,
        qv_sds,
        jax.ShapeDtypeStruct((b, s, CONV_DIM), BF16),
        jax.ShapeDtypeStruct((b, CONV_WIDTH - 1, CONV_DIM), F32),
    )
    return pl.pallas_call(
        functools.partial(_proj_kernel, tile=tile, transposed_qv=transposed_qv),
        grid=(b, ns),
        in_specs=[tok(D_MODEL), per_b(1, D_MODEL), per_b(1, D_MODEL), _const_spec((1, D_MODEL)),
                  _const_spec((D_MODEL, PROJ_COLS)), _const_spec((ATT_DIM + BF16_ROWS, D_MODEL)),
                  _const_spec((1, LANES)), _const_spec((N_HEADS, LANES)),
                  _const_spec((CONV_WIDTH, CONV_DIM)), _const_spec((1, CONV_DIM)),
                  per_b(CONV_WIDTH - 1, CONV_DIM), per_b(1, LANES),
                  _const_spec((tile, tile)), _const_spec((LANES, N_HEADS * LANES)),
                  _const_spec((ATT_DIM, LANES))],
        out_specs=(tok(ATT_DIM), tok(ATT_DIM), lf, qv, aug, qv, tok(CONV_DIM),
                   per_b(CONV_WIDTH - 1, CONV_DIM)),
        out_shape=out_shape,
        scratch_shapes=[pltpu.VMEM((tile + CARRY_ROWS, CONV_DIM), F32), pltpu.VMEM((1, LANES), F32)],
        compiler_params=pltpu.CompilerParams(dimension_semantics=("arbitrary", "arbitrary"),
                                             vmem_limit_bytes=VMEM_LIMIT_BYTES),
        name="proj",
    )(x, sh, sc, g1, w_pack, w_t, bf_pad, bf_col, w_conv, g_conv, prev, f0, tri, pq, pq_t)


def _cache_cumsum_kernel(l_ref, triu_ref, f_ref, *, tile):
    n = l_ref.shape[2] // tile
    rows = []
    for c in range(n):
        rows.extend(_split3(l_ref[0, :, c * tile:(c + 1) * tile]))
    sums = jnp.dot(jnp.concatenate(rows, axis=0).astype(BF16), triu_ref[...],
                   preferred_element_type=F32)
    carry = jnp.zeros((N_HEADS, 1), F32)
    for c in range(n):
        hi, mid, lo = [sums[(N_SPLIT * c + i) * N_HEADS:(N_SPLIT * c + i + 1) * N_HEADS]
                       for i in range(N_SPLIT)]
        f = ((hi + mid) + lo) + carry
        f_ref[0, :, c * tile:(c + 1) * tile] = f
        carry = f[:, tile - 1:tile]


def _cache_cumsum(cl_t, triu, tile):
    b, h, p = cl_t.shape
    blk = pl.BlockSpec((1, h, p), lambda i: (i, 0, 0))
    return pl.pallas_call(
        functools.partial(_cache_cumsum_kernel, tile=tile),
        grid=(b,),
        in_specs=[blk, _const_spec((tile, tile))],
        out_specs=blk,
        out_shape=jax.ShapeDtypeStruct((b, h, p), F32),
        compiler_params=pltpu.CompilerParams(dimension_semantics=("arbitrary",)),
        name="cache_cumsum",
    )(cl_t, triu)


def _softmax_block_t(qt, k, vt, m, acc, mask):
    s = jnp.dot(k, qt, preferred_element_type=F32)
    if mask is not None:
        s = jnp.where(mask, s, -jnp.inf)
    m_new = jnp.maximum(m, jnp.max(s, axis=0, keepdims=True))
    p = jnp.exp2(s - m_new)
    acc = acc * jnp.exp2(m - m_new) + jnp.dot(vt, p.astype(BF16), preferred_element_type=F32)
    return m_new, acc


def _head_online(qt, kv_block, qi, causal, tq):
    def body(j, carry):
        k, vt = kv_block(j)
        return _softmax_block_t(qt, k, vt, carry[0], carry[1], None)

    init = (jnp.full((1, tq), -jnp.inf, F32), jnp.zeros((LANES, tq), F32))
    m, acc = lax.fori_loop(0, qi, body, init)
    k, vt = kv_block(qi)
    return _softmax_block_t(qt, k, vt, m, acc, causal)[1]


def _with_shift_rows(qt, m, parity):
    g0 = HEAD_DIM if parity == 0 else 0
    hi, mid, lo = _split3(-m)
    row = lax.broadcasted_iota(jnp.int32, (BF16_ROWS, m.shape[1]), 0)
    add = jnp.where(row == SHIFT_ROW, hi,
                    jnp.where(row == SHIFT_ROW + 1, mid, jnp.where(row == SHIFT_ROW + 2, lo, 0.0)))
    grp = (qt[g0:g0 + BF16_ROWS].astype(F32) + add).astype(BF16)
    pieces = ([qt[:g0]] if g0 else []) + [grp, qt[g0 + BF16_ROWS:]]
    return jnp.concatenate(pieces, axis=0)


def _heads_fixed_max(qts, kv_block, qi, causal, acc_ref, qs_ref):
    n = len(qts)
    dot = functools.partial(jnp.dot, preferred_element_type=F32)
    half = causal.shape[0]

    def diag(hh):
        k, qt = kv_block(hh, qi)[0], qts[hh]
        s_a = dot(k[:half], qt)
        s_a = jnp.concatenate([jnp.where(causal, s_a[:, :half], -jnp.inf), s_a[:, half:]], axis=1)
        return s_a, jnp.where(causal, dot(k[half:], qt[:, half:]), -jnp.inf)

    s_next = diag(0)
    for hh in range(n):
        (s_a, s_b), s_next = s_next, (diag(hh + 1) if hh + 1 < n else None)
        m_a = jnp.max(s_a, axis=0, keepdims=True)
        m_late = jnp.maximum(m_a[:, half:], jnp.max(s_b, axis=0, keepdims=True))
        m = jnp.concatenate([m_a[:, :half], m_late], axis=1)
        qs_ref[hh] = _with_shift_rows(qts[hh], m, hh % 2)
        vt = kv_block(hh, qi)[1]
        acc = dot(vt[:, :half], jnp.exp2(s_a - m).astype(BF16))
        late = acc[:, half:] + dot(vt[:, half:], jnp.exp2(s_b - m_late).astype(BF16))
        acc_ref[hh] = jnp.concatenate([acc[:, :half], late], axis=1)

    def run(blocks):
        items = [(j, hh) for j in blocks for hh in range(n)]
        scores = lambda item: dot(kv_block(item[1], item[0])[0], qs_ref[item[1]])
        s_next = scores(items[0])
        for idx, (j, hh) in enumerate(items):
            s, s_next = s_next, (scores(items[idx + 1]) if idx + 1 < len(items) else None)
            acc_ref[hh] += dot(kv_block(hh, j)[1], jnp.exp2(s).astype(BF16))

    def body(i, carry):
        run([2 * i, 2 * i + 1])
        return carry

    lax.fori_loop(0, qi // 2, body, 0)

    @pl.when(qi % 2 == 1)
    def _():
        run([qi - 1])

    return [acc_ref[hh] for hh in range(n)]


def _attn_kernel(qt_ref, k_ref, vt_ref, o_ref, acc_ref, qs_ref, *, tq):
    qi = pl.program_id(2)

    def causal_mask(n):
        return (lax.broadcasted_iota(jnp.int32, (n, n), 0)
                <= lax.broadcasted_iota(jnp.int32, (n, n), 1))

    def kv_block(hh, j):
        off = pl.multiple_of(j * tq, tq)
        return k_ref[0, hh, pl.ds(off, tq), :], vt_ref[0, hh, :, pl.ds(off, tq)]

    heads = qt_ref.shape[1]
    qts = [qt_ref[0, hh] for hh in range(heads)]

    def finish(accs):
        norm = lambda acc: acc[0:HEAD_DIM] * (1.0 / acc[HEAD_DIM:HEAD_DIM + 1])
        finite = None
        for pair in range(heads // 2):
            both = jnp.concatenate([norm(accs[2 * pair]), norm(accs[2 * pair + 1])], axis=0)
            o_ref[0, pair * LANES:(pair + 1) * LANES, :] = both
            ok = jnp.where(jnp.isfinite(both), 1.0, 0.0)
            finite = ok if finite is None else jnp.minimum(finite, ok)
        return jnp.min(finite) > 0.5

    all_finite = finish(_heads_fixed_max(qts, kv_block, qi, causal_mask(tq // 2), acc_ref, qs_ref))

    @pl.when(jnp.logical_not(all_finite))
    def _():
        finish([_head_online(qts[hh], functools.partial(kv_block, hh), qi, causal_mask(tq), tq)
                for hh in range(heads)])


def _attn(qt, ka, vt, tq, heads):
    b, _, s, _ = ka.shape
    assert heads % 2 == 0 and N_HEADS % heads == 0
    return pl.pallas_call(
        functools.partial(_attn_kernel, tq=tq),
        grid=(b, N_HEADS // heads, s // tq),
        in_specs=[pl.BlockSpec((1, heads, LANES, tq), lambda i, p, j: (i, p, 0, j)),
                  pl.BlockSpec((1, heads, s, LANES), lambda i, p, j: (i, p, 0, 0)),
                  pl.BlockSpec((1, heads, LANES, s), lambda i, p, j: (i, p, 0, 0))],
        out_specs=pl.BlockSpec((1, heads * HEAD_DIM, tq), lambda i, p, j: (i, p, j)),
        out_shape=jax.ShapeDtypeStruct((b, ATT_DIM, s), F32),
        scratch_shapes=[pltpu.VMEM((heads, LANES, tq), F32), pltpu.VMEM((heads, LANES, tq), BF16)],
        compiler_params=pltpu.CompilerParams(
            dimension_semantics=("arbitrary", "arbitrary", "arbitrary"),
            vmem_limit_bytes=VMEM_LIMIT_BYTES),
        name="attn",
    )(qt, ka, vt)


def _attn_sample_kernel(q_ref, kn_ref, vn_ref, kt_ref, vt_ref, fc_ref, o_ref, *, t):
    nt = (((1,), (1,)), ((), ()))
    causal = (lax.broadcasted_iota(jnp.int32, (t, LANES), 1)
              <= lax.broadcasted_iota(jnp.int32, (t, LANES), 0))
    pad = jnp.zeros((LANES - t, LANES), BF16)
    fk = fc_ref[0] * LOG2E
    outs = []
    for h in range(N_HEADS):
        head, extras = (0, HEAD_DIM) if h % 2 == 0 else (HEAD_DIM, 0)
        tile = q_ref[0, h]
        ex = tile[:, extras:extras + N_SPLIT].astype(F32)
        fq = (ex[:, 0:1] + ex[:, 1:2]) + ex[:, 2:3]
        s_old = jnp.dot(tile[:, head:head + HEAD_DIM], kt_ref[0, h].astype(BF16),
                        preferred_element_type=F32) + (fq - fk[h:h + 1, :])
        kn = jnp.concatenate([kn_ref[0, h], pad], axis=0)
        vn = jnp.concatenate([vn_ref[0, h], pad], axis=0)
        s_new = jnp.where(causal, lax.dot_general(tile, kn, nt, preferred_element_type=F32), -jnp.inf)
        m = jnp.maximum(jnp.max(s_old, axis=1, keepdims=True), jnp.max(s_new, axis=1, keepdims=True))
        p_old = jnp.exp2(s_old - m)
        acc_new = jnp.dot(jnp.exp2(s_new - m).astype(BF16), vn, preferred_element_type=F32)
        o_old = lax.dot_general(p_old.astype(BF16), vt_ref[0, h].astype(BF16), nt,
                                preferred_element_type=F32)
        denom = jnp.sum(p_old, axis=1, keepdims=True) + acc_new[:, extras:extras + 1]
        outs.append((o_old + acc_new[:, head:head + HEAD_DIM]) * (1.0 / denom))
    o_ref[0] = jnp.concatenate(outs, axis=1)


def _attn_sample(qa, kn, vn, kt, vt, fc):
    b, _, t, _ = qa.shape
    p = kt.shape[3]
    new = pl.BlockSpec((1, N_HEADS, t, LANES), lambda i: (i, 0, 0, 0))
    old = pl.BlockSpec((1, N_HEADS, HEAD_DIM, p), lambda i: (i, 0, 0, 0))
    return pl.pallas_call(
        functools.partial(_attn_sample_kernel, t=t),
        grid=(b,),
        in_specs=[new, new, new, old, old, pl.BlockSpec((1, N_HEADS, p), lambda i: (i, 0, 0))],
        out_specs=pl.BlockSpec((1, t, ATT_DIM), lambda i: (i, 0, 0)),
        out_shape=jax.ShapeDtypeStruct((b, t, ATT_DIM), F32),
        compiler_params=pltpu.CompilerParams(dimension_semantics=("arbitrary",),
                                             vmem_limit_bytes=VMEM_LIMIT_BYTES),
        name="attn_sample",
    )(qa, kn, vn, kt, vt, fc)


def _post_kernel(x_ref, att_ref, cn_ref, gt1_ref, sh2_ref, sc2_ref, gt2_ref, shf_ref, scf_ref,
                 gatt_ref, g2_ref, gf_ref, wo_ref, wu_ref, wd_ref, y_ref, *, ff_chunk, row_chunk,
                 att_feature_major):
    tile = x_ref.shape[1]
    chunks = [slice(r0, r0 + row_chunk) for r0 in range(0, tile, row_chunk)]
    gain2 = g2_ref[...] * (1.0 + sc2_ref[0])
    gain_f = gf_ref[...] * (1.0 + scf_ref[0])

    def normed_att(rows):
        if att_feature_major:
            a = att_ref[0, :, rows]
            scale = lax.rsqrt(jnp.mean(a * a, axis=0, keepdims=True) + NORM_EPS)
            return (a * scale * gatt_ref[...]).astype(BF16)
        return (_rms(att_ref[0, rows, :]) * gatt_ref[...]).astype(BF16)

    def out_proj(rows, xa):
        if att_feature_major:
            top = lax.dot_general(xa, wo_ref[0:ATT_DIM, :], (((0,), (0,)), ((), ())),
                                  preferred_element_type=F32)
        else:
            top = jnp.dot(xa, wo_ref[0:ATT_DIM, :], preferred_element_type=F32)
        return top + jnp.dot(cn_ref[0, rows, :], wo_ref[ATT_DIM:, :], preferred_element_type=F32)

    def residual1(rows, mixed):
        x1 = x_ref[0, rows, :] + gt1_ref[0] * mixed
        return x1, (_rms(x1) * gain2 + sh2_ref[0]).astype(BF16)

    def mlp(h2):
        total = None
        for c in range(D_FF // ff_chunk):
            up = jnp.dot(h2, wu_ref[:, c * ff_chunk:(c + 1) * ff_chunk], preferred_element_type=F32)
            act = jnp.square(jnp.maximum(up, 0.0)).astype(BF16)
            part = jnp.dot(act, wd_ref[c * ff_chunk:(c + 1) * ff_chunk, :], preferred_element_type=F32)
            total = part if total is None else total + part
        return total

    def finish(rows, x1, m):
        x2 = x1 + gt2_ref[0] * m
        y_ref[0, rows, :] = _rms(x2) * gain_f + shf_ref[0]

    n = len(chunks)
    mixed = {0: out_proj(chunks[0], normed_att(chunks[0]))}
    pending = None
    for c in range(n):
        if c + 1 < n:
            mixed[c + 1] = out_proj(chunks[c + 1], normed_att(chunks[c + 1]))
        x1, h2 = residual1(chunks[c], mixed.pop(c))
        m = mlp(h2)
        if pending is not None:
            finish(*pending)
        pending = (chunks[c], x1, m)
    finish(*pending)


def _post(x, att, cn, mods, g_att, g2, g_final, w_out, w_up, w_down, tile, att_feature_major=False):
    b, s, _ = x.shape
    row_chunk = POST_ROW_CHUNK if tile % POST_ROW_CHUNK == 0 else tile
    tok = lambda n: pl.BlockSpec((1, tile, n), lambda i, j: (i, j, 0))
    att_spec = (pl.BlockSpec((1, ATT_DIM, tile), lambda i, j: (i, 0, j)) if att_feature_major
                else tok(ATT_DIM))
    if mods[0].shape[1] == 1:
        per_b = pl.BlockSpec((1, 1, D_MODEL), lambda i, j: (i, 0, 0))
    else:
        assert row_chunk == tile
        per_b = tok(D_MODEL)
    return pl.pallas_call(
        functools.partial(_post_kernel, ff_chunk=1024, row_chunk=row_chunk,
                          att_feature_major=att_feature_major),
        grid=(b, s // tile),
        in_specs=[tok(D_MODEL), att_spec, tok(CONV_DIM)] + [per_b] * 6
                 + [_const_spec(g_att.shape), _const_spec((1, D_MODEL)), _const_spec((1, D_MODEL)),
                    _const_spec((D_MODEL, D_MODEL)), _const_spec((D_MODEL, D_FF)),
                    _const_spec((D_FF, D_MODEL))],
        out_specs=tok(D_MODEL),
        out_shape=jax.ShapeDtypeStruct((b, s, D_MODEL), F32),
        compiler_params=pltpu.CompilerParams(dimension_semantics=("arbitrary", "arbitrary"),
                                             vmem_limit_bytes=VMEM_LIMIT_BYTES),
        name="post",
    )(x, att, cn, *mods, g_att, g2, g_final, w_out, w_up, w_down)


def _placement():
    pq = np.zeros((LANES, N_HEADS * LANES), np.float32)
    pq_t = np.zeros((ATT_DIM, LANES), np.float32)
    for h in range(N_HEADS):
        base = h * LANES + (HEAD_DIM if h % 2 == 0 else 0)
        for part in range(N_SPLIT):
            src, own = part * N_HEADS + h, E_F0 + part * N_HEADS + h
            pq[src, base + part] = 1.0
            pq[ONES_ROW, base + own] = 1.0
            pq_t[h * HEAD_DIM + part, src] = 1.0
            pq_t[h * HEAD_DIM + own, ONES_ROW] = 1.0
    return jnp.asarray(pq, BF16), jnp.asarray(pq_t, BF16)


def _tri(n):
    return jnp.asarray(np.tri(n, dtype=np.float32), BF16)


def _pack_w_in(w):
    a, c = ATT_DIM, CONV_DIM
    f0 = 3 * a
    b0 = f0 + N_HEADS
    pad = jnp.zeros((D_MODEL, LANES - N_HEADS), w.dtype)
    wq = w[:, :a] * (ATT_SCALE * LOG2E)
    packed = jnp.concatenate([wq, w[:, a:f0], w[:, b0:b0 + 3 * c], w[:, f0:b0], pad], axis=1)
    transposed = jnp.concatenate([wq, w[:, f0:b0], pad[:, :BF16_ROWS - N_HEADS]], axis=1).T
    return packed.astype(BF16), transposed.astype(BF16)


def kernel(x_prompt, x_sample, cache_k, cache_v, cache_logf, cache_conv, c_prompt, c_sample, w_ada, b_ada, g_norm1, g_norm2, w_in, b_f, w_conv, g_attn_out, g_conv_out, w_out, w_up, w_down, w_ada_final, b_ada_final, g_final):
    assert w_ada.shape[0] == 1, "single layer"
    nb, s, _ = x_prompt.shape
    db, t, _ = x_sample.shape
    p = cache_k.shape[2]
    tile_prompt, tile_cache, tq, heads_per_step, tile_post = 512, 512, 512, 8, 512

    c_all = jnp.concatenate([c_prompt, c_sample], axis=0)
    mod = _ada(c_all, w_ada[0], b_ada[0])
    mod_f = _ada(c_all, w_ada_final, b_ada_final)
    sh1, sc1, gt1, sh2, sc2, gt2 = [m[:, None, :] for m in jnp.split(mod, 6, axis=-1)]
    shf, scf = [m[:, None, :] for m in jnp.split(mod_f, 2, axis=-1)]

    w_pack, w_t = _pack_w_in(w_in[0])
    bf_pad = jnp.pad(b_f[0], (0, LANES - N_HEADS)).reshape(1, LANES)
    bf_col = jnp.broadcast_to(b_f[0][:, None], (N_HEADS, LANES))
    pq, pq_t = _placement()
    g1 = g_norm1[0].reshape(1, D_MODEL)
    g2 = g_norm2[0].reshape(1, D_MODEL)
    gf = g_final.reshape(1, D_MODEL)
    g_att = g_attn_out[0].reshape(1, ATT_DIM)
    g_conv = g_conv_out[0].reshape(1, CONV_DIM)
    wo, wu, wd = w_out[0].astype(BF16), w_up[0].astype(BF16), w_down[0].astype(BF16)

    def layer(x, rows, prev, f0, tile, transposed_qv, attend):
        sel = lambda m: m[rows]
        k, v, logf, qa, ka, va, cn, cs = _proj(x, sel(sh1), sel(sc1), g1, w_pack, w_t, bf_pad, bf_col, w_conv[0],
                                               g_conv, prev, f0, _tri(tile), pq, pq_t, tile,
                                               transposed_qv)
        att = attend(qa, ka, va)
        mods = [sel(m) for m in (gt1, sh2, sc2, gt2, shf, scf)]
        bsz, sl = x.shape[0], x.shape[1]
        if sl >= tile_post:
            y = _post(x, att, cn, mods, g_att.reshape(ATT_DIM, 1), g2, gf, wo, wu, wd, tile_post,
                      att_feature_major=True)
        else:
            flat = lambda a: a.reshape(1, bsz * sl, a.shape[-1])
            rows_of = lambda m: flat(jnp.broadcast_to(m, (bsz, sl, D_MODEL)))
            y = _post(flat(x), flat(att), flat(cn), [rows_of(m) for m in mods], g_att, g2, gf,
                      wo, wu, wd, bsz * sl).reshape(bsz, sl, D_MODEL)
        heads = lambda a: a.reshape(1, bsz, sl, N_HEADS, HEAD_DIM)
        if transposed_qv:
            logf = jnp.transpose(logf, (0, 2, 1))
        return y, heads(k), heads(v), logf[None], cs[None]

    zeros_prev = jnp.zeros((nb, CONV_WIDTH - 1, CONV_DIM), F32)
    zeros_f = jnp.zeros((nb, 1, LANES), F32)
    yp, kp, vp, lp, cp = layer(x_prompt, slice(0, nb), zeros_prev, zeros_f, tile_prompt, True,
                               lambda qt, ka, vt: _attn(qt, ka, vt, tq, heads_per_step))

    kt = jnp.transpose(cache_k[0], (0, 2, 3, 1))
    vt = jnp.transpose(cache_v[0], (0, 2, 3, 1))
    fc = _cache_cumsum(jnp.transpose(cache_logf[0], (0, 2, 1)), _tri(tile_cache).T, tile_cache)
    f_tot = jnp.pad(fc[:, :, p - 1], ((0, 0), (0, LANES - N_HEADS)))[:, None, :]
    ys, ks, vs, ls, cs = layer(x_sample, slice(nb, nb + db), cache_conv[0], f_tot, t, False,
                               lambda qa, ka, va: _attn_sample(qa, ka, va, kt, vt, fc))
    return (yp, ys, kp, vp, lp, cp, ks, vs, ls, cs)
```

```python
import functools

import jax
import jax.numpy as jnp
import numpy as np
from jax import lax
from jax.experimental import pallas as pl
from jax.experimental.pallas import tpu as pltpu

F32 = jnp.float32
BF16 = jnp.bfloat16

D_MODEL = 1024
N_HEADS = 8
HEAD_DIM = 64
ATT_DIM = N_HEADS * HEAD_DIM
CONV_DIM = 512
CONV_WIDTH = 3
D_FF = 4 * D_MODEL
NORM_EPS = 1e-6
ATT_SCALE = HEAD_DIM ** -0.5

LANES = 128
CARRY_ROWS = 8
N_SPLIT = 3
ONES_ROW = N_SPLIT * N_HEADS
VMEM_LIMIT_BYTES = 56 * 1024 * 1024
POST_VMEM_LIMIT_BYTES = 62 * 1024 * 1024
BF16_ROWS = 16
SHIFT_ROW = 8
E_F0 = 16
ROW_CHUNKS = 2
POST_ROW_CHUNK = 256
LOG2E = 1.4426950408889634

SEC_Q, SEC_K, SEC_V, SEC_BG, SEC_CG, SEC_U = range(6)
SEC_W = 512
F_COL = 6 * SEC_W
PROJ_COLS = F_COL + LANES


def _const_spec(shape):
    zeros = (0,) * len(shape)
    return pl.BlockSpec(shape, lambda *_: zeros, pipeline_mode=pl.Buffered(1))


def _rms(x):
    return x * lax.rsqrt(jnp.mean(x * x, axis=-1, keepdims=True) + NORM_EPS)


def _log_sigmoid(x):
    return jnp.minimum(x, 0.0) - jnp.log1p(jnp.exp(-jnp.abs(x)))


def _split3(x):
    hi = x.astype(BF16).astype(F32)
    r = x - hi
    mid = r.astype(BF16).astype(F32)
    lo = (r - mid).astype(BF16).astype(F32)
    return hi, mid, lo


def _cumsum_rows(l128, tri_ref, lane):
    hi, mid, lo = _split3(l128)
    cat = hi + pltpu.roll(mid, N_HEADS, axis=1) + pltpu.roll(lo, 2 * N_HEADS, axis=1)
    sums = jnp.dot(tri_ref[...], cat.astype(BF16), preferred_element_type=F32)
    total = ((sums + pltpu.roll(sums, LANES - N_HEADS, axis=1))
             + pltpu.roll(sums, LANES - 2 * N_HEADS, axis=1))
    return jnp.where(lane < N_HEADS, total, 0.0)


def _f_parts(f128):
    hi, mid, lo = _split3(f128 * LOG2E)
    return hi + pltpu.roll(mid, N_HEADS, axis=1) + pltpu.roll(lo, 2 * N_HEADS, axis=1)


def _fcat(parts, lane):
    return (parts + jnp.where(lane == ONES_ROW, 1.0, 0.0)).astype(BF16)


def _k_extras(parts, lane):
    ones = jnp.where((lane < N_SPLIT) | ((lane >= SHIFT_ROW) & (lane < SHIFT_ROW + N_SPLIT)), 1.0, 0.0)
    odd = ones - pltpu.roll(parts, E_F0, axis=1)
    return pltpu.roll(odd, HEAD_DIM, axis=1), odd


def _store_aug(dst_ref, main, extras_even, extras_odd, lane):
    low = lane < HEAD_DIM
    for pair in range(N_HEADS // 2):
        m = main[:, pair * LANES:(pair + 1) * LANES]
        dst_ref[0, 2 * pair] = jnp.where(low, m, extras_even(pair)).astype(BF16)
        dst_ref[0, 2 * pair + 1] = jnp.where(low, extras_odd(pair), m).astype(BF16)


def _store_v_aug(dst_ref, v, lane):
    even = jnp.where(lane == HEAD_DIM, 1.0, 0.0)
    odd = jnp.where(lane == 0, 1.0, 0.0)
    _store_aug(dst_ref, v, lambda pair: even, lambda pair: odd, lane)


def _ada_kernel(c_ref, w_ref, b_ref, o_ref):
    c = c_ref[...]
    sc = (c * jax.nn.sigmoid(c)).astype(BF16)
    o_ref[...] = jnp.dot(sc, w_ref[...].astype(BF16), preferred_element_type=F32) + b_ref[...]


def _ada(c, w, b):
    rows, n = c.shape[0], w.shape[1]
    bn = 1024
    return pl.pallas_call(
        _ada_kernel,
        grid=(n // bn,),
        in_specs=[pl.BlockSpec((rows, D_MODEL), lambda j: (0, 0)),
                  pl.BlockSpec((D_MODEL, bn), lambda j: (0, j)),
                  pl.BlockSpec((1, bn), lambda j: (0, j))],
        out_specs=pl.BlockSpec((rows, bn), lambda j: (0, j)),
        out_shape=jax.ShapeDtypeStruct((rows, n), F32),
        name="ada",
    )(c, w, b.reshape(1, n))


def _store_qv_transposed(qa_ref, va_ref, q_t, v_t, eq_t, tile):
    ones_row = jnp.where(lax.broadcasted_iota(jnp.int32, (HEAD_DIM, tile), 0) == 0, 1.0, 0.0)
    for h in range(N_HEADS):
        rows = slice(h * HEAD_DIM, (h + 1) * HEAD_DIM)
        parts = [q_t[rows], eq_t[rows]]
        qa_ref[0, h] = jnp.concatenate(parts if h % 2 == 0 else parts[::-1], axis=0).astype(BF16)
        va_ref[0, h] = jnp.concatenate([v_t[rows], ones_row], axis=0).astype(BF16)


def _proj_kernel(x_ref, sh_ref, sc_ref, g1_ref, w_ref, wt_ref, bf_ref, bfc_ref, wconv_ref, gconv_ref, prev_ref,
                 f0_ref, tri_ref, pq_ref, pqt_ref,
                 k_ref, v_ref, logf_ref, qa_ref, ka_ref, va_ref, cn_ref, cs_ref,
                 ue_ref, fc_ref, *, tile, transposed_qv):
    s = pl.program_id(1)

    @pl.when(s == 0)
    def _():
        ue_ref[0:CARRY_ROWS, :] = jnp.zeros((CARRY_ROWS, CONV_DIM), F32)
        ue_ref[CARRY_ROWS - (CONV_WIDTH - 1):CARRY_ROWS, :] = prev_ref[0]
        fc_ref[...] = f0_ref[0]

    lane = lax.broadcasted_iota(jnp.int32, (tile, LANES), 1)
    nt = (((1,), (1,)), ((), ()))
    gain = g1_ref[...] * (1.0 + sc_ref[0])
    shift = sh_ref[0]

    chunk = tile // ROW_CHUNKS if tile % (ROW_CHUNKS * BF16_ROWS) == 0 else tile
    starts = list(range(0, tile, chunk))

    def conv_matmuls(r0):
        hb_c = (_rms(x_ref[0, r0:r0 + chunk, :]) * gain + shift).astype(BF16)
        sec = lambda i: jnp.dot(hb_c, w_ref[:, i * SEC_W:(i + 1) * SEC_W], preferred_element_type=F32)
        return hb_c, sec(SEC_CG), sec(SEC_U), sec(SEC_BG)

    def conv_elementwise(r0, cg, u, bg):
        up = cg * u
        base = CARRY_ROWS + r0
        ue_ref[base:base + chunk, :] = up
        cv = (wconv_ref[0:1, :] * ue_ref[base - 2:base - 2 + chunk, :]
              + wconv_ref[1:2, :] * ue_ref[base - 1:base - 1 + chunk, :]
              + wconv_ref[2:3, :] * up)
        cn_ref[0, r0:r0 + chunk, :] = (_rms(bg * cv) * gconv_ref[...]).astype(BF16)

    hbs, pending = [], None
    for r0 in starts:
        hb_c, cg, u, bg = conv_matmuls(r0)
        hbs.append(hb_c)
        if pending is not None:
            conv_elementwise(*pending)
        pending = (r0, cg, u, bg)
    hb = jnp.concatenate(hbs, axis=0) if len(hbs) > 1 else hbs[0]

    def proj(sec, width=SEC_W):
        return jnp.dot(hb, w_ref[:, sec * SEC_W:sec * SEC_W + width], preferred_element_type=F32)

    if transposed_qv:
        qf_t = lax.dot_general(wt_ref[...], hb, nt, preferred_element_type=F32)
        q_t, fl_t = qf_t[:ATT_DIM], qf_t[ATT_DIM:]
        fl = jnp.concatenate([fl_t, jnp.zeros((LANES - BF16_ROWS, tile), F32)], axis=0).T
    else:
        fl = proj(6, LANES)
    k = proj(SEC_K)
    k_ref[0] = k
    conv_elementwise(*pending)
    ue_ref[0:CARRY_ROWS, :] = ue_ref[tile:tile + CARRY_ROWS, :]

    logf = jnp.where(lane < N_HEADS, _log_sigmoid(fl + bf_ref[...]), 0.0)
    if transposed_qv:
        logf_ref[0] = _log_sigmoid(fl_t[0:N_HEADS] + bfc_ref[:, 0:1])
    else:
        logf_ref[0] = logf[:, :N_HEADS]
    f128 = _cumsum_rows(logf, tri_ref, lane) + fc_ref[...]
    v = proj(SEC_V)
    v_ref[0] = v
    fc_ref[...] = f128[tile - 1:tile, :]
    parts = _f_parts(f128)
    fcat = _fcat(parts, lane)

    ke_even, ke_odd = _k_extras(parts, lane)
    _store_aug(ka_ref, k, lambda pair: ke_even, lambda pair: ke_odd, lane)
    if transposed_qv:
        eq_t = lax.dot_general(pqt_ref[...], fcat, nt, preferred_element_type=F32)
        _store_qv_transposed(qa_ref, va_ref, q_t, v.T, eq_t, tile)
    else:
        eq = jnp.dot(fcat, pq_ref[...], preferred_element_type=F32)
        _store_aug(qa_ref, proj(SEC_Q), lambda pair: eq[:, 2 * pair * LANES:(2 * pair + 1) * LANES],
                   lambda pair: eq[:, (2 * pair + 1) * LANES:(2 * pair + 2) * LANES], lane)
        _store_v_aug(va_ref, v, lane)

    @pl.when(s == pl.num_programs(1) - 1)
    def _():
        cs_ref[0] = ue_ref[CARRY_ROWS + tile - (CONV_WIDTH - 1):CARRY_ROWS + tile, :]


def _proj(x, sh, sc, g1, w_pack, w_t, bf_pad, bf_col, w_conv, g_conv, prev, f0, tri, pq, pq_t, tile,
          transposed_qv):
    b, s, _ = x.shape
    assert s % tile == 0 and tile % CARRY_ROWS == 0
    ns = s // tile
    tok = lambda n: pl.BlockSpec((1, tile, n), lambda i, j: (i, j, 0))
    per_b = lambda r, n: pl.BlockSpec((1, r, n), lambda i, j: (i, 0, 0))
    aug = pl.BlockSpec((1, N_HEADS, tile, LANES), lambda i, j: (i, 0, j, 0))
    aug_sds = jax.ShapeDtypeStruct((b, N_HEADS, s, LANES), BF16)
    if transposed_qv:
        qv = pl.BlockSpec((1, N_HEADS, LANES, tile), lambda i, j: (i, 0, 0, j))
        qv_sds = jax.ShapeDtypeStruct((b, N_HEADS, LANES, s), BF16)
        lf = pl.BlockSpec((1, N_HEADS, tile), lambda i, j: (i, 0, j))
        lf_sds = jax.ShapeDtypeStruct((b, N_HEADS, s), F32)
    else:
        qv, qv_sds = aug, aug_sds
        lf, lf_sds = tok(N_HEADS), jax.ShapeDtypeStruct((b, s, N_HEADS), F32)
    out_shape = (
        jax.ShapeDtypeStruct((b, s, ATT_DIM), F32),
        jax.ShapeDtypeStruct((b, s, ATT_DIM), F32),
        lf_sds,
        qv_sds,
        aug_sds,
        qv_sds,
        jax.ShapeDtypeStruct((b, s, CONV_DIM), BF16),
        jax.ShapeDtypeStruct((b, CONV_WIDTH - 1, CONV_DIM), F32),
    )
    return pl.pallas_call(
        functools.partial(_proj_kernel, tile=tile, transposed_qv=transposed_qv),
        grid=(b, ns),
        in_specs=[tok(D_MODEL), per_b(1, D_MODEL), per_b(1, D_MODEL), _const_spec((1, D_MODEL)),
                  _const_spec((D_MODEL, PROJ_COLS)), _const_spec((ATT_DIM + BF16_ROWS, D_MODEL)),
                  _const_spec((1, LANES)), _const_spec((N_HEADS, LANES)),
                  _const_spec((CONV_WIDTH, CONV_DIM)), _const_spec((1, CONV_DIM)),
                  per_b(CONV_WIDTH - 1, CONV_DIM), per_b(1, LANES),
                  _const_spec((tile, tile)), _const_spec((LANES, N_HEADS * LANES)),
                  _const_spec((ATT_DIM, LANES))],
        out_specs=(tok(ATT_DIM), tok(ATT_DIM), lf, qv, aug, qv, tok(CONV_DIM),
                   per_b(CONV_WIDTH - 1, CONV_DIM)),
        out_shape=out_shape,
        scratch_shapes=[pltpu.VMEM((tile + CARRY_ROWS, CONV_DIM), F32), pltpu.VMEM((1, LANES), F32)],
        compiler_params=pltpu.CompilerParams(dimension_semantics=("arbitrary", "arbitrary"),
                                             vmem_limit_bytes=VMEM_LIMIT_BYTES),
        name="proj",
    )(x, sh, sc, g1, w_pack, w_t, bf_pad, bf_col, w_conv, g_conv, prev, f0, tri, pq, pq_t)


def _cache_cumsum_kernel(l_ref, triu_ref, f_ref, *, tile):
    n = l_ref.shape[2] // tile
    rows = []
    for c in range(n):
        rows.extend(_split3(l_ref[0, :, c * tile:(c + 1) * tile]))
    sums = jnp.dot(jnp.concatenate(rows, axis=0).astype(BF16), triu_ref[...],
                   preferred_element_type=F32)
    carry = jnp.zeros((N_HEADS, 1), F32)
    for c in range(n):
        hi, mid, lo = [sums[(N_SPLIT * c + i) * N_HEADS:(N_SPLIT * c + i + 1) * N_HEADS]
                       for i in range(N_SPLIT)]
        f = ((hi + mid) + lo) + carry
        f_ref[0, :, c * tile:(c + 1) * tile] = f
        carry = f[:, tile - 1:tile]


def _cache_cumsum(cl_t, triu, tile):
    b, h, p = cl_t.shape
    blk = pl.BlockSpec((1, h, p), lambda i: (i, 0, 0))
    return pl.pallas_call(
        functools.partial(_cache_cumsum_kernel, tile=tile),
        grid=(b,),
        in_specs=[blk, _const_spec((tile, tile))],
        out_specs=blk,
        out_shape=jax.ShapeDtypeStruct((b, h, p), F32),
        compiler_params=pltpu.CompilerParams(dimension_semantics=("arbitrary",)),
        name="cache_cumsum",
    )(cl_t, triu)


def _softmax_block_t(qt, k, vt, m, acc, mask):
    s = jnp.dot(k, qt, preferred_element_type=F32)
    if mask is not None:
        s = jnp.where(mask, s, -jnp.inf)
    m_new = jnp.maximum(m, jnp.max(s, axis=0, keepdims=True))
    p = jnp.exp2(s - m_new)
    acc = acc * jnp.exp2(m - m_new) + jnp.dot(vt, p.astype(BF16), preferred_element_type=F32)
    return m_new, acc


def _head_online(qt, kv_block, qi, causal, tq):
    def body(j, carry):
        k, vt = kv_block(j)
        return _softmax_block_t(qt, k, vt, carry[0], carry[1], None)

    init = (jnp.full((1, tq), -jnp.inf, F32), jnp.zeros((LANES, tq), F32))
    m, acc = lax.fori_loop(0, qi, body, init)
    k, vt = kv_block(qi)
    return _softmax_block_t(qt, k, vt, m, acc, causal)[1]


def _with_shift_rows(qt, m, parity):
    g0 = HEAD_DIM if parity == 0 else 0
    hi, mid, lo = _split3(-m)
    row = lax.broadcasted_iota(jnp.int32, (BF16_ROWS, m.shape[1]), 0)
    add = jnp.where(row == SHIFT_ROW, hi,
                    jnp.where(row == SHIFT_ROW + 1, mid, jnp.where(row == SHIFT_ROW + 2, lo, 0.0)))
    grp = (qt[g0:g0 + BF16_ROWS].astype(F32) + add).astype(BF16)
    pieces = ([qt[:g0]] if g0 else []) + [grp, qt[g0 + BF16_ROWS:]]
    return jnp.concatenate(pieces, axis=0)


def _heads_fixed_max(qts, kv_block, qi, causal, acc_ref, qs_ref):
    n = len(qts)
    dot = functools.partial(jnp.dot, preferred_element_type=F32)
    half = causal.shape[0]

    def diag(hh):
        k, qt = kv_block(hh, qi)[0], qts[hh]
        s_a = dot(k[:half], qt)
        s_a = jnp.concatenate([jnp.where(causal, s_a[:, :half], -jnp.inf), s_a[:, half:]], axis=1)
        return s_a, jnp.where(causal, dot(k[half:], qt[:, half:]), -jnp.inf)

    s_next = diag(0)
    for hh in range(n):
        (s_a, s_b), s_next = s_next, (diag(hh + 1) if hh + 1 < n else None)
        m_a = jnp.max(s_a, axis=0, keepdims=True)
        m_late = jnp.maximum(m_a[:, half:], jnp.max(s_b, axis=0, keepdims=True))
        m = jnp.concatenate([m_a[:, :half], m_late], axis=1)
        qs_ref[hh] = _with_shift_rows(qts[hh], m, hh % 2)
        vt = kv_block(hh, qi)[1]
        acc = dot(vt[:, :half], jnp.exp2(s_a - m).astype(BF16))
        late = acc[:, half:] + dot(vt[:, half:], jnp.exp2(s_b - m_late).astype(BF16))
        acc_ref[hh] = jnp.concatenate([acc[:, :half], late], axis=1)

    def run(blocks):
        items = [(j, hh) for j in blocks for hh in range(n)]
        scores = lambda item: dot(kv_block(item[1], item[0])[0], qs_ref[item[1]])
        s_next = scores(items[0])
        for idx, (j, hh) in enumerate(items):
            s, s_next = s_next, (scores(items[idx + 1]) if idx + 1 < len(items) else None)
            acc_ref[hh] += dot(kv_block(hh, j)[1], jnp.exp2(s).astype(BF16))

    def body(i, carry):
        run([2 * i, 2 * i + 1])
        return carry

    lax.fori_loop(0, qi // 2, body, 0)

    @pl.when(qi % 2 == 1)
    def _():
        run([qi - 1])

    return [acc_ref[hh] for hh in range(n)]


def _attn_kernel(qt_ref, k_ref, vt_ref, o_ref, acc_ref, qs_ref, *, tq):
    qi = pl.program_id(2)

    def causal_mask(n):
        return (lax.broadcasted_iota(jnp.int32, (n, n), 0)
                <= lax.broadcasted_iota(jnp.int32, (n, n), 1))

    def kv_block(hh, j):
        off = pl.multiple_of(j * tq, tq)
        return k_ref[0, hh, pl.ds(off, tq), :], vt_ref[0, hh, :, pl.ds(off, tq)]

    heads = qt_ref.shape[1]
    qts = [qt_ref[0, hh] for hh in range(heads)]

    def finish(accs):
        norm = lambda acc: acc[0:HEAD_DIM] * (1.0 / acc[HEAD_DIM:HEAD_DIM + 1])
        finite = None
        for pair in range(heads // 2):
            both = jnp.concatenate([norm(accs[2 * pair]), norm(accs[2 * pair + 1])], axis=0)
            o_ref[0, :, pair * LANES:(pair + 1) * LANES] = both.T
            ok = jnp.where(jnp.isfinite(both), 1.0, 0.0)
            finite = ok if finite is None else jnp.minimum(finite, ok)
        return jnp.min(finite) > 0.5

    all_finite = finish(_heads_fixed_max(qts, kv_block, qi, causal_mask(tq // 2), acc_ref, qs_ref))

    @pl.when(jnp.logical_not(all_finite))
    def _():
        finish([_head_online(qts[hh], functools.partial(kv_block, hh), qi, causal_mask(tq), tq)
                for hh in range(heads)])


def _attn(qt, ka, vt, tq, heads):
    b, _, s, _ = ka.shape
    assert heads % 2 == 0 and N_HEADS % heads == 0
    return pl.pallas_call(
        functools.partial(_attn_kernel, tq=tq),
        grid=(b, N_HEADS // heads, s // tq),
        in_specs=[pl.BlockSpec((1, heads, LANES, tq), lambda i, p, j: (i, p, 0, j)),
                  pl.BlockSpec((1, heads, s, LANES), lambda i, p, j: (i, p, 0, 0)),
                  pl.BlockSpec((1, heads, LANES, s), lambda i, p, j: (i, p, 0, 0))],
        out_specs=pl.BlockSpec((1, tq, heads * HEAD_DIM), lambda i, p, j: (i, j, p)),
        out_shape=jax.ShapeDtypeStruct((b, s, ATT_DIM), F32),
        scratch_shapes=[pltpu.VMEM((heads, LANES, tq), F32), pltpu.VMEM((heads, LANES, tq), BF16)],
        compiler_params=pltpu.CompilerParams(
            dimension_semantics=("arbitrary", "arbitrary", "arbitrary"),
            vmem_limit_bytes=VMEM_LIMIT_BYTES),
        name="attn",
    )(qt, ka, vt)


def _attn_sample_kernel(q_ref, kn_ref, vn_ref, kt_ref, vt_ref, fc_ref, o_ref, *, t):
    nt = (((1,), (1,)), ((), ()))
    causal = (lax.broadcasted_iota(jnp.int32, (t, LANES), 1)
              <= lax.broadcasted_iota(jnp.int32, (t, LANES), 0))
    pad = jnp.zeros((LANES - t, LANES), BF16)
    fk = fc_ref[0] * LOG2E
    outs = []
    for h in range(N_HEADS):
        head, extras = (0, HEAD_DIM) if h % 2 == 0 else (HEAD_DIM, 0)
        tile = q_ref[0, h]
        ex = tile[:, extras:extras + N_SPLIT].astype(F32)
        fq = (ex[:, 0:1] + ex[:, 1:2]) + ex[:, 2:3]
        s_old = jnp.dot(tile[:, head:head + HEAD_DIM], kt_ref[0, h].astype(BF16),
                        preferred_element_type=F32) + (fq - fk[h:h + 1, :])
        kn = jnp.concatenate([kn_ref[0, h], pad], axis=0)
        vn = jnp.concatenate([vn_ref[0, h], pad], axis=0)
        s_new = jnp.where(causal, lax.dot_general(tile, kn, nt, preferred_element_type=F32), -jnp.inf)
        m = jnp.maximum(jnp.max(s_old, axis=1, keepdims=True), jnp.max(s_new, axis=1, keepdims=True))
        p_old = jnp.exp2(s_old - m)
        acc_new = jnp.dot(jnp.exp2(s_new - m).astype(BF16), vn, preferred_element_type=F32)
        o_old = lax.dot_general(p_old.astype(BF16), vt_ref[0, h].astype(BF16), nt,
                                preferred_element_type=F32)
        denom = jnp.sum(p_old, axis=1, keepdims=True) + acc_new[:, extras:extras + 1]
        outs.append((o_old + acc_new[:, head:head + HEAD_DIM]) * (1.0 / denom))
    o_ref[0] = jnp.concatenate(outs, axis=1)


def _attn_sample(qa, kn, vn, kt, vt, fc):
    b, _, t, _ = qa.shape
    p = kt.shape[3]
    new = pl.BlockSpec((1, N_HEADS, t, LANES), lambda i: (i, 0, 0, 0))
    old = pl.BlockSpec((1, N_HEADS, HEAD_DIM, p), lambda i: (i, 0, 0, 0))
    return pl.pallas_call(
        functools.partial(_attn_sample_kernel, t=t),
        grid=(b,),
        in_specs=[new, new, new, old, old, pl.BlockSpec((1, N_HEADS, p), lambda i: (i, 0, 0))],
        out_specs=pl.BlockSpec((1, t, ATT_DIM), lambda i: (i, 0, 0)),
        out_shape=jax.ShapeDtypeStruct((b, t, ATT_DIM), F32),
        compiler_params=pltpu.CompilerParams(dimension_semantics=("arbitrary",),
                                             vmem_limit_bytes=VMEM_LIMIT_BYTES),
        name="attn_sample",
    )(qa, kn, vn, kt, vt, fc)


def _post_kernel(x_ref, att_ref, cn_ref, gt1_ref, sh2_ref, sc2_ref, gt2_ref, shf_ref, scf_ref,
                 gatt_ref, g2_ref, gf_ref, wo_ref, wu_ref, wd_ref, y_ref, *, ff_chunk, row_chunk):
    tile = x_ref.shape[1]
    chunks = [slice(r0, r0 + row_chunk) for r0 in range(0, tile, row_chunk)]
    gain2 = g2_ref[...] * (1.0 + sc2_ref[0])
    gain_f = gf_ref[...] * (1.0 + scf_ref[0])

    def normed_att(rows):
        return (_rms(att_ref[0, rows, :]) * gatt_ref[...]).astype(BF16)

    def out_proj(rows, xa):
        return (jnp.dot(xa, wo_ref[0:ATT_DIM, :], preferred_element_type=F32)
                + jnp.dot(cn_ref[0, rows, :], wo_ref[ATT_DIM:, :], preferred_element_type=F32))

    def residual1(rows, mixed):
        x1 = x_ref[0, rows, :] + gt1_ref[0] * mixed
        return x1, (_rms(x1) * gain2 + sh2_ref[0]).astype(BF16)

    def mlp(h2):
        total = None
        for c in range(D_FF // ff_chunk):
            up = jnp.dot(h2, wu_ref[:, c * ff_chunk:(c + 1) * ff_chunk], preferred_element_type=F32)
            act = jnp.square(jnp.maximum(up, 0.0)).astype(BF16)
            part = jnp.dot(act, wd_ref[c * ff_chunk:(c + 1) * ff_chunk, :], preferred_element_type=F32)
            total = part if total is None else total + part
        return total

    def finish(rows, x1, m):
        x2 = x1 + gt2_ref[0] * m
        y_ref[0, rows, :] = _rms(x2) * gain_f + shf_ref[0]

    n = len(chunks)
    mixed = {0: out_proj(chunks[0], normed_att(chunks[0]))}
    pending = None
    for c in range(n):
        if c + 1 < n:
            mixed[c + 1] = out_proj(chunks[c + 1], normed_att(chunks[c + 1]))
        x1, h2 = residual1(chunks[c], mixed.pop(c))
        m = mlp(h2)
        if pending is not None:
            finish(*pending)
        pending = (chunks[c], x1, m)
    finish(*pending)


def _post(x, att, cn, mods, g_att, g2, g_final, w_out, w_up, w_down, tile):
    b, s, _ = x.shape
    row_chunk = POST_ROW_CHUNK if tile % POST_ROW_CHUNK == 0 else tile
    tok = lambda n: pl.BlockSpec((1, tile, n), lambda i, j: (i, j, 0))
    if mods[0].shape[1] == 1:
        per_b = pl.BlockSpec((1, 1, D_MODEL), lambda i, j: (i, 0, 0))
    else:
        assert row_chunk == tile
        per_b = tok(D_MODEL)
    return pl.pallas_call(
        functools.partial(_post_kernel, ff_chunk=1024, row_chunk=row_chunk),
        grid=(b, s // tile),
        in_specs=[tok(D_MODEL), tok(ATT_DIM), tok(CONV_DIM)] + [per_b] * 6
                 + [_const_spec((1, ATT_DIM)), _const_spec((1, D_MODEL)), _const_spec((1, D_MODEL)),
                    _const_spec((D_MODEL, D_MODEL)), _const_spec((D_MODEL, D_FF)),
                    _const_spec((D_FF, D_MODEL))],
        out_specs=tok(D_MODEL),
        out_shape=jax.ShapeDtypeStruct((b, s, D_MODEL), F32),
        compiler_params=pltpu.CompilerParams(dimension_semantics=("arbitrary", "arbitrary"),
                                             vmem_limit_bytes=POST_VMEM_LIMIT_BYTES),
        name="post",
    )(x, att, cn, *mods, g_att, g2, g_final, w_out, w_up, w_down)


def _placement():
    pq = np.zeros((LANES, N_HEADS * LANES), np.float32)
    pq_t = np.zeros((ATT_DIM, LANES), np.float32)
    for h in range(N_HEADS):
        base = h * LANES + (HEAD_DIM if h % 2 == 0 else 0)
        for part in range(N_SPLIT):
            src, own = part * N_HEADS + h, E_F0 + part * N_HEADS + h
            pq[src, base + part] = 1.0
            pq[ONES_ROW, base + own] = 1.0
            pq_t[h * HEAD_DIM + part, src] = 1.0
            pq_t[h * HEAD_DIM + own, ONES_ROW] = 1.0
    return jnp.asarray(pq, BF16), jnp.asarray(pq_t, BF16)


def _tri(n):
    return jnp.asarray(np.tri(n, dtype=np.float32), BF16)


def _pack_w_in(w):
    a, c = ATT_DIM, CONV_DIM
    f0 = 3 * a
    b0 = f0 + N_HEADS
    pad = jnp.zeros((D_MODEL, LANES - N_HEADS), w.dtype)
    wq = w[:, :a] * (ATT_SCALE * LOG2E)
    packed = jnp.concatenate([wq, w[:, a:f0], w[:, b0:b0 + 3 * c], w[:, f0:b0], pad], axis=1)
    transposed = jnp.concatenate([wq, w[:, f0:b0], pad[:, :BF16_ROWS - N_HEADS]], axis=1).T
    return packed.astype(BF16), transposed.astype(BF16)


def kernel(x_prompt, x_sample, cache_k, cache_v, cache_logf, cache_conv, c_prompt, c_sample, w_ada, b_ada, g_norm1, g_norm2, w_in, b_f, w_conv, g_attn_out, g_conv_out, w_out, w_up, w_down, w_ada_final, b_ada_final, g_final):
    assert w_ada.shape[0] == 1, "single layer"
    nb, s, _ = x_prompt.shape
    db, t, _ = x_sample.shape
    p = cache_k.shape[2]
    tile_prompt, tile_cache, tq, heads_per_step, tile_post = 512, 512, 512, 8, 1024

    c_all = jnp.concatenate([c_prompt, c_sample], axis=0)
    mod = _ada(c_all, w_ada[0], b_ada[0])
    mod_f = _ada(c_all, w_ada_final, b_ada_final)
    sh1, sc1, gt1, sh2, sc2, gt2 = [m[:, None, :] for m in jnp.split(mod, 6, axis=-1)]
    shf, scf = [m[:, None, :] for m in jnp.split(mod_f, 2, axis=-1)]

    w_pack, w_t = _pack_w_in(w_in[0])
    bf_pad = jnp.pad(b_f[0], (0, LANES - N_HEADS)).reshape(1, LANES)
    bf_col = jnp.broadcast_to(b_f[0][:, None], (N_HEADS, LANES))
    pq, pq_t = _placement()
    g1 = g_norm1[0].reshape(1, D_MODEL)
    g2 = g_norm2[0].reshape(1, D_MODEL)
    gf = g_final.reshape(1, D_MODEL)
    g_att = g_attn_out[0].reshape(1, ATT_DIM)
    g_conv = g_conv_out[0].reshape(1, CONV_DIM)
    wo, wu, wd = w_out[0].astype(BF16), w_up[0].astype(BF16), w_down[0].astype(BF16)

    def layer(x, rows, prev, f0, tile, transposed_qv, attend):
        sel = lambda m: m[rows]
        k, v, logf, qa, ka, va, cn, cs = _proj(x, sel(sh1), sel(sc1), g1, w_pack, w_t, bf_pad, bf_col, w_conv[0],
                                               g_conv, prev, f0, _tri(tile), pq, pq_t, tile,
                                               transposed_qv)
        att = attend(qa, ka, va)
        mods = [sel(m) for m in (gt1, sh2, sc2, gt2, shf, scf)]
        bsz, sl = x.shape[0], x.shape[1]
        if sl >= tile_post:
            y = _post(x, att, cn, mods, g_att, g2, gf, wo, wu, wd, tile_post)
        else:
            flat = lambda a: a.reshape(1, bsz * sl, a.shape[-1])
            rows_of = lambda m: flat(jnp.broadcast_to(m, (bsz, sl, D_MODEL)))
            y = _post(flat(x), flat(att), flat(cn), [rows_of(m) for m in mods], g_att, g2, gf,
                      wo, wu, wd, bsz * sl).reshape(bsz, sl, D_MODEL)
        heads = lambda a: a.reshape(1, bsz, sl, N_HEADS, HEAD_DIM)
        if transposed_qv:
            logf = jnp.transpose(logf, (0, 2, 1))
        return y, heads(k), heads(v), logf[None], cs[None]

    zeros_prev = jnp.zeros((nb, CONV_WIDTH - 1, CONV_DIM), F32)
    zeros_f = jnp.zeros((nb, 1, LANES), F32)
    yp, kp, vp, lp, cp = layer(x_prompt, slice(0, nb), zeros_prev, zeros_f, tile_prompt, True,
                               lambda qt, ka, vt: _attn(qt, ka, vt, tq, heads_per_step))

    kt = jnp.transpose(cache_k[0], (0, 2, 3, 1))
    vt = jnp.transpose(cache_v[0], (0, 2, 3, 1))
    fc = _cache_cumsum(jnp.transpose(cache_logf[0], (0, 2, 1)), _tri(tile_cache).T, tile_cache)
    f_tot = jnp.pad(fc[:, :, p - 1], ((0, 0), (0, LANES - N_HEADS)))[:, None, :]
    ys, ks, vs, ls, cs = layer(x_sample, slice(nb, nb + db), cache_conv[0], f_tot, t, False,
                               lambda qa, ka, va: _attn_sample(qa, ka, va, kt, vt, fc))
    return (yp, ys, kp, vp, lp, cp, ks, vs, ls, cs)
```
